```python
import jax
import jax.numpy as jnp
from jax import lax
import numpy as np

D_MODEL = 2048
BATCH = 4
SEQ = 4096
DEPTH = 2

ATTN_HEAD_DIM = 64
ATTN_HEADS = (D_MODEL // 2) // ATTN_HEAD_DIM
ATTN_WIDTH = ATTN_HEADS * ATTN_HEAD_DIM
DILATED_PATTERNS = ((128, 1), (512, 4), (2048, 16))
ATTN_BLOCK = 128
NUM_BUCKETS = 32
REL_MAX_DIST = 2048
MLSTM_HEADS = 4
MLSTM_WIDTH = D_MODEL - ATTN_WIDTH
MLSTM_V_DIM = MLSTM_WIDTH // MLSTM_HEADS
MLSTM_QK_DIM = MLSTM_V_DIM // 2
MLSTM_CHUNK = 128
CONV_WIDTH = 4
MIX_WIDTH = ATTN_WIDTH + MLSTM_WIDTH
IN_SPLITS = (ATTN_WIDTH, ATTN_WIDTH, ATTN_WIDTH,
             MLSTM_HEADS * MLSTM_QK_DIM, MLSTM_HEADS * MLSTM_QK_DIM,
             MLSTM_WIDTH, MLSTM_WIDTH, MLSTM_HEADS, MLSTM_HEADS)
IN_WIDTH = 3 * ATTN_WIDTH + 2 * MLSTM_HEADS * MLSTM_QK_DIM + 2 * MLSTM_WIDTH + 2 * MLSTM_HEADS
MOE_GROUPS = 4
MOE_EXPERTS_PER_GROUP = 8
N_EXPERTS = MOE_GROUPS * MOE_EXPERTS_PER_GROUP
MOE_TOP_K = 2
EXPERT_FF = D_MODEL // 2
MOE_BLOCK = 128
NORM_EPS = 1e-6

kernel_name = "hybrid_dilated_attn_mlstm_hier_moe"


def rms_norm(x, w):
    xf = x.astype(jnp.float32)
    return xf * lax.rsqrt(jnp.mean(xf * xf, axis=-1, keepdims=True) + NORM_EPS) * w.astype(jnp.float32)


def t5_causal_bucket(dist):
    max_exact = NUM_BUCKETS // 2
    d = np.maximum(dist, 1).astype(np.float32)
    large = max_exact + (np.log(d / max_exact) / np.log(REL_MAX_DIST / max_exact)
                         * (NUM_BUCKETS - max_exact)).astype(np.int32)
    large = np.minimum(large, NUM_BUCKETS - 1)
    return np.where(dist < max_exact, dist, large).astype(np.int32)


def dilated_branch(q, k, v, rel_bias, window, dilation):
    B, S, H, dh = q.shape
    L = S // dilation
    win = window // dilation
    blk = ATTN_BLOCK
    nb = -(-L // blk)
    Lp = nb * blk

    def to_strided(t):
        return t.reshape(B, L, dilation, H, dh).transpose(0, 2, 3, 1, 4)

    qs, ks, vs = to_strided(q), to_strided(k), to_strided(v)
    qb = jnp.pad(qs, ((0, 0), (0, 0), (0, 0), (0, Lp - L), (0, 0))).reshape(B, dilation, H, nb, blk, dh)
    kpad = ((0, 0), (0, 0), (0, 0), (blk, Lp - L), (0, 0))
    kb = jnp.pad(ks, kpad).reshape(B, dilation, H, nb + 1, blk, dh)
    vb = jnp.pad(vs, kpad).reshape(B, dilation, H, nb + 1, blk, dh)
    kwin = jnp.concatenate([kb[:, :, :, :-1], kb[:, :, :, 1:]], axis=-2)
    vwin = jnp.concatenate([vb[:, :, :, :-1], vb[:, :, :, 1:]], axis=-2)

    qi = np.arange(blk)[:, None]
    kj = np.arange(2 * blk)[None, :]
    rel = qi - kj + blk
    blk_id = np.arange(nb)[:, None, None]
    valid = (rel >= 0) & (rel <= win) & (blk_id * blk + kj - blk >= 0)
    bucket = t5_causal_bucket(np.clip(rel, 0, win) * dilation)
    bias = jnp.transpose(rel_bias[bucket], (2, 0, 1)).astype(jnp.float32)

    s = jnp.einsum('bdhnqe,bdhnke->bdhnqk', qb, kwin) * (dh ** -0.5) + bias[:, None]
    s = jnp.where(valid, s, -1e30)
    m = jnp.max(s, axis=-1, keepdims=True)
    p = jnp.exp(s - m)
    l = jnp.sum(p, axis=-1, keepdims=True)
    o = jnp.einsum('bdhnqk,bdhnke->bdhnqe', p, vwin) / l
    lse = (m + jnp.log(l))[..., 0]

    o = o.reshape(B, dilation, H, Lp, dh)[:, :, :, :L].transpose(0, 3, 1, 2, 4).reshape(B, S, H, dh)
    lse = lse.reshape(B, dilation, H, Lp)[:, :, :, :L].transpose(0, 3, 1, 2).reshape(B, S, H)
    return o, lse


def dilated_mixture_attention(q, k, v, rel_bias):
    outs, lses = [], []
    for window, dilation in DILATED_PATTERNS:
        o, lse = dilated_branch(q, k, v, rel_bias, window, dilation)
        outs.append(o)
        lses.append(lse)
    alpha = jax.nn.softmax(jnp.stack(lses, axis=0), axis=0)
    return jnp.sum(alpha[..., None] * jnp.stack(outs, axis=0), axis=0)


def causal_depthwise_conv(x, w):
    K, C = w.shape
    return lax.conv_general_dilated(x, w[:, None, :], window_strides=(1,), padding=[(K - 1, 0)],
                                    dimension_numbers=('NWC', 'WIO', 'NWC'), feature_group_count=C)


def mlstm_chunkwise(q, k, v, i_pre, f_pre):
    B, S, H, dk = q.shape
    dv = v.shape[-1]
    ch = MLSTM_CHUNK
    nc = S // ch
    k = k * (dk ** -0.5)

    def chunk(t):
        return t.reshape(B, nc, ch, H, -1).transpose(0, 3, 1, 2, 4)

    qc, kc, vc = chunk(q), chunk(k), chunk(v)
    ig = i_pre.reshape(B, nc, ch, H).transpose(0, 3, 1, 2)
    lf = jax.nn.log_sigmoid(f_pre).reshape(B, nc, ch, H).transpose(0, 3, 1, 2)
    b = jnp.cumsum(lf, axis=-1)
    g = b[..., -1]
    a = g[..., None] - b + ig

    def step(carry, inp):
        C, n, m = carry
        kk, vv, aa, gg = inp
        m_new = jnp.maximum(gg + m, jnp.max(aa, axis=-1))
        decay = jnp.exp(gg + m - m_new)
        w = jnp.exp(aa - m_new[..., None])
        C_new = decay[..., None, None] * C + jnp.einsum('bhsv,bhsk->bhvk', w[..., None] * vv, kk)
        n_new = decay[..., None] * n + jnp.einsum('bhs,bhsk->bhk', w, kk)
        return (C_new, n_new, m_new), (C, n, m)

    init = (jnp.zeros((B, H, dv, dk), jnp.float32), jnp.zeros((B, H, dk), jnp.float32),
            jnp.zeros((B, H), jnp.float32))
    xs = (jnp.moveaxis(kc, 2, 0), jnp.moveaxis(vc, 2, 0), jnp.moveaxis(a, 2, 0), jnp.moveaxis(g, 2, 0))
    _, (Cs, ns, ms) = lax.scan(step, init, xs)
    Cs = jnp.moveaxis(Cs, 0, 2)
    ns = jnp.moveaxis(ns, 0, 2)
    ms = jnp.moveaxis(ms, 0, 2)

    causal = np.tril(np.ones((ch, ch), dtype=bool))
    Dlog = jnp.where(causal, b[..., :, None] - b[..., None, :] + ig[..., None, :], -jnp.inf)
    inter = b + ms[..., None]
    m_t = jnp.maximum(inter, jnp.max(Dlog, axis=-1))
    Sm = jnp.einsum('bhcqd,bhcsd->bhcqs', qc, kc) * jnp.exp(Dlog - m_t[..., None])
    e_inter = jnp.exp(inter - m_t)
    num = (e_inter[..., None] * jnp.einsum('bhcvd,bhcqd->bhcqv', Cs, qc)
           + jnp.einsum('bhcqs,bhcsv->bhcqv', Sm, vc))
    den = e_inter * jnp.einsum('bhcd,bhcqd->bhcq', ns, qc) + jnp.sum(Sm, axis=-1)
    h = num / jnp.maximum(jnp.abs(den), jnp.exp(-m_t))[..., None]
    return h.transpose(0, 2, 3, 1, 4).reshape(B, S, H, dv)


def hybrid_mixer(h, w_in, q_norm_w, k_norm_w, rel_bias, conv_w, gate_b, h_norm_w, w_out):
    B, S, _ = h.shape
    split_idx = np.cumsum(IN_SPLITS)[:-1].tolist()
    aq, ak, av, mq, mk, mv, mo, mi, mf = jnp.split(h @ w_in, split_idx, axis=-1)

    aq = rms_norm(aq.reshape(B, S, ATTN_HEADS, ATTN_HEAD_DIM), q_norm_w)
    ak = rms_norm(ak.reshape(B, S, ATTN_HEADS, ATTN_HEAD_DIM), k_norm_w)
    av = av.reshape(B, S, ATTN_HEADS, ATTN_HEAD_DIM).astype(jnp.float32)
    attn = dilated_mixture_attention(aq, ak, av, rel_bias).reshape(B, S, ATTN_WIDTH)

    qk = jax.nn.silu(causal_depthwise_conv(jnp.concatenate([mq, mk], axis=-1), conv_w)).astype(jnp.float32)
    mq, mk = jnp.split(qk, 2, axis=-1)
    mi = mi.astype(jnp.float32) + gate_b[:MLSTM_HEADS].astype(jnp.float32)
    mf = mf.astype(jnp.float32) + gate_b[MLSTM_HEADS:].astype(jnp.float32)
    hm = mlstm_chunkwise(mq.reshape(B, S, MLSTM_HEADS, MLSTM_QK_DIM), mk.reshape(B, S, MLSTM_HEADS, MLSTM_QK_DIM),
                         mv.reshape(B, S, MLSTM_HEADS, MLSTM_V_DIM).astype(jnp.float32), mi, mf)
    hm = rms_norm(hm, h_norm_w.reshape(MLSTM_HEADS, MLSTM_V_DIM)).reshape(B, S, MLSTM_WIDTH)
    hm = jax.nn.sigmoid(mo.astype(jnp.float32)) * hm

    mixed = jnp.concatenate([attn, hm], axis=-1).astype(h.dtype)
    return mixed @ w_out


def hier_route(ht, wg, bg, we, be):
    T = ht.shape[0]
    lg = (ht @ wg + bg).astype(jnp.float32)
    pg = jax.nn.softmax(lg, axis=-1)
    _, gsel = lax.top_k(lg, 1)
    p_group = jnp.take_along_axis(pg, gsel, axis=-1)
    le = (ht @ we + be).astype(jnp.float32).reshape(T, MOE_GROUPS, MOE_EXPERTS_PER_GROUP)
    le_sel = jnp.take_along_axis(le, gsel[:, :, None], axis=1)[:, 0]
    top_v, top_i = lax.top_k(le_sel, MOE_TOP_K)
    weights = p_group * jax.nn.softmax(top_v, axis=-1)
    expert_idx = gsel * MOE_EXPERTS_PER_GROUP + top_i
    return expert_idx, weights


def hier_moe(h, wg, bg, we, be, w_gate, w_up, w_down):
    B, S, D = h.shape
    T = B * S
    ht = h.reshape(T, D)
    expert_idx, weights = hier_route(ht, wg, bg, we, be)
    A = T * MOE_TOP_K
    flat_e = expert_idx.reshape(A)
    flat_tok = jnp.arange(A, dtype=jnp.int32) // MOE_TOP_K
    order = jnp.argsort(flat_e)
    se, stok, sw = flat_e[order], flat_tok[order], weights.reshape(A)[order]
    counts = jnp.bincount(flat_e, length=N_EXPERTS)
    padded = (counts + MOE_BLOCK - 1) // MOE_BLOCK * MOE_BLOCK
    pad_end = jnp.cumsum(padded)
    pad_start = pad_end - padded
    start = jnp.cumsum(counts) - counts
    dest = pad_start[se] + jnp.arange(A, dtype=jnp.int32) - start[se]
    n_rows = (-(-A // MOE_BLOCK)) * MOE_BLOCK + N_EXPERTS * MOE_BLOCK
    n_blocks = n_rows // MOE_BLOCK
    rows = jnp.zeros((n_rows, D), h.dtype).at[dest].set(ht[stok])
    block_e = jnp.minimum(jnp.searchsorted(pad_end, jnp.arange(n_blocks, dtype=jnp.int32) * MOE_BLOCK,
                                           side='right'), N_EXPERTS - 1)

    def expert_block(args):
        xb, e = args
        return (jax.nn.silu(xb @ w_gate[e]) * (xb @ w_up[e])) @ w_down[e]

    out_rows = lax.map(expert_block, (rows.reshape(n_blocks, MOE_BLOCK, D), block_e)).reshape(n_rows, D)
    y = jnp.zeros((T, D), jnp.float32).at[stok].add(out_rows[dest].astype(jnp.float32) * sw[:, None])
    return y.reshape(B, S, D)


def setup_inputs(seed: int = 0) -> dict:
    key = jax.random.key(seed)
    ks = jax.random.split(key, 24)
    f32 = jnp.float32

    def nrm(k, shape, std):
        return jax.random.normal(k, shape, f32) * std

    gate_b = jnp.concatenate([
        nrm(ks[11], (DEPTH, MLSTM_HEADS), 0.1),
        jnp.linspace(3.0, 6.0, MLSTM_HEADS, dtype=f32)[None, :] + nrm(ks[12], (DEPTH, MLSTM_HEADS), 0.1)],
        axis=-1)
    return {
        'x': nrm(ks[0], (BATCH, SEQ, D_MODEL), 1.0),
        'c': nrm(ks[1], (BATCH, D_MODEL), 1.0),
        'rel_bias': nrm(ks[2], (NUM_BUCKETS, ATTN_HEADS), 0.5),
        'w_mod': nrm(ks[3], (DEPTH, D_MODEL, 6 * D_MODEL), D_MODEL ** -0.5),
        'b_mod': nrm(ks[4], (DEPTH, 6 * D_MODEL), 0.02),
        'norm1_w': 1.0 + nrm(ks[5], (DEPTH, D_MODEL), 0.02),
        'norm2_w': 1.0 + nrm(ks[6], (DEPTH, D_MODEL), 0.02),
        'w_in': nrm(ks[7], (DEPTH, D_MODEL, IN_WIDTH), D_MODEL ** -0.5),
        'q_norm_w': 1.0 + nrm(ks[8], (DEPTH, ATTN_HEAD_DIM), 0.02),
        'k_norm_w': 1.0 + nrm(ks[9], (DEPTH, ATTN_HEAD_DIM), 0.02),
        'conv_w': nrm(ks[10], (DEPTH, CONV_WIDTH, 2 * MLSTM_HEADS * MLSTM_QK_DIM), CONV_WIDTH ** -0.5),
        'gate_b': gate_b,
        'h_norm_w': 1.0 + nrm(ks[13], (DEPTH, MLSTM_WIDTH), 0.02),
        'w_out': nrm(ks[14], (DEPTH, MIX_WIDTH, D_MODEL), MIX_WIDTH ** -0.5),
        'router_group_w': nrm(ks[15], (DEPTH, D_MODEL, MOE_GROUPS), D_MODEL ** -0.5),
        'router_group_b': nrm(ks[16], (DEPTH, MOE_GROUPS), 0.01),
        'router_expert_w': nrm(ks[17], (DEPTH, D_MODEL, N_EXPERTS), D_MODEL ** -0.5),
        'router_expert_b': nrm(ks[18], (DEPTH, N_EXPERTS), 0.01),
        'w_gate': nrm(ks[19], (DEPTH, N_EXPERTS, D_MODEL, EXPERT_FF), D_MODEL ** -0.5),
        'w_up': nrm(ks[20], (DEPTH, N_EXPERTS, D_MODEL, EXPERT_FF), D_MODEL ** -0.5),
        'w_down': nrm(ks[21], (DEPTH, N_EXPERTS, EXPERT_FF, D_MODEL), EXPERT_FF ** -0.5),
    }


def reference(x, c, rel_bias, w_mod, b_mod, norm1_w, norm2_w, w_in, q_norm_w, k_norm_w, conv_w,
              gate_b, h_norm_w, w_out, router_group_w, router_group_b, router_expert_w,
              router_expert_b, w_gate, w_up, w_down):
    B, S, D = x.shape
    for l in range(DEPTH):
        mod = (jax.nn.silu(c) @ w_mod[l] + b_mod[l]).astype(jnp.float32).reshape(B, 6, 1, D)
        shift1, scale1, gate1 = mod[:, 0], mod[:, 1], mod[:, 2]
        shift2, scale2, gate2 = mod[:, 3], mod[:, 4], mod[:, 5]

        h = (rms_norm(x, norm1_w[l]) * (1.0 + scale1) + shift1).astype(x.dtype)
        mix = hybrid_mixer(h, w_in[l], q_norm_w[l], k_norm_w[l], rel_bias, conv_w[l], gate_b[l],
                           h_norm_w[l], w_out[l])
        x = x + (gate1 * mix.astype(jnp.float32)).astype(x.dtype)

        h = (rms_norm(x, norm2_w[l]) * (1.0 + scale2) + shift2).astype(x.dtype)
        ffn = hier_moe(h, router_group_w[l], router_group_b[l], router_expert_w[l], router_expert_b[l],
                       w_gate[l], w_up[l], w_down[l])
        x = x + (gate2 * ffn).astype(x.dtype)
    return x
```

```python
import functools

import jax
import jax.numpy as jnp
import numpy as np
from jax import lax
from jax.experimental import pallas as pl
from jax.experimental.pallas import tpu as pltpu

F32 = jnp.float32
BF16 = jnp.bfloat16

NORM_EPS = 1e-6
ATTN_HEAD_DIM = 64
ATTN_BLOCK = 128
DILATED_PATTERNS = ((128, 1), (512, 4), (2048, 16))
NUM_BUCKETS = 32
REL_MAX_DIST = 2048
MLSTM_HEADS = 4
MLSTM_CHUNK = 128
CONV_WIDTH = 4
MOE_GROUPS = 4
MOE_EXPERTS_PER_GROUP = 8
N_EXPERTS = MOE_GROUPS * MOE_EXPERTS_PER_GROUP
MOE_TOP_K = 2
LANES = 128
SUBLANES = 8
ROUTER_ROWS = 40
MOE_ROWS = 256
MASK_VALUE = -1e30
MiB = 1024 * 1024


def _cparams(sem, vmem_mib):
    return pltpu.CompilerParams(dimension_semantics=sem, vmem_limit_bytes=vmem_mib * MiB)


def _split_bf16(a):
    hi = a.astype(BF16)
    lo = (a - hi.astype(F32)).astype(BF16)
    return hi, lo


def _dot(a, b):
    return jnp.dot(a, b, preferred_element_type=F32)


def _dot_nt(a, b):
    return lax.dot_general(a, b, (((1,), (1,)), ((), ())), preferred_element_type=F32)


def _dot_tn(a, b):
    return lax.dot_general(a, b, (((0,), (0,)), ((), ())), preferred_element_type=F32)


def _dot3(a_f32, w_hi, w_lo):
    a_hi, a_lo = _split_bf16(a_f32)
    return _dot(a_hi, w_hi) + _dot(a_lo, w_hi) + _dot(a_hi, w_lo)


def _rms_modulate(x, norm_w, scale, shift):
    ms = jnp.mean(x * x, axis=-1, keepdims=True)
    return x * lax.rsqrt(ms + NORM_EPS) * norm_w * (1.0 + scale) + shift


def _mod_kernel(c_ref, w_ref, b_ref, o_ref):
    c = c_ref[...]
    a = c * jax.nn.sigmoid(c)
    w_hi, w_lo = _split_bf16(w_ref[...])
    o_ref[...] = _dot3(a, w_hi, w_lo) + b_ref[...]


def _modulation(c, w_mod, b_mod):
    depth, d, n = w_mod.shape
    b = c.shape[0]
    bp = -(-b // SUBLANES) * SUBLANES
    cp = jnp.zeros((bp, d), F32).at[:b].set(c)
    tn = 1024
    out = pl.pallas_call(
        _mod_kernel,
        out_shape=jax.ShapeDtypeStruct((depth, bp, n), F32),
        grid=(depth, n // tn),
        in_specs=[
            pl.BlockSpec((bp, d), lambda l, j: (0, 0)),
            pl.BlockSpec((None, d, tn), lambda l, j: (l, 0, j)),
            pl.BlockSpec((None, 1, tn), lambda l, j: (l, 0, j)),
        ],
        out_specs=pl.BlockSpec((None, bp, tn), lambda l, j: (l, 0, j)),
        compiler_params=_cparams(("arbitrary", "arbitrary"), 40),
        name="modulation",
    )(cp, w_mod, b_mod.reshape(depth, 1, n))
    return out[:, :b].reshape(depth, b, 6, d)


def _in_proj_kernel(x_ref, mod_ref, nw_ref, w_ref, wgh_ref, wgl_ref, o_ref, g_ref, h_scr):
    @pl.when(pl.program_id(1) == 0)
    def _():
        h = _rms_modulate(x_ref[...], nw_ref[...], mod_ref[1:2, :], mod_ref[0:1, :])
        h_scr[...] = h.astype(BF16)
        g_ref[...] = _dot3(h, wgh_ref[...], wgl_ref[...])

    o_ref[...] = _dot(h_scr[...], w_ref[...]).astype(BF16)


def _in_proj(x2d, mod, norm_w, w_main, wg_hi, wg_lo, seq):
    t, d = x2d.shape
    n = w_main.shape[1]
    tm, tn = 1024, 512
    blocks_per_seq = seq // tm
    return pl.pallas_call(
        _in_proj_kernel,
        out_shape=(jax.ShapeDtypeStruct((t, n), BF16), jax.ShapeDtypeStruct((t, LANES), F32)),
        grid=(t // tm, n // tn),
        in_specs=[
            pl.BlockSpec((tm, d), lambda i, j: (i, 0)),
            pl.BlockSpec((None, 6, d), lambda i, j: (i // blocks_per_seq, 0, 0)),
            pl.BlockSpec((1, d), lambda i, j: (0, 0)),
            pl.BlockSpec((d, tn), lambda i, j: (0, j)),
            pl.BlockSpec((d, LANES), lambda i, j: (0, 0)),
            pl.BlockSpec((d, LANES), lambda i, j: (0, 0)),
        ],
        out_specs=(pl.BlockSpec((tm, tn), lambda i, j: (i, j)),
                   pl.BlockSpec((tm, LANES), lambda i, j: (i, 0))),
        scratch_shapes=[pltpu.VMEM((tm, d), BF16)],
        compiler_params=_cparams(("arbitrary", "arbitrary"), 48),
        name="in_proj",
    )(x2d, mod, norm_w, w_main, wg_hi, wg_lo)


def _t5_causal_bucket(dist):
    max_exact = NUM_BUCKETS // 2
    d = np.maximum(dist, 1).astype(np.float32)
    large = max_exact + (np.log(d / max_exact) / np.log(REL_MAX_DIST / max_exact)
                         * (NUM_BUCKETS - max_exact)).astype(np.int32)
    large = np.minimum(large, NUM_BUCKETS - 1)
    return np.where(dist < max_exact, dist, large).astype(np.int32)


def _attn_bias_tables(rel_bias):
    blk = ATTN_BLOCK
    qi = np.arange(blk)[:, None]
    kj = np.arange(2 * blk)[None, :]
    rel = qi - kj + blk
    tables = []
    for window, dilation in DILATED_PATTERNS:
        win = window // dilation
        valid = (rel >= 0) & (rel <= win)
        bucket = _t5_causal_bucket(np.clip(rel, 0, win) * dilation)
        bias = jnp.transpose(rel_bias[bucket], (2, 0, 1)).astype(F32)
        tables.append(jnp.where(valid[None], bias, MASK_VALUE))
    return jnp.stack(tables, axis=0)


def _attn_kernel(q_ref, k_ref, v_ref, qw_ref, kw_ref, bias_ref, o_ref,
                 qs, ks, vs, acc_s, m_s, l_s, *, seq):
    blk = ATTN_BLOCK
    lane = lax.broadcasted_iota(jnp.int32, (1, LANES), 1)
    head0 = lane < ATTN_HEAD_DIM
    gi = lax.broadcasted_iota(jnp.int32, (LANES, LANES), 0) // ATTN_HEAD_DIM
    gj = lax.broadcasted_iota(jnp.int32, (LANES, LANES), 1) // ATTN_HEAD_DIM
    group_ones = (gi == gj).astype(BF16)
    first_cols = lax.broadcasted_iota(jnp.int32, (1, 2 * blk), 1) < blk
    rows = 256

    def norm_body(c, carry):
        sl = pl.ds(pl.multiple_of(c * rows, rows), rows)
        for src, dst, w_ref, scale in ((q_ref, qs, qw_ref, ATTN_HEAD_DIM ** -0.5), (k_ref, ks, kw_ref, 1.0)):
            xx = src[sl, :].astype(F32)
            hi, lo = _split_bf16(xx * xx)
            ssq = _dot(hi, group_ones) + _dot(lo, group_ones)
            dst[sl, :] = xx * lax.rsqrt(ssq * (1.0 / ATTN_HEAD_DIM) + NORM_EPS) * w_ref[...] * scale
        vs[sl, :] = v_ref[sl, :].astype(F32)
        return carry

    lax.fori_loop(0, seq // rows, norm_body, 0)

    for p, (_, dil) in enumerate(DILATED_PATTERNS):
        nb = (seq // dil) // blk

        def strided(ref, start, dil=dil):
            if dil == 1:
                return ref[pl.ds(start, blk), :]
            return ref[pl.ds(start, blk, stride=dil), :]

        def strided_store(ref, start, val, dil=dil):
            if dil == 1:
                ref[pl.ds(start, blk), :] = val
            else:
                ref[pl.ds(start, blk, stride=dil), :] = val

        def body(idx, carry, p=p, dil=dil, nb=nb, strided=strided, strided_store=strided_store):
            r = idx // nb
            n = idx - r * nb
            base = r + n * (blk * dil)
            prev = r + jnp.maximum(n - 1, 0) * (blk * dil)
            qb = strided(qs, base).astype(BF16)
            kw = jnp.concatenate([strided(ks, prev), strided(ks, base)], axis=0).astype(BF16)
            vw = jnp.concatenate([strided(vs, prev), strided(vs, base)], axis=0).astype(BF16)
            kill = jnp.logical_and(first_cols, n == 0)
            ms, ls, accs = [], [], []
            for h, hmask in enumerate((head0, jnp.logical_not(head0))):
                qh = jnp.where(hmask, qb, jnp.zeros_like(qb))
                s = _dot_nt(qh, kw) + bias_ref[p, h]
                s = jnp.where(kill, MASK_VALUE, s)
                mh = jnp.max(s, axis=-1, keepdims=True)
                ph = jnp.exp(s - mh)
                ls.append(jnp.sum(ph, axis=-1, keepdims=True))
                ms.append(mh)
                accs.append(_dot(ph.astype(BF16), vw))
            m_b = jnp.where(head0, ms[0], ms[1])
            l_b = jnp.where(head0, ls[0], ls[1])
            a_b = jnp.where(head0, accs[0], accs[1])
            if p == 0:
                m_new, l_new, a_new = m_b, l_b, a_b
            else:
                m_o = strided(m_s, base)
                m_new = jnp.maximum(m_o, m_b)
                w_o = jnp.exp(m_o - m_new)
                w_b = jnp.exp(m_b - m_new)
                l_new = w_o * strided(l_s, base) + w_b * l_b
                a_new = w_o * strided(acc_s, base) + w_b * a_b
            strided_store(m_s, base, m_new)
            strided_store(l_s, base, l_new)
            strided_store(acc_s, base, a_new)
            return carry

        lax.fori_loop(0, seq // blk, body, 0)

    def out_body(c, carry):
        sl = pl.ds(pl.multiple_of(c * rows, rows), rows)
        o_ref[sl, :] = (acc_s[sl, :] / l_s[sl, :]).astype(o_ref.dtype)
        return carry

    lax.fori_loop(0, seq // rows, out_body, 0)


def _attention(proj, qw2, kw2, bias_tab, batch, seq, attn_width):
    t, n = proj.shape
    proj3 = proj.reshape(batch, seq, n)
    pairs = attn_width // LANES
    npat = len(DILATED_PATTERNS)
    kern = functools.partial(_attn_kernel, seq=seq)
    return pl.pallas_call(
        kern,
        out_shape=jax.ShapeDtypeStruct((batch, seq, attn_width), BF16),
        grid=(batch, pairs),
        in_specs=[
            pl.BlockSpec((None, seq, LANES), lambda b, h: (b, 0, h)),
            pl.BlockSpec((None, seq, LANES), lambda b, h: (b, 0, pairs + h)),
            pl.BlockSpec((None, seq, LANES), lambda b, h: (b, 0, 2 * pairs + h)),
            pl.BlockSpec((1, LANES), lambda b, h: (0, 0)),
            pl.BlockSpec((1, LANES), lambda b, h: (0, 0)),
            pl.BlockSpec((npat, 2, ATTN_BLOCK, 2 * ATTN_BLOCK), lambda b, h: (0, h, 0, 0)),
        ],
        out_specs=pl.BlockSpec((None, seq, LANES), lambda b, h: (b, 0, h)),
        scratch_shapes=[pltpu.VMEM((seq, LANES), F32) for _ in range(6)],
        compiler_params=_cparams(("arbitrary", "arbitrary"), 40),
        name="dilated_attention",
    )(proj3, proj3, proj3, qw2, kw2, bias_tab)


def _mlstm_kernel(q_ref, k_ref, v_ref, og_ref, g_ref, cw_ref, gb_ref, hw_ref, o_ref,
                  x_scr, c_scr, m_scr, *, rows, dk, dv):
    ch = MLSTM_CHUNK
    nheads = MLSTM_HEADS
    qk_w = nheads * dk
    aug = dv + LANES

    @pl.when(pl.program_id(1) == 0)
    def _():
        x_scr[0:SUBLANES, :] = jnp.zeros((SUBLANES, 2 * qk_w), F32)
        c_scr[...] = jnp.zeros_like(c_scr)
        m_scr[...] = jnp.zeros_like(m_scr)

    x_scr[SUBLANES:, 0:qk_w] = q_ref[...].astype(F32)
    x_scr[SUBLANES:, qk_w:] = k_ref[...].astype(F32)

    ri = lax.broadcasted_iota(jnp.int32, (ch, ch), 0)
    ci = lax.broadcasted_iota(jnp.int32, (ch, ch), 1)
    causal = ci <= ri
    tril = causal.astype(BF16)
    row8 = lax.broadcasted_iota(jnp.int32, (SUBLANES, 1), 0)
    ones_col = (lax.broadcasted_iota(jnp.int32, (ch, LANES), 1) == 0).astype(F32)
    k_scale = dk ** -0.5

    def chunk_body(c, carry):
        r0 = pl.multiple_of(c * ch, ch)
        prev8 = x_scr[pl.ds(r0, SUBLANES), :]
        cur = x_scr[pl.ds(r0 + SUBLANES, ch), :]
        conv = cur * cw_ref[CONV_WIDTH - 1:CONV_WIDTH, :]
        for sh in range(1, CONV_WIDTH):
            rolled = pltpu.roll(cur, sh, 0)
            head_rows = jnp.where(row8 < sh, pltpu.roll(prev8, sh, 0), rolled[0:SUBLANES])
            shifted = jnp.concatenate([head_rows, rolled[SUBLANES:]], axis=0)
            conv = conv + shifted * cw_ref[CONV_WIDTH - 1 - sh:CONV_WIDTH - sh, :]
        qk = conv * jax.nn.sigmoid(conv)

        gates = g_ref[pl.ds(r0, ch), :] + gb_ref[...]
        logf = jax.nn.log_sigmoid(gates)
        lf_hi, lf_lo = _split_bf16(logf)
        bcum = _dot(tril, lf_hi) + _dot(tril, lf_lo)
        gates_t = gates.T
        bcum_t = bcum.T

        for h in range(nheads):
            q = qk[:, h * dk:(h + 1) * dk]
            k = qk[:, qk_w + h * dk:qk_w + (h + 1) * dk] * k_scale
            v = v_ref[pl.ds(r0, ch), h * dv:(h + 1) * dv].astype(F32)
            v_aug = jnp.concatenate([v, ones_col], axis=1)
            i_col = gates[:, h:h + 1]
            b_col = bcum[:, nheads + h:nheads + h + 1]
            i_row = gates_t[h:h + 1, :]
            b_row = bcum_t[nheads + h:nheads + h + 1, :]
            m_prev = m_scr[h, 0:1, 0:1]
            c_prev = c_scr[h]

            dlog = jnp.where(causal, b_col - b_row + i_row, -jnp.inf)
            inter = b_col + m_prev
            m_t = jnp.maximum(inter, jnp.max(dlog, axis=-1, keepdims=True))
            qb = q.astype(BF16)
            kb = k.astype(BF16)
            sm = _dot_nt(qb, kb) * jnp.exp(dlog - m_t)
            e_inter = jnp.exp(inter - m_t)
            tot = e_inter * _dot(qb, c_prev.astype(BF16)) + _dot(sm.astype(BF16), v_aug.astype(BF16))
            num = tot[:, 0:dv]
            den = tot[:, dv:dv + 1]
            hcell = num / jnp.maximum(jnp.abs(den), jnp.exp(-m_t))

            g_last = b_col[ch - 1:ch, :]
            a_col = g_last - b_col + i_col
            m_new = jnp.maximum(g_last + m_prev, jnp.max(a_col, axis=0, keepdims=True))
            decay = jnp.exp(g_last + m_prev - m_new)
            w_col = jnp.exp(a_col - m_new)
            c_scr[h] = decay * c_prev + _dot_tn(kb, (w_col * v_aug).astype(BF16))
            m_scr[h] = jnp.broadcast_to(m_new, (SUBLANES, LANES))

            ms = jnp.mean(hcell * hcell, axis=-1, keepdims=True)
            hn = hcell * lax.rsqrt(ms + NORM_EPS) * hw_ref[:, h * dv:(h + 1) * dv]
            og = og_ref[pl.ds(r0, ch), h * dv:(h + 1) * dv].astype(F32)
            o_ref[pl.ds(r0, ch), h * dv:(h + 1) * dv] = (jax.nn.sigmoid(og) * hn).astype(o_ref.dtype)
        return carry

    lax.fori_loop(0, rows // ch, chunk_body, 0)
    x_scr[0:SUBLANES, :] = x_scr[rows:rows + SUBLANES, :]


def _mlstm(proj, gates, conv_w, gate_b_row, h_norm_w, batch, seq, attn_width, dk, dv):
    t, n = proj.shape
    proj3 = proj.reshape(batch, seq, n)
    gates3 = gates.reshape(batch, seq, LANES)
    nheads = MLSTM_HEADS
    qk_w = nheads * dk
    v_w = nheads * dv
    rows = 512
    q_blk = (3 * attn_width) // qk_w
    v_blk = (3 * attn_width + 2 * qk_w) // v_w
    kern = functools.partial(_mlstm_kernel, rows=rows, dk=dk, dv=dv)
    return pl.pallas_call(
        kern,
        out_shape=jax.ShapeDtypeStruct((batch, seq, v_w), BF16),
        grid=(batch, seq // rows),
        in_specs=[
            pl.BlockSpec((None, rows, qk_w), lambda b, j: (b, j, q_blk)),
            pl.BlockSpec((None, rows, qk_w), lambda b, j: (b, j, q_blk + 1)),
            pl.BlockSpec((None, rows, v_w), lambda b, j: (b, j, v_blk)),
            pl.BlockSpec((None, rows, v_w), lambda b, j: (b, j, v_blk + 1)),
            pl.BlockSpec((None, rows, LANES), lambda b, j: (b, j, 0)),
            pl.BlockSpec((CONV_WIDTH, 2 * qk_w), lambda b, j: (0, 0)),
            pl.BlockSpec((1, LANES), lambda b, j: (0, 0)),
            pl.BlockSpec((1, v_w), lambda b, j: (0, 0)),
        ],
        out_specs=pl.BlockSpec((None, rows, v_w), lambda b, j: (b, j, 0)),
        scratch_shapes=[
            pltpu.VMEM((rows + SUBLANES, 2 * qk_w), F32),
            pltpu.VMEM((nheads, dk, dv + LANES), F32),
            pltpu.VMEM((nheads, SUBLANES, LANES), F32),
        ],
        compiler_params=_cparams(("arbitrary", "arbitrary"), 40),
        name="mlstm",
    )(proj3, proj3, proj3, proj3, gates3, conv_w, gate_b_row, h_norm_w)


def _store_token_tiles(ref, base, h):
    m, w2 = h.shape
    w = w2 // 2
    tile_rows = w // LANES
    u = pltpu.bitcast(h.astype(BF16).astype(F32), jnp.uint32)
    packed = (u[:, :w] >> 16) | (u[:, w:] & jnp.uint32(0xFFFF0000))
    for k in range(tile_rows):
        ref[pl.ds(base + k, m, stride=tile_rows), :] = packed[:, k * LANES:(k + 1) * LANES]


def _load_token_tiles(ref, base, m, tile_rows):
    lo, hi = [], []
    for k in range(tile_rows):
        p = ref[pl.ds(base + k, m, stride=tile_rows), :]
        lo.append(pltpu.bitcast(p << 16, F32))
        hi.append(pltpu.bitcast(p & jnp.uint32(0xFFFF0000), F32))
    return jnp.concatenate(lo, axis=1), jnp.concatenate(hi, axis=1)


def _out_proj_kernel(a_ref, m_ref, w_ref, x_ref, mod_ref, nw_ref, wrh_ref, wrl_ref, br_ref,
                     xo_ref, hp_ref, lg_ref, *, attn_width):
    mix = _dot(a_ref[...], w_ref[0:attn_width, :]) + _dot(m_ref[...], w_ref[attn_width:, :])
    xn = x_ref[...] + mod_ref[2:3, :] * mix
    xo_ref[...] = xn
    h2 = _rms_modulate(xn, nw_ref[...], mod_ref[4:5, :], mod_ref[3:4, :])
    _store_token_tiles(hp_ref, 0, h2)
    h_hi, h_lo = _split_bf16(h2)
    lg_ref[...] = (_dot_nt(wrh_ref[...], h_hi) + _dot_nt(wrh_ref[...], h_lo)
                   + _dot_nt(wrl_ref[...], h_hi) + br_ref[...])


def _out_proj(attn2d, hm2d, w_out, x2d, mod, norm_w, wr_hi, wr_lo, br, seq):
    t, d = x2d.shape
    aw = attn2d.shape[1]
    mw = hm2d.shape[1]
    tm = 256
    blocks_per_seq = seq // tm
    tile_rows = d // 2 // LANES
    kern = functools.partial(_out_proj_kernel, attn_width=aw)
    return pl.pallas_call(
        kern,
        out_shape=(jax.ShapeDtypeStruct((t, d), F32),
                   jax.ShapeDtypeStruct((t * tile_rows, LANES), jnp.uint32),
                   jax.ShapeDtypeStruct((ROUTER_ROWS, t), F32)),
        grid=(t // tm,),
        in_specs=[
            pl.BlockSpec((tm, aw), lambda i: (i, 0)),
            pl.BlockSpec((tm, mw), lambda i: (i, 0)),
            pl.BlockSpec((aw + mw, d), lambda i: (0, 0)),
            pl.BlockSpec((tm, d), lambda i: (i, 0)),
            pl.BlockSpec((None, 6, d), lambda i: (i // blocks_per_seq, 0, 0)),
            pl.BlockSpec((1, d), lambda i: (0, 0)),
            pl.BlockSpec((ROUTER_ROWS, d), lambda i: (0, 0)),
            pl.BlockSpec((ROUTER_ROWS, d), lambda i: (0, 0)),
            pl.BlockSpec((ROUTER_ROWS, 1), lambda i: (0, 0)),
        ],
        out_specs=(pl.BlockSpec((tm, d), lambda i: (i, 0)),
                   pl.BlockSpec((tm * tile_rows, LANES), lambda i: (i, 0)),
                   pl.BlockSpec((ROUTER_ROWS, tm), lambda i: (0, i))),
        compiler_params=_cparams(("arbitrary",), 48),
        name="out_proj",
    )(attn2d, hm2d, w_out, x2d, mod, norm_w, wr_hi, wr_lo, br)


def _route_kernel(lg_ref, idx_ref, w_ref):
    ng, ne = MOE_GROUPS, MOE_EXPERTS_PER_GROUP
    lg = [lg_ref[g:g + 1, :] for g in range(ng)]
    best, gsel = lg[0], jnp.zeros_like(lg[0], dtype=jnp.int32)
    for g in range(1, ng):
        better = lg[g] > best
        best = jnp.where(better, lg[g], best)
        gsel = jnp.where(better, g, gsel)
    denom = jnp.exp(lg[0] - best)
    for g in range(1, ng):
        denom = denom + jnp.exp(lg[g] - best)
    p_group = 1.0 / denom

    le = []
    for e in range(ne):
        v = lg_ref[ng + e:ng + e + 1, :]
        for g in range(1, ng):
            v = jnp.where(gsel == g, lg_ref[ng + g * ne + e:ng + g * ne + e + 1, :], v)
        le.append(v)
    v1, i1 = le[0], jnp.zeros_like(gsel)
    for e in range(1, ne):
        better = le[e] > v1
        v1 = jnp.where(better, le[e], v1)
        i1 = jnp.where(better, e, i1)
    v2 = jnp.full_like(v1, -jnp.inf)
    i2 = jnp.zeros_like(gsel)
    for e in range(ne):
        better = jnp.logical_and(le[e] > v2, i1 != e)
        v2 = jnp.where(better, le[e], v2)
        i2 = jnp.where(better, e, i2)
    e2 = jnp.exp(v2 - v1)
    inv = 1.0 / (1.0 + e2)
    zero_i = jnp.zeros((SUBLANES - MOE_TOP_K,) + gsel.shape[1:], jnp.int32)
    zero_f = jnp.zeros((SUBLANES - MOE_TOP_K,) + gsel.shape[1:], F32)
    idx_ref[...] = jnp.concatenate([gsel * ne + i1, gsel * ne + i2, zero_i], axis=0)
    w_ref[...] = jnp.concatenate([p_group * inv, p_group * (e2 * inv), zero_f], axis=0)


def _route(logits_t):
    rows, t = logits_t.shape
    tn = 2048
    return pl.pallas_call(
        _route_kernel,
        out_shape=(jax.ShapeDtypeStruct((SUBLANES, t), jnp.int32),
                   jax.ShapeDtypeStruct((SUBLANES, t), F32)),
        grid=(t // tn,),
        in_specs=[pl.BlockSpec((rows, tn), lambda i: (0, i))],
        out_specs=(pl.BlockSpec((SUBLANES, tn), lambda i: (0, i)),
                   pl.BlockSpec((SUBLANES, tn), lambda i: (0, i))),
        compiler_params=_cparams(("arbitrary",), 32),
        name="route",
    )(logits_t)


def _dispatch_tables(eidx, ew, t):
    m = MOE_ROWS
    a_total = MOE_TOP_K * t
    n_blocks = a_total // m + N_EXPERTS
    n_rows = n_blocks * m
    e_flat = eidx[:MOE_TOP_K].reshape(a_total)
    w_flat = ew[:MOE_TOP_K].reshape(a_total)
    order = jnp.argsort(e_flat).astype(jnp.int32)
    counts = jnp.sum((e_flat[:, None] == jnp.arange(N_EXPERTS, dtype=jnp.int32)[None, :]).astype(jnp.int32), axis=0)
    padded = (counts + m - 1) // m * m
    pad_end = jnp.cumsum(padded)
    pad_start = pad_end - padded
    start = jnp.cumsum(counts) - counts
    n_used = (pad_end[-1] // m).astype(jnp.int32)
    blk = jnp.arange(n_blocks, dtype=jnp.int32)
    block_e = jnp.minimum(jnp.searchsorted(pad_end, blk * m, side='right'), N_EXPERTS - 1).astype(jnp.int32)
    block_e = jnp.where(blk < n_used, block_e, block_e[jnp.maximum(n_used - 1, 0)])
    r = jnp.arange(n_rows, dtype=jnp.int32)
    e_r = jnp.minimum(jnp.searchsorted(pad_end, r, side='right'), N_EXPERTS - 1).astype(jnp.int32)
    off = r - pad_start[e_r]
    valid = off < counts[e_r]
    a_r = order[jnp.clip(start[e_r] + off, 0, a_total - 1)]
    row_dst = jnp.where(valid, a_r, r % m).astype(jnp.int32)
    row_w = jnp.where(valid, w_flat[a_r], 0.0)
    block_valid = jnp.sum(valid.reshape(n_blocks, m).astype(jnp.int32), axis=1)
    return (block_e, block_valid, n_used.reshape(1), row_dst.reshape(n_blocks, m),
            jnp.broadcast_to(row_w[:, None], (n_rows, LANES)))


def _moe_kernel(be_ref, bv_ref, nu_ref, idx_hbm, h_hbm, wrow_ref, wg_ref, wu_ref, wd_ref, out_hbm,
                idx_s, xbuf, ybuf, sem_i, sem_g, sem_s, *, tokens, ff_chunk):
    m = MOE_ROWS
    i = pl.program_id(0)
    n_used = nu_ref[0]
    slot = lax.rem(i, 2)
    other = 1 - slot
    ff = wg_ref.shape[2]
    half = wg_ref.shape[1] // 2
    tr = half // LANES
    slot_rows = m * tr

    def tile_rows_at(row):
        return pl.ds(pl.multiple_of(row * tr, tr), tr)

    def idx_copy(blk, s):
        return pltpu.make_async_copy(idx_hbm.at[blk], idx_s.at[s], sem_i.at[s])

    def issue_gather(s):
        def body(j, carry):
            tok = lax.rem(idx_s[s, j], tokens)
            pltpu.make_async_copy(h_hbm.at[tile_rows_at(tok)], xbuf.at[tile_rows_at(s * m + j)],
                                  sem_g.at[s]).start()
            return carry
        lax.fori_loop(0, m, body, 0, unroll=8)

    def wait_gather(s):
        dst = xbuf.at[pl.ds(pl.multiple_of(s * slot_rows, slot_rows), slot_rows)]
        pltpu.make_async_copy(h_hbm.at[pl.ds(0, slot_rows)], dst, sem_g.at[s]).wait()

    def issue_scatter(s, n_valid):
        def body(j, carry):
            dst = idx_s[s, j]
            pltpu.make_async_copy(ybuf.at[tile_rows_at(s * m + j)], out_hbm.at[tile_rows_at(dst)],
                                  sem_s.at[s]).start()
            return carry
        lax.fori_loop(0, n_valid, body, 0)

    def wait_scatter(s, n_valid):
        n = pl.multiple_of(n_valid * tr, tr)
        src = ybuf.at[pl.ds(pl.multiple_of(s * slot_rows, slot_rows), n)]
        pltpu.make_async_copy(src, out_hbm.at[pl.ds(0, n)], sem_s.at[s]).wait()

    @pl.when(i == 0)
    def _():
        idx_copy(0, 0).start()
        idx_copy(0, 0).wait()
        issue_gather(0)

        @pl.when(n_used > 1)
        def _():
            idx_copy(1, 1).start()

    @pl.when(i + 1 < n_used)
    def _():
        idx_copy(i + 1, other).wait()
        issue_gather(other)

    @pl.when(i < n_used)
    def _():
        wait_gather(slot)
        base = pl.multiple_of(slot * slot_rows, slot_rows)
        x_lo, x_hi = _load_token_tiles(xbuf, base, m, tr)
        x_lo, x_hi = x_lo.astype(BF16), x_hi.astype(BF16)
        acc = jnp.zeros((m, wd_ref.shape[2]), F32)
        for c in range(ff // ff_chunk):
            cs = slice(c * ff_chunk, (c + 1) * ff_chunk)
            g = _dot(x_lo, wg_ref[0, 0:half, cs]) + _dot(x_hi, wg_ref[0, half:, cs])
            u = _dot(x_lo, wu_ref[0, 0:half, cs]) + _dot(x_hi, wu_ref[0, half:, cs])
            act = (g * jax.nn.sigmoid(g) * u).astype(BF16)
            acc = acc + _dot(act, wd_ref[0, cs, :])

        @pl.when(i >= 1)
        def _():
            wait_scatter(other, bv_ref[jnp.maximum(i - 1, 0)])

        _store_token_tiles(ybuf, base, acc * wrow_ref[:, 0:1])
        issue_scatter(slot, bv_ref[i])

        @pl.when(i + 2 < n_used)
        def _():
            idx_copy(i + 2, slot).start()

        @pl.when(i == n_used - 1)
        def _():
            wait_scatter(slot, bv_ref[i])


def _moe(block_e, block_valid, n_used, row_dst, row_w, h_packed, wg, wu, wd):
    d, ff = wg.shape[1], wg.shape[2]
    tr = d // 2 // LANES
    t = h_packed.shape[0] // tr
    m = MOE_ROWS
    n_blocks = row_dst.shape[0]
    kern = functools.partial(_moe_kernel, tokens=t, ff_chunk=256)
    grid_spec = pltpu.PrefetchScalarGridSpec(
        num_scalar_prefetch=3,
        grid=(n_blocks,),
        in_specs=[
            pl.BlockSpec(memory_space=pl.ANY),
            pl.BlockSpec(memory_space=pl.ANY),
            pl.BlockSpec((m, LANES), lambda i, be, bv, nu: (i, 0)),
            pl.BlockSpec((1, d, ff), lambda i, be, bv, nu: (be[i], 0, 0)),
            pl.BlockSpec((1, d, ff), lambda i, be, bv, nu: (be[i], 0, 0)),
            pl.BlockSpec((1, ff, d), lambda i, be, bv, nu: (be[i], 0, 0)),
        ],
        out_specs=pl.BlockSpec(memory_space=pl.ANY),
        scratch_shapes=[
            pltpu.SMEM((2, m), jnp.int32),
            pltpu.VMEM((2 * m * tr, LANES), jnp.uint32),
            pltpu.VMEM((2 * m * tr, LANES), jnp.uint32),
            pltpu.SemaphoreType.DMA((2,)),
            pltpu.SemaphoreType.DMA((2,)),
            pltpu.SemaphoreType.DMA((2,)),
        ],
    )
    return pl.pallas_call(
        kern,
        out_shape=jax.ShapeDtypeStruct((MOE_TOP_K * t * tr, LANES), jnp.uint32),
        grid_spec=grid_spec,
        compiler_params=_cparams(("arbitrary",), 56),
        name="moe_experts",
    )(block_e, block_valid, n_used, row_dst, h_packed, row_w, wg, wu, wd)


def _combine_kernel(x_ref, y0_ref, y1_ref, mod_ref, o_ref):
    tm, d = x_ref.shape
    half = d // 2
    tr = half // LANES
    lo0, hi0 = _load_token_tiles(y0_ref, 0, tm, tr)
    lo1, hi1 = _load_token_tiles(y1_ref, 0, tm, tr)
    o_ref[:, 0:half] = x_ref[:, 0:half] + mod_ref[5:6, 0:half] * (lo0 + lo1)
    o_ref[:, half:] = x_ref[:, half:] + mod_ref[5:6, half:] * (hi0 + hi1)


def _combine(x2d, y, mod, seq):
    t, d = x2d.shape
    tm = 512
    tr = d // 2 // LANES
    blocks_per_seq = seq // tm
    nblk = t // tm
    return pl.pallas_call(
        _combine_kernel,
        out_shape=jax.ShapeDtypeStruct((t, d), F32),
        grid=(nblk,),
        in_specs=[
            pl.BlockSpec((tm, d), lambda i: (i, 0)),
            pl.BlockSpec((tm * tr, LANES), lambda i: (i, 0)),
            pl.BlockSpec((tm * tr, LANES), lambda i: (nblk + i, 0)),
            pl.BlockSpec((None, 6, d), lambda i: (i // blocks_per_seq, 0, 0)),
        ],
        out_specs=pl.BlockSpec((tm, d), lambda i: (i, 0)),
        compiler_params=_cparams(("arbitrary",), 48),
        name="combine",
    )(x2d, y, y, mod)


def kernel(x, c, rel_bias, w_mod, b_mod, norm1_w, norm2_w, w_in, q_norm_w, k_norm_w, conv_w, gate_b, h_norm_w, w_out, router_group_w, router_group_b, router_expert_w, router_expert_b, w_gate, w_up, w_down):
    batch, seq, d = x.shape
    depth = w_mod.shape[0]
    t = batch * seq
    attn_width = d // 2
    mlstm_width = d - attn_width
    dv = mlstm_width // MLSTM_HEADS
    dk = dv // 2
    n_main = 3 * attn_width + 2 * MLSTM_HEADS * dk + 2 * mlstm_width
    assert dk == LANES and attn_width % LANES == 0 and seq % 1024 == 0 and t % MOE_ROWS == 0
    assert all((seq // dil) % ATTN_BLOCK == 0 and window // dil == ATTN_BLOCK for window, dil in DILATED_PATTERNS)

    mod_all = _modulation(c, w_mod, b_mod)
    bias_tab = _attn_bias_tables(rel_bias)
    x2d = x.reshape(t, d)

    for l in range(depth):
        mod = mod_all[l]
        w_main = w_in[l, :, :n_main].astype(BF16)
        w_gates = jnp.zeros((d, LANES), F32).at[:, :2 * MLSTM_HEADS].set(w_in[l, :, n_main:])
        wg_hi, wg_lo = _split_bf16(w_gates)
        proj, gates = _in_proj(x2d, mod, norm1_w[l].reshape(1, d), w_main, wg_hi, wg_lo, seq)

        qw2 = jnp.tile(q_norm_w[l], LANES // ATTN_HEAD_DIM).reshape(1, LANES)
        kw2 = jnp.tile(k_norm_w[l], LANES // ATTN_HEAD_DIM).reshape(1, LANES)
        attn = _attention(proj, qw2, kw2, bias_tab, batch, seq, attn_width)

        gate_b_row = jnp.zeros((1, LANES), F32).at[0, :2 * MLSTM_HEADS].set(gate_b[l])
        hm = _mlstm(proj, gates, conv_w[l], gate_b_row, h_norm_w[l].reshape(1, mlstm_width),
                    batch, seq, attn_width, dk, dv)

        w_r = jnp.concatenate([router_group_w[l], router_expert_w[l]], axis=1).T
        w_r = jnp.zeros((ROUTER_ROWS, d), F32).at[:w_r.shape[0]].set(w_r)
        wr_hi, wr_lo = _split_bf16(w_r)
        b_r = jnp.concatenate([router_group_b[l], router_expert_b[l]])
        b_r = jnp.zeros((ROUTER_ROWS, 1), F32).at[:b_r.shape[0], 0].set(b_r)
        x2d, h_packed, logits_t = _out_proj(
            attn.reshape(t, attn_width), hm.reshape(t, mlstm_width), w_out[l].astype(BF16),
            x2d, mod, norm2_w[l].reshape(1, d), wr_hi, wr_lo, b_r, seq)

        eidx, ew = _route(logits_t)
        block_e, block_valid, n_used, row_dst, row_w = _dispatch_tables(eidx, ew, t)
        y = _moe(block_e, block_valid, n_used, row_dst, row_w, h_packed,
                 w_gate[l].astype(BF16), w_up[l].astype(BF16), w_down[l].astype(BF16))
        x2d = _combine(x2d, y, mod, seq)

    return x2d.reshape(batch, seq, d)
```

```python
import functools

import jax
import jax.numpy as jnp
import numpy as np
from jax import lax
from jax.experimental import pallas as pl
from jax.experimental.pallas import tpu as pltpu

F32 = jnp.float32
BF16 = jnp.bfloat16

NORM_EPS = 1e-6
ATTN_HEAD_DIM = 64
ATTN_BLOCK = 128
ATTN_MAJOR = 16
LOG2E = 1.4426950408889634
DILATED_PATTERNS = ((128, 1), (512, 4), (2048, 16))
NUM_BUCKETS = 32
REL_MAX_DIST = 2048
MLSTM_HEADS = 4
MLSTM_CHUNK = 128
CONV_WIDTH = 4
MOE_GROUPS = 4
MOE_EXPERTS_PER_GROUP = 8
N_EXPERTS = MOE_GROUPS * MOE_EXPERTS_PER_GROUP
MOE_TOP_K = 2
LANES = 128
SUBLANES = 8
ROUTER_ROWS = 40
MOE_ROWS = 256
MASK_VALUE = -1e30
MiB = 1024 * 1024


def _cparams(sem, vmem_mib):
    return pltpu.CompilerParams(dimension_semantics=sem, vmem_limit_bytes=vmem_mib * MiB)


def _split_bf16(a):
    hi = a.astype(BF16)
    lo = (a - hi.astype(F32)).astype(BF16)
    return hi, lo


def _dot(a, b):
    return jnp.dot(a, b, preferred_element_type=F32)


def _dot_nt(a, b):
    return lax.dot_general(a, b, (((1,), (1,)), ((), ())), preferred_element_type=F32)


def _dot_tn(a, b):
    return lax.dot_general(a, b, (((0,), (0,)), ((), ())), preferred_element_type=F32)


def _dot3(a_f32, w_hi, w_lo):
    a_hi, a_lo = _split_bf16(a_f32)
    return _dot(a_hi, w_hi) + _dot(a_lo, w_hi) + _dot(a_hi, w_lo)


def _rms_modulate(x, norm_w, scale, shift):
    ms = jnp.mean(x * x, axis=-1, keepdims=True)
    return x * lax.rsqrt(ms + NORM_EPS) * norm_w * (1.0 + scale) + shift


def _mod_kernel(c_ref, w_ref, b_ref, o_ref):
    c = c_ref[...]
    a = c * jax.nn.sigmoid(c)
    w_hi, w_lo = _split_bf16(w_ref[...])
    o_ref[...] = _dot3(a, w_hi, w_lo) + b_ref[...]


def _modulation(c, w_mod, b_mod):
    depth, d, n = w_mod.shape
    b = c.shape[0]
    bp = -(-b // SUBLANES) * SUBLANES
    cp = jnp.zeros((bp, d), F32).at[:b].set(c)
    tn = 1024
    out = pl.pallas_call(
        _mod_kernel,
        out_shape=jax.ShapeDtypeStruct((depth, bp, n), F32),
        grid=(depth, n // tn),
        in_specs=[
            pl.BlockSpec((bp, d), lambda l, j: (0, 0)),
            pl.BlockSpec((None, d, tn), lambda l, j: (l, 0, j)),
            pl.BlockSpec((None, 1, tn), lambda l, j: (l, 0, j)),
        ],
        out_specs=pl.BlockSpec((None, bp, tn), lambda l, j: (l, 0, j)),
        compiler_params=_cparams(("arbitrary", "arbitrary"), 40),
        name="modulation",
    )(cp, w_mod, b_mod.reshape(depth, 1, n))
    return out[:, :b].reshape(depth, b, 6, d)


def _in_proj_kernel(x_ref, mod_ref, nw_ref, w_ref, wgh_ref, wgl_ref, o_ref, g_ref, h_scr):
    @pl.when(pl.program_id(1) == 0)
    def _():
        h = _rms_modulate(x_ref[...], nw_ref[...], mod_ref[1:2, :], mod_ref[0:1, :])
        h_scr[...] = h.astype(BF16)
        g_ref[...] = _dot3(h, wgh_ref[...], wgl_ref[...])

    o_ref[...] = _dot(h_scr[...], w_ref[...]).astype(BF16)


def _in_proj(x2d, mod, norm_w, w_main, wg_hi, wg_lo, seq):
    t, d = x2d.shape
    n = w_main.shape[1]
    tm, tn = 1024, 512
    blocks_per_seq = seq // tm
    return pl.pallas_call(
        _in_proj_kernel,
        out_shape=(jax.ShapeDtypeStruct((t, n), BF16), jax.ShapeDtypeStruct((t, LANES), F32)),
        grid=(t // tm, n // tn),
        in_specs=[
            pl.BlockSpec((tm, d), lambda i, j: (i, 0)),
            pl.BlockSpec((None, 6, d), lambda i, j: (i // blocks_per_seq, 0, 0)),
            pl.BlockSpec((1, d), lambda i, j: (0, 0)),
            pl.BlockSpec((d, tn), lambda i, j: (0, j)),
            pl.BlockSpec((d, LANES), lambda i, j: (0, 0)),
            pl.BlockSpec((d, LANES), lambda i, j: (0, 0)),
        ],
        out_specs=(pl.BlockSpec((tm, tn), lambda i, j: (i, j)),
                   pl.BlockSpec((tm, LANES), lambda i, j: (i, 0))),
        scratch_shapes=[pltpu.VMEM((tm, d), BF16)],
        compiler_params=_cparams(("arbitrary", "arbitrary"), 48),
        name="in_proj",
    )(x2d, mod, norm_w, w_main, wg_hi, wg_lo)


def _t5_causal_bucket(dist):
    max_exact = NUM_BUCKETS // 2
    d = np.maximum(dist, 1).astype(np.float32)
    large = max_exact + (np.log(d / max_exact) / np.log(REL_MAX_DIST / max_exact)
                         * (NUM_BUCKETS - max_exact)).astype(np.int32)
    large = np.minimum(large, NUM_BUCKETS - 1)
    return np.where(dist < max_exact, dist, large).astype(np.int32)


def _block_positions(dilation):
    g = ATTN_MAJOR // dilation
    plen = ATTN_BLOCK // g
    i = np.arange(ATTN_BLOCK)
    return g * (i % plen) + i // plen


def _attn_bias_tables(rel_bias):
    blk = ATTN_BLOCK
    tables = []
    for window, dilation in DILATED_PATTERNS:
        win = window // dilation
        loc = _block_positions(dilation)
        kpos = np.concatenate([loc, loc + blk])
        rel = loc[:, None] - kpos[None, :] + blk
        valid = (rel >= 0) & (rel <= win)
        bucket = _t5_causal_bucket(np.clip(rel, 0, win) * dilation)
        bias = jnp.transpose(rel_bias[bucket], (2, 0, 1)).astype(F32) * LOG2E
        normal = jnp.where(valid[None], bias, MASK_VALUE)
        first = jnp.where((valid & (kpos >= blk)[None, :])[None], bias, MASK_VALUE)
        tables.append(jnp.stack([normal, first], axis=0))
    return jnp.stack(tables, axis=0)


def _attn_kernel(q_ref, k_ref, v_ref, qw_ref, kw_ref, bias_ref, o_ref,
                 stage, qs, ks, vs, acc_s, m_s, l_s, *, seq, unroll):
    blk = ATTN_BLOCK
    major = ATTN_MAJOR
    l16 = seq // major
    lane = lax.broadcasted_iota(jnp.int32, (1, LANES), 1)
    head0 = lane < ATTN_HEAD_DIM
    gi = lax.broadcasted_iota(jnp.int32, (LANES, LANES), 0) // ATTN_HEAD_DIM
    gj = lax.broadcasted_iota(jnp.int32, (LANES, LANES), 1) // ATTN_HEAD_DIM
    group_ones = (gi == gj).astype(BF16)
    ones_cols = jnp.ones((2 * blk, LANES), BF16)
    rows = 256

    def to_major(dst):
        def body(r, carry):
            for h in range(l16 // blk):
                dst[pl.ds(pl.multiple_of(r * l16 + h * blk, blk), blk), :] = (
                    stage[pl.ds(r + major * blk * h, blk, stride=major), :])
            return carry
        lax.fori_loop(0, major, body, 0)

    for src, dst, w_ref, scale in ((q_ref, qs, qw_ref, ATTN_HEAD_DIM ** -0.5 * LOG2E),
                                   (k_ref, ks, kw_ref, 1.0), (v_ref, vs, None, None)):
        def norm_body(c, carry, src=src, w_ref=w_ref, scale=scale):
            sl = pl.ds(pl.multiple_of(c * rows, rows), rows)
            xx = src[sl, :].astype(F32)
            if w_ref is not None:
                hi, lo = _split_bf16(xx * xx)
                ssq = _dot(hi, group_ones) + _dot(lo, group_ones)
                xx = xx * lax.rsqrt(ssq * (1.0 / ATTN_HEAD_DIM) + NORM_EPS) * w_ref[...] * scale
            stage[sl, :] = xx
            return carry
        lax.fori_loop(0, seq // rows, norm_body, 0, unroll=4)
        to_major(dst)

    for p, (_, dil) in enumerate(DILATED_PATTERNS):
        groups = major // dil
        plen = blk // groups
        nb_log2 = ((seq // dil) // blk).bit_length() - 1

        def starts(r_d, n, dil=dil, groups=groups, plen=plen):
            return [pl.multiple_of((j * dil + r_d) * l16 + n * plen, SUBLANES) for j in range(groups)]

        def load_blk(ref, st, plen=plen):
            return jnp.concatenate([ref[pl.ds(s, plen), :] for s in st], axis=0)

        def store_blk(ref, st, val, plen=plen):
            for j, s in enumerate(st):
                ref[pl.ds(s, plen), :] = val[j * plen:(j + 1) * plen]

        def body(it, carry, p=p, nb_log2=nb_log2, starts=starts, load_blk=load_blk, store_blk=store_blk):
            loaded = []
            for u in range(unroll):
                idx = it * unroll + u
                r_d = lax.shift_right_logical(idx, nb_log2)
                n = idx - lax.shift_left(r_d, nb_log2)
                cur = starts(r_d, n)
                prv = starts(r_d, jnp.maximum(n - 1, 0))
                first = (n == 0).astype(jnp.int32)
                qb = load_blk(qs, cur).astype(BF16)
                kw = jnp.concatenate([load_blk(ks, prv), load_blk(ks, cur)], axis=0).astype(BF16)
                vw = jnp.concatenate([load_blk(vs, prv), load_blk(vs, cur)], axis=0).astype(BF16)
                old = (load_blk(m_s, cur), load_blk(l_s, cur), load_blk(acc_s, cur)) if p > 0 else None
                loaded.append((cur, first, qb, kw, vw, old))
            results = []
            for cur, first, qb, kw, vw, old in loaded:
                vw_aug = jnp.concatenate([vw, ones_cols], axis=1)
                ms, res = [], []
                for h, hmask in enumerate((head0, jnp.logical_not(head0))):
                    qh = jnp.where(hmask, qb, jnp.zeros_like(qb))
                    s = _dot_nt(qh, kw) + bias_ref[p, first, h]
                    mh = jnp.max(s, axis=-1, keepdims=True)
                    ms.append(mh)
                    res.append(_dot(jnp.exp2(s - mh).astype(BF16), vw_aug))
                m_b = jnp.where(head0, ms[0], ms[1])
                a_b = jnp.where(head0, res[0][:, :LANES], res[1][:, :LANES])
                l_b = jnp.where(head0, res[0][:, LANES:], res[1][:, LANES:])
                if old is None:
                    results.append((cur, m_b, l_b, a_b))
                else:
                    m_o, l_o, a_o = old
                    m_new = jnp.maximum(m_o, m_b)
                    w_o = jnp.exp2(m_o - m_new)
                    w_b = jnp.exp2(m_b - m_new)
                    results.append((cur, m_new, w_o * l_o + w_b * l_b, w_o * a_o + w_b * a_b))
            for cur, m_new, l_new, a_new in results:
                store_blk(m_s, cur, m_new)
                store_blk(l_s, cur, l_new)
                store_blk(acc_s, cur, a_new)
            return carry

        lax.fori_loop(0, (seq // blk) // unroll, body, 0)

    def from_major(r, carry):
        for h in range(l16 // blk):
            sl = pl.ds(pl.multiple_of(r * l16 + h * blk, blk), blk)
            stage[pl.ds(r + major * blk * h, blk, stride=major), :] = acc_s[sl, :] / l_s[sl, :]
        return carry

    lax.fori_loop(0, major, from_major, 0)

    def out_body(c, carry):
        sl = pl.ds(pl.multiple_of(c * rows, rows), rows)
        o_ref[sl, :] = stage[sl, :].astype(o_ref.dtype)
        return carry

    lax.fori_loop(0, seq // rows, out_body, 0)


def _attention(proj, qw2, kw2, bias_tab, batch, seq, attn_width):
    t, n = proj.shape
    proj3 = proj.reshape(batch, seq, n)
    pairs = attn_width // LANES
    npat = len(DILATED_PATTERNS)
    kern = functools.partial(_attn_kernel, seq=seq, unroll=8)
    return pl.pallas_call(
        kern,
        out_shape=jax.ShapeDtypeStruct((batch, seq, attn_width), BF16),
        grid=(batch, pairs),
        in_specs=[
            pl.BlockSpec((None, seq, LANES), lambda b, h: (b, 0, h)),
            pl.BlockSpec((None, seq, LANES), lambda b, h: (b, 0, pairs + h)),
            pl.BlockSpec((None, seq, LANES), lambda b, h: (b, 0, 2 * pairs + h)),
            pl.BlockSpec((1, LANES), lambda b, h: (0, 0)),
            pl.BlockSpec((1, LANES), lambda b, h: (0, 0)),
            pl.BlockSpec((npat, 2, 2, ATTN_BLOCK, 2 * ATTN_BLOCK), lambda b, h: (0, 0, h, 0, 0)),
        ],
        out_specs=pl.BlockSpec((None, seq, LANES), lambda b, h: (b, 0, h)),
        scratch_shapes=[pltpu.VMEM((seq, LANES), F32) for _ in range(7)],
        compiler_params=_cparams(("arbitrary", "arbitrary"), 40),
        name="dilated_attention",
    )(proj3, proj3, proj3, qw2, kw2, bias_tab)


def _mlstm_kernel(q_ref, k_ref, v_ref, og_ref, g_ref, cw_ref, gb_ref, hw_ref, o_ref,
                  x_scr, c_scr, m_scr, *, rows, dk, dv):
    ch = MLSTM_CHUNK
    nheads = MLSTM_HEADS
    qk_w = nheads * dk
    aug = dv + LANES

    @pl.when(pl.program_id(1) == 0)
    def _():
        x_scr[0:SUBLANES, :] = jnp.zeros((SUBLANES, 2 * qk_w), F32)
        c_scr[...] = jnp.zeros_like(c_scr)
        m_scr[...] = jnp.zeros_like(m_scr)

    x_scr[SUBLANES:, 0:qk_w] = q_ref[...].astype(F32)
    x_scr[SUBLANES:, qk_w:] = k_ref[...].astype(F32)

    ri = lax.broadcasted_iota(jnp.int32, (ch, ch), 0)
    ci = lax.broadcasted_iota(jnp.int32, (ch, ch), 1)
    causal = ci <= ri
    tril = causal.astype(BF16)
    row8 = lax.broadcasted_iota(jnp.int32, (SUBLANES, 1), 0)
    ones_col = (lax.broadcasted_iota(jnp.int32, (ch, LANES), 1) == 0).astype(F32)
    k_scale = dk ** -0.5

    def chunk_body(c, carry):
        r0 = pl.multiple_of(c * ch, ch)
        prev8 = x_scr[pl.ds(r0, SUBLANES), :]
        cur = x_scr[pl.ds(r0 + SUBLANES, ch), :]
        conv = cur * cw_ref[CONV_WIDTH - 1:CONV_WIDTH, :]
        for sh in range(1, CONV_WIDTH):
            rolled = pltpu.roll(cur, sh, 0)
            head_rows = jnp.where(row8 < sh, pltpu.roll(prev8, sh, 0), rolled[0:SUBLANES])
            shifted = jnp.concatenate([head_rows, rolled[SUBLANES:]], axis=0)
            conv = conv + shifted * cw_ref[CONV_WIDTH - 1 - sh:CONV_WIDTH - sh, :]
        qk = conv * jax.nn.sigmoid(conv)

        gates = g_ref[pl.ds(r0, ch), :] + gb_ref[...]
        logf = jax.nn.log_sigmoid(gates)
        lf_hi, lf_lo = _split_bf16(logf)
        bcum = _dot(tril, lf_hi) + _dot(tril, lf_lo)
        gates_t = gates.T
        bcum_t = bcum.T

        for h in range(nheads):
            q = qk[:, h * dk:(h + 1) * dk]
            k = qk[:, qk_w + h * dk:qk_w + (h + 1) * dk] * k_scale
            v = v_ref[pl.ds(r0, ch), h * dv:(h + 1) * dv].astype(F32)
            v_aug = jnp.concatenate([v, ones_col], axis=1)
            i_col = gates[:, h:h + 1]
            b_col = bcum[:, nheads + h:nheads + h + 1]
            i_row = gates_t[h:h + 1, :]
            b_row = bcum_t[nheads + h:nheads + h + 1, :]
            m_prev = m_scr[h, 0:1, 0:1]
            c_prev = c_scr[h]

            dlog = jnp.where(causal, b_col - b_row + i_row, -jnp.inf)
            inter = b_col + m_prev
            m_t = jnp.maximum(inter, jnp.max(dlog, axis=-1, keepdims=True))
            qb = q.astype(BF16)
            kb = k.astype(BF16)
            sm = _dot_nt(qb, kb) * jnp.exp(dlog - m_t)
            e_inter = jnp.exp(inter - m_t)
            tot = e_inter * _dot(qb, c_prev.astype(BF16)) + _dot(sm.astype(BF16), v_aug.astype(BF16))
            num = tot[:, 0:dv]
            den = tot[:, dv:dv + 1]
            hcell = num / jnp.maximum(jnp.abs(den), jnp.exp(-m_t))

            g_last = b_col[ch - 1:ch, :]
            a_col = g_last - b_col + i_col
            m_new = jnp.maximum(g_last + m_prev, jnp.max(a_col, axis=0, keepdims=True))
            decay = jnp.exp(g_last + m_prev - m_new)
            w_col = jnp.exp(a_col - m_new)
            c_scr[h] = decay * c_prev + _dot_tn(kb, (w_col * v_aug).astype(BF16))
            m_scr[h] = jnp.broadcast_to(m_new, (SUBLANES, LANES))

            ms = jnp.mean(hcell * hcell, axis=-1, keepdims=True)
            hn = hcell * lax.rsqrt(ms + NORM_EPS) * hw_ref[:, h * dv:(h + 1) * dv]
            og = og_ref[pl.ds(r0, ch), h * dv:(h + 1) * dv].astype(F32)
            o_ref[pl.ds(r0, ch), h * dv:(h + 1) * dv] = (jax.nn.sigmoid(og) * hn).astype(o_ref.dtype)
        return carry

    lax.fori_loop(0, rows // ch, chunk_body, 0)
    x_scr[0:SUBLANES, :] = x_scr[rows:rows + SUBLANES, :]


def _mlstm(proj, gates, conv_w, gate_b_row, h_norm_w, batch, seq, attn_width, dk, dv):
    t, n = proj.shape
    proj3 = proj.reshape(batch, seq, n)
    gates3 = gates.reshape(batch, seq, LANES)
    nheads = MLSTM_HEADS
    qk_w = nheads * dk
    v_w = nheads * dv
    rows = 512
    q_blk = (3 * attn_width) // qk_w
    v_blk = (3 * attn_width + 2 * qk_w) // v_w
    kern = functools.partial(_mlstm_kernel, rows=rows, dk=dk, dv=dv)
    return pl.pallas_call(
        kern,
        out_shape=jax.ShapeDtypeStruct((batch, seq, v_w), BF16),
        grid=(batch, seq // rows),
        in_specs=[
            pl.BlockSpec((None, rows, qk_w), lambda b, j: (b, j, q_blk)),
            pl.BlockSpec((None, rows, qk_w), lambda b, j: (b, j, q_blk + 1)),
            pl.BlockSpec((None, rows, v_w), lambda b, j: (b, j, v_blk)),
            pl.BlockSpec((None, rows, v_w), lambda b, j: (b, j, v_blk + 1)),
            pl.BlockSpec((None, rows, LANES), lambda b, j: (b, j, 0)),
            pl.BlockSpec((CONV_WIDTH, 2 * qk_w), lambda b, j: (0, 0)),
            pl.BlockSpec((1, LANES), lambda b, j: (0, 0)),
            pl.BlockSpec((1, v_w), lambda b, j: (0, 0)),
        ],
        out_specs=pl.BlockSpec((None, rows, v_w), lambda b, j: (b, j, 0)),
        scratch_shapes=[
            pltpu.VMEM((rows + SUBLANES, 2 * qk_w), F32),
            pltpu.VMEM((nheads, dk, dv + LANES), F32),
            pltpu.VMEM((nheads, SUBLANES, LANES), F32),
        ],
        compiler_params=_cparams(("arbitrary", "arbitrary"), 40),
        name="mlstm",
    )(proj3, proj3, proj3, proj3, gates3, conv_w, gate_b_row, h_norm_w)


def _store_token_tiles(ref, base, h):
    m, w2 = h.shape
    w = w2 // 2
    tile_rows = w // LANES
    u = pltpu.bitcast(h.astype(BF16).astype(F32), jnp.uint32)
    packed = (u[:, :w] >> 16) | (u[:, w:] & jnp.uint32(0xFFFF0000))
    for k in range(tile_rows):
        ref[pl.ds(base + k, m, stride=tile_rows), :] = packed[:, k * LANES:(k + 1) * LANES]


def _load_token_tiles(ref, base, m, tile_rows):
    lo, hi = [], []
    for k in range(tile_rows):
        p = ref[pl.ds(base + k, m, stride=tile_rows), :]
        lo.append(pltpu.bitcast(p << 16, F32))
        hi.append(pltpu.bitcast(p & jnp.uint32(0xFFFF0000), F32))
    return jnp.concatenate(lo, axis=1), jnp.concatenate(hi, axis=1)


def _out_proj_kernel(a_ref, m_ref, w_ref, x_ref, mod_ref, nw_ref, wrh_ref, wrl_ref, br_ref,
                     xo_ref, hp_ref, lg_ref, *, attn_width):
    mix = _dot(a_ref[...], w_ref[0:attn_width, :]) + _dot(m_ref[...], w_ref[attn_width:, :])
    xn = x_ref[...] + mod_ref[2:3, :] * mix
    xo_ref[...] = xn
    h2 = _rms_modulate(xn, nw_ref[...], mod_ref[4:5, :], mod_ref[3:4, :])
    _store_token_tiles(hp_ref, 0, h2)
    h_hi, h_lo = _split_bf16(h2)
    lg_ref[...] = (_dot_nt(wrh_ref[...], h_hi) + _dot_nt(wrh_ref[...], h_lo)
                   + _dot_nt(wrl_ref[...], h_hi) + br_ref[...])


def _out_proj(attn2d, hm2d, w_out, x2d, mod, norm_w, wr_hi, wr_lo, br, seq):
    t, d = x2d.shape
    aw = attn2d.shape[1]
    mw = hm2d.shape[1]
    tm = 256
    blocks_per_seq = seq // tm
    tile_rows = d // 2 // LANES
    kern = functools.partial(_out_proj_kernel, attn_width=aw)
    return pl.pallas_call(
        kern,
        out_shape=(jax.ShapeDtypeStruct((t, d), F32),
                   jax.ShapeDtypeStruct((t * tile_rows, LANES), jnp.uint32),
                   jax.ShapeDtypeStruct((ROUTER_ROWS, t), F32)),
        grid=(t // tm,),
        in_specs=[
            pl.BlockSpec((tm, aw), lambda i: (i, 0)),
            pl.BlockSpec((tm, mw), lambda i: (i, 0)),
            pl.BlockSpec((aw + mw, d), lambda i: (0, 0)),
            pl.BlockSpec((tm, d), lambda i: (i, 0)),
            pl.BlockSpec((None, 6, d), lambda i: (i // blocks_per_seq, 0, 0)),
            pl.BlockSpec((1, d), lambda i: (0, 0)),
            pl.BlockSpec((ROUTER_ROWS, d), lambda i: (0, 0)),
            pl.BlockSpec((ROUTER_ROWS, d), lambda i: (0, 0)),
            pl.BlockSpec((ROUTER_ROWS, 1), lambda i: (0, 0)),
        ],
        out_specs=(pl.BlockSpec((tm, d), lambda i: (i, 0)),
                   pl.BlockSpec((tm * tile_rows, LANES), lambda i: (i, 0)),
                   pl.BlockSpec((ROUTER_ROWS, tm), lambda i: (0, i))),
        compiler_params=_cparams(("arbitrary",), 48),
        name="out_proj",
    )(attn2d, hm2d, w_out, x2d, mod, norm_w, wr_hi, wr_lo, br)


def _route_kernel(lg_ref, idx_ref, w_ref):
    ng, ne = MOE_GROUPS, MOE_EXPERTS_PER_GROUP
    lg = [lg_ref[g:g + 1, :] for g in range(ng)]
    best, gsel = lg[0], jnp.zeros_like(lg[0], dtype=jnp.int32)
    for g in range(1, ng):
        better = lg[g] > best
        best = jnp.where(better, lg[g], best)
        gsel = jnp.where(better, g, gsel)
    denom = jnp.exp(lg[0] - best)
    for g in range(1, ng):
        denom = denom + jnp.exp(lg[g] - best)
    p_group = 1.0 / denom

    le = []
    for e in range(ne):
        v = lg_ref[ng + e:ng + e + 1, :]
        for g in range(1, ng):
            v = jnp.where(gsel == g, lg_ref[ng + g * ne + e:ng + g * ne + e + 1, :], v)
        le.append(v)
    v1, i1 = le[0], jnp.zeros_like(gsel)
    for e in range(1, ne):
        better = le[e] > v1
        v1 = jnp.where(better, le[e], v1)
        i1 = jnp.where(better, e, i1)
    v2 = jnp.full_like(v1, -jnp.inf)
    i2 = jnp.zeros_like(gsel)
    for e in range(ne):
        better = jnp.logical_and(le[e] > v2, i1 != e)
        v2 = jnp.where(better, le[e], v2)
        i2 = jnp.where(better, e, i2)
    e2 = jnp.exp(v2 - v1)
    inv = 1.0 / (1.0 + e2)
    tn = gsel.shape[1]
    zero_i = jnp.zeros((SUBLANES - MOE_TOP_K, tn), jnp.int32)
    idx_ref[...] = jnp.concatenate([gsel * ne + i1, gsel * ne + i2, zero_i], axis=0)
    w_rows = jnp.concatenate([p_group * inv, p_group * (e2 * inv), jnp.zeros((LANES - MOE_TOP_K, tn), F32)], axis=0)
    for c in range(tn // LANES):
        w_ref[c * LANES:(c + 1) * LANES, :] = w_rows[:, c * LANES:(c + 1) * LANES].T


def _route(logits_t):
    rows, t = logits_t.shape
    tn = 2048
    return pl.pallas_call(
        _route_kernel,
        out_shape=(jax.ShapeDtypeStruct((SUBLANES, t), jnp.int32),
                   jax.ShapeDtypeStruct((t, LANES), F32)),
        grid=(t // tn,),
        in_specs=[pl.BlockSpec((rows, tn), lambda i: (0, i))],
        out_specs=(pl.BlockSpec((SUBLANES, tn), lambda i: (0, i)),
                   pl.BlockSpec((tn, LANES), lambda i: (i, 0))),
        compiler_params=_cparams(("arbitrary",), 32),
        name="route",
    )(logits_t)


def _dispatch_tables(eidx, t):
    m = MOE_ROWS
    a_total = MOE_TOP_K * t
    n_blocks = a_total // m + N_EXPERTS
    n_rows = n_blocks * m
    src_bits = (a_total - 1).bit_length()
    experts = jnp.arange(N_EXPERTS, dtype=jnp.int32)
    e_flat = eidx[:MOE_TOP_K].reshape(a_total)
    counts = jnp.sum((e_flat[:, None] == experts[None, :]).astype(jnp.int32), axis=0)
    padded = (counts + m - 1) // m * m
    pad_end = jnp.cumsum(padded)
    pad_start = pad_end - padded
    start = jnp.cumsum(counts) - counts
    n_used = (pad_end[-1] // m).astype(jnp.int32)
    packed = lax.sort(e_flat * (1 << src_bits) + jnp.arange(a_total, dtype=jnp.int32))
    order = packed & ((1 << src_bits) - 1)

    def lookup(idx, table):
        return jnp.sum(jnp.where(idx[:, None] == experts[None, :], table[None, :], 0), axis=1)

    blk = jnp.arange(n_blocks, dtype=jnp.int32)
    block_e = jnp.minimum(jnp.sum((blk[:, None] * m >= pad_end[None, :]).astype(jnp.int32), axis=1), N_EXPERTS - 1)
    block_off = blk * m - lookup(block_e, pad_start)
    block_valid = jnp.clip(lookup(block_e, counts) - block_off, 0, m)
    block_valid = jnp.where(blk < n_used, block_valid, 0)
    sorted_pos = (lookup(block_e, start) + block_off)[:, None] + jnp.arange(m, dtype=jnp.int32)[None, :]
    valid = jnp.arange(m, dtype=jnp.int32)[None, :] < block_valid[:, None]
    row_dst = jnp.where(valid, order[jnp.clip(sorted_pos, 0, a_total - 1)], jnp.arange(m, dtype=jnp.int32)[None, :])
    block_e = jnp.where(blk < n_used, block_e, jnp.max(jnp.where(blk < n_used, block_e, 0)))
    return block_e, block_valid, n_used.reshape(1), row_dst


def _moe_kernel(be_ref, bv_ref, nu_ref, idx_hbm, h_hbm, wg_ref, wu_ref, wd_ref, out_hbm,
                idx_s, xbuf, ybuf, sem_i, sem_g, sem_s, *, tokens, ff_chunk):
    m = MOE_ROWS
    i = pl.program_id(0)
    n_used = nu_ref[0]
    slot = lax.rem(i, 2)
    other = 1 - slot
    ff = wg_ref.shape[2]
    half = wg_ref.shape[1] // 2
    tr = half // LANES
    slot_rows = m * tr

    def tile_rows_at(row):
        return pl.ds(pl.multiple_of(row * tr, tr), tr)

    def idx_copy(blk, s):
        return pltpu.make_async_copy(idx_hbm.at[blk], idx_s.at[s], sem_i.at[s])

    def issue_gather(s):
        def body(j, carry):
            tok = lax.rem(idx_s[s, j], tokens)
            pltpu.make_async_copy(h_hbm.at[tile_rows_at(tok)], xbuf.at[tile_rows_at(s * m + j)],
                                  sem_g.at[s]).start()
            return carry
        lax.fori_loop(0, m, body, 0, unroll=8)

    def wait_gather(s):
        dst = xbuf.at[pl.ds(pl.multiple_of(s * slot_rows, slot_rows), slot_rows)]
        pltpu.make_async_copy(h_hbm.at[pl.ds(0, slot_rows)], dst, sem_g.at[s]).wait()

    def issue_scatter(s, n_valid):
        def body(j, carry):
            dst = idx_s[s, j]
            pltpu.make_async_copy(ybuf.at[tile_rows_at(s * m + j)], out_hbm.at[tile_rows_at(dst)],
                                  sem_s.at[s]).start()
            return carry
        lax.fori_loop(0, n_valid, body, 0)

    def wait_scatter(s, n_valid):
        n = pl.multiple_of(n_valid * tr, tr)
        src = ybuf.at[pl.ds(pl.multiple_of(s * slot_rows, slot_rows), n)]
        pltpu.make_async_copy(src, out_hbm.at[pl.ds(0, n)], sem_s.at[s]).wait()

    @pl.when(i == 0)
    def _():
        idx_copy(0, 0).start()
        idx_copy(0, 0).wait()
        issue_gather(0)

        @pl.when(n_used > 1)
        def _():
            idx_copy(1, 1).start()

    @pl.when(i + 1 < n_used)
    def _():
        idx_copy(i + 1, other).wait()
        issue_gather(other)

    @pl.when(i < n_used)
    def _():
        wait_gather(slot)
        base = pl.multiple_of(slot * slot_rows, slot_rows)
        x_lo, x_hi = _load_token_tiles(xbuf, base, m, tr)
        x_lo, x_hi = x_lo.astype(BF16), x_hi.astype(BF16)
        acc = jnp.zeros((m, wd_ref.shape[2]), F32)
        for c in range(ff // ff_chunk):
            cs = slice(c * ff_chunk, (c + 1) * ff_chunk)
            g = _dot(x_lo, wg_ref[0, 0:half, cs]) + _dot(x_hi, wg_ref[0, half:, cs])
            u = _dot(x_lo, wu_ref[0, 0:half, cs]) + _dot(x_hi, wu_ref[0, half:, cs])
            act = (g * jax.nn.sigmoid(g) * u).astype(BF16)
            acc = acc + _dot(act, wd_ref[0, cs, :])

        @pl.when(i >= 1)
        def _():
            wait_scatter(other, bv_ref[jnp.maximum(i - 1, 0)])

        _store_token_tiles(ybuf, base, acc)
        issue_scatter(slot, bv_ref[i])

        @pl.when(i + 2 < n_used)
        def _():
            idx_copy(i + 2, slot).start()

        @pl.when(i == n_used - 1)
        def _():
            wait_scatter(slot, bv_ref[i])


def _moe(block_e, block_valid, n_used, row_dst, h_packed, wg, wu, wd):
    d, ff = wg.shape[1], wg.shape[2]
    tr = d // 2 // LANES
    t = h_packed.shape[0] // tr
    m = MOE_ROWS
    n_blocks = row_dst.shape[0]
    kern = functools.partial(_moe_kernel, tokens=t, ff_chunk=256)
    grid_spec = pltpu.PrefetchScalarGridSpec(
        num_scalar_prefetch=3,
        grid=(n_blocks,),
        in_specs=[
            pl.BlockSpec(memory_space=pl.ANY),
            pl.BlockSpec(memory_space=pl.ANY),
            pl.BlockSpec((1, d, ff), lambda i, be, bv, nu: (be[i], 0, 0)),
            pl.BlockSpec((1, d, ff), lambda i, be, bv, nu: (be[i], 0, 0)),
            pl.BlockSpec((1, ff, d), lambda i, be, bv, nu: (be[i], 0, 0)),
        ],
        out_specs=pl.BlockSpec(memory_space=pl.ANY),
        scratch_shapes=[
            pltpu.SMEM((2, m), jnp.int32),
            pltpu.VMEM((2 * m * tr, LANES), jnp.uint32),
            pltpu.VMEM((2 * m * tr, LANES), jnp.uint32),
            pltpu.SemaphoreType.DMA((2,)),
            pltpu.SemaphoreType.DMA((2,)),
            pltpu.SemaphoreType.DMA((2,)),
        ],
    )
    return pl.pallas_call(
        kern,
        out_shape=jax.ShapeDtypeStruct((MOE_TOP_K * t * tr, LANES), jnp.uint32),
        grid_spec=grid_spec,
        compiler_params=_cparams(("arbitrary",), 56),
        name="moe_experts",
    )(block_e, block_valid, n_used, row_dst, h_packed, wg, wu, wd)


def _combine_kernel(x_ref, y0_ref, y1_ref, w_ref, mod_ref, o_ref):
    tm, d = x_ref.shape
    half = d // 2
    tr = half // LANES
    w0 = w_ref[:, 0:1]
    w1 = w_ref[:, 1:2]
    lo0, hi0 = _load_token_tiles(y0_ref, 0, tm, tr)
    lo1, hi1 = _load_token_tiles(y1_ref, 0, tm, tr)
    o_ref[:, 0:half] = x_ref[:, 0:half] + mod_ref[5:6, 0:half] * (w0 * lo0 + w1 * lo1)
    o_ref[:, half:] = x_ref[:, half:] + mod_ref[5:6, half:] * (w0 * hi0 + w1 * hi1)


def _combine(x2d, y, w_tok, mod, seq):
    t, d = x2d.shape
    tm = 512
    tr = d // 2 // LANES
    blocks_per_seq = seq // tm
    nblk = t // tm
    return pl.pallas_call(
        _combine_kernel,
        out_shape=jax.ShapeDtypeStruct((t, d), F32),
        grid=(nblk,),
        in_specs=[
            pl.BlockSpec((tm, d), lambda i: (i, 0)),
            pl.BlockSpec((tm * tr, LANES), lambda i: (i, 0)),
            pl.BlockSpec((tm * tr, LANES), lambda i: (nblk + i, 0)),
            pl.BlockSpec((tm, LANES), lambda i: (i, 0)),
            pl.BlockSpec((None, 6, d), lambda i: (i // blocks_per_seq, 0, 0)),
        ],
        out_specs=pl.BlockSpec((tm, d), lambda i: (i, 0)),
        compiler_params=_cparams(("arbitrary",), 48),
        name="combine",
    )(x2d, y, y, w_tok, mod)


def kernel(x, c, rel_bias, w_mod, b_mod, norm1_w, norm2_w, w_in, q_norm_w, k_norm_w, conv_w, gate_b, h_norm_w, w_out, router_group_w, router_group_b, router_expert_w, router_expert_b, w_gate, w_up, w_down):
    batch, seq, d = x.shape
    depth = w_mod.shape[0]
    t = batch * seq
    attn_width = d // 2
    mlstm_width = d - attn_width
    dv = mlstm_width // MLSTM_HEADS
    dk = dv // 2
    n_main = 3 * attn_width + 2 * MLSTM_HEADS * dk + 2 * mlstm_width
    assert dk == LANES and attn_width % LANES == 0 and seq % 1024 == 0 and t % MOE_ROWS == 0
    assert all((seq // dil) % ATTN_BLOCK == 0 and window // dil == ATTN_BLOCK for window, dil in DILATED_PATTERNS)

    mod_all = _modulation(c, w_mod, b_mod)
    bias_tab = _attn_bias_tables(rel_bias)
    x2d = x.reshape(t, d)

    for l in range(depth):
        mod = mod_all[l]
        w_main = w_in[l, :, :n_main].astype(BF16)
        w_gates = jnp.zeros((d, LANES), F32).at[:, :2 * MLSTM_HEADS].set(w_in[l, :, n_main:])
        wg_hi, wg_lo = _split_bf16(w_gates)
        proj, gates = _in_proj(x2d, mod, norm1_w[l].reshape(1, d), w_main, wg_hi, wg_lo, seq)

        qw2 = jnp.tile(q_norm_w[l], LANES // ATTN_HEAD_DIM).reshape(1, LANES)
        kw2 = jnp.tile(k_norm_w[l], LANES // ATTN_HEAD_DIM).reshape(1, LANES)
        attn = _attention(proj, qw2, kw2, bias_tab, batch, seq, attn_width)

        gate_b_row = jnp.zeros((1, LANES), F32).at[0, :2 * MLSTM_HEADS].set(gate_b[l])
        hm = _mlstm(proj, gates, conv_w[l], gate_b_row, h_norm_w[l].reshape(1, mlstm_width),
                    batch, seq, attn_width, dk, dv)

        w_r = jnp.concatenate([router_group_w[l], router_expert_w[l]], axis=1).T
        w_r = jnp.zeros((ROUTER_ROWS, d), F32).at[:w_r.shape[0]].set(w_r)
        wr_hi, wr_lo = _split_bf16(w_r)
        b_r = jnp.concatenate([router_group_b[l], router_expert_b[l]])
        b_r = jnp.zeros((ROUTER_ROWS, 1), F32).at[:b_r.shape[0], 0].set(b_r)
        x2d, h_packed, logits_t = _out_proj(
            attn.reshape(t, attn_width), hm.reshape(t, mlstm_width), w_out[l].astype(BF16),
            x2d, mod, norm2_w[l].reshape(1, d), wr_hi, wr_lo, b_r, seq)

        eidx, w_tok = _route(logits_t)
        block_e, block_valid, n_used, row_dst = _dispatch_tables(eidx, t)
        y = _moe(block_e, block_valid, n_used, row_dst, h_packed,
                 w_gate[l].astype(BF16), w_up[l].astype(BF16), w_down[l].astype(BF16))
        x2d = _combine(x2d, y, w_tok, mod, seq)

    return x2d.reshape(batch, seq, d)
```

```python
import functools

import jax
import jax.numpy as jnp
import numpy as np
from jax import lax
from jax.experimental import pallas as pl
from jax.experimental.pallas import tpu as pltpu

F32 = jnp.float32
BF16 = jnp.bfloat16

NORM_EPS = 1e-6
ATTN_HEAD_DIM = 64
ATTN_BLOCK = 128
ATTN_MAJOR = 16
LOG2E = 1.4426950408889634
DILATED_PATTERNS = ((128, 1), (512, 4), (2048, 16))
NUM_BUCKETS = 32
REL_MAX_DIST = 2048
MLSTM_HEADS = 4
MLSTM_CHUNK = 128
CONV_WIDTH = 4
MOE_GROUPS = 4
MOE_EXPERTS_PER_GROUP = 8
N_EXPERTS = MOE_GROUPS * MOE_EXPERTS_PER_GROUP
MOE_TOP_K = 2
LANES = 128
SUBLANES = 8
ROUTER_ROWS = 40
MOE_ROWS = 256
MOE_IDX_SLOTS = 4
MOE_TABLE_TAIL = 2
MASK_VALUE = -1e30
MiB = 1024 * 1024


def _cparams(sem, vmem_mib):
    return pltpu.CompilerParams(dimension_semantics=sem, vmem_limit_bytes=vmem_mib * MiB)


def _split_bf16(a):
    hi = a.astype(BF16)
    lo = (a - hi.astype(F32)).astype(BF16)
    return hi, lo


def _dot(a, b):
    return jnp.dot(a, b, preferred_element_type=F32)


def _dot_nt(a, b):
    return lax.dot_general(a, b, (((1,), (1,)), ((), ())), preferred_element_type=F32)


def _dot_tn(a, b):
    return lax.dot_general(a, b, (((0,), (0,)), ((), ())), preferred_element_type=F32)


def _dot3(a_f32, w_hi, w_lo):
    a_hi, a_lo = _split_bf16(a_f32)
    return _dot(a_hi, w_hi) + _dot(a_lo, w_hi) + _dot(a_hi, w_lo)


def _rms_modulate(x, norm_w, scale, shift):
    ms = jnp.mean(x * x, axis=-1, keepdims=True)
    return x * lax.rsqrt(ms + NORM_EPS) * norm_w * (1.0 + scale) + shift


def _mod_kernel(c_ref, w_ref, b_ref, o_ref):
    c = c_ref[...]
    a = c * jax.nn.sigmoid(c)
    w_hi, w_lo = _split_bf16(w_ref[...])
    o_ref[...] = _dot3(a, w_hi, w_lo) + b_ref[...]


def _modulation(c, w_mod, b_mod):
    depth, d, n = w_mod.shape
    b = c.shape[0]
    bp = -(-b // SUBLANES) * SUBLANES
    cp = jnp.zeros((bp, d), F32).at[:b].set(c)
    tn = 1024
    out = pl.pallas_call(
        _mod_kernel,
        out_shape=jax.ShapeDtypeStruct((depth, bp, n), F32),
        grid=(depth, n // tn),
        in_specs=[
            pl.BlockSpec((bp, d), lambda l, j: (0, 0)),
            pl.BlockSpec((None, d, tn), lambda l, j: (l, 0, j)),
            pl.BlockSpec((None, 1, tn), lambda l, j: (l, 0, j)),
        ],
        out_specs=pl.BlockSpec((None, bp, tn), lambda l, j: (l, 0, j)),
        compiler_params=_cparams(("arbitrary", "arbitrary"), 40),
        name="modulation",
    )(cp, w_mod, b_mod.reshape(depth, 1, n))
    return out[:, :b].reshape(depth, b, 6, d)


def _in_proj_kernel(x_ref, mod_ref, nw_ref, w_ref, wgh_ref, wgl_ref, o_ref, g_ref, h_scr):
    @pl.when(pl.program_id(1) == 0)
    def _():
        h = _rms_modulate(x_ref[...], nw_ref[...], mod_ref[1:2, :], mod_ref[0:1, :])
        h_scr[...] = h.astype(BF16)
        g_ref[...] = _dot3(h, wgh_ref[...], wgl_ref[...])

    o_ref[...] = _dot(h_scr[...], w_ref[...]).astype(BF16)


def _in_proj(x2d, mod, norm_w, w_main, wg_hi, wg_lo, seq):
    t, d = x2d.shape
    n = w_main.shape[1]
    tm, tn = 1024, 512
    blocks_per_seq = seq // tm
    return pl.pallas_call(
        _in_proj_kernel,
        out_shape=(jax.ShapeDtypeStruct((t, n), BF16), jax.ShapeDtypeStruct((t, LANES), F32)),
        grid=(t // tm, n // tn),
        in_specs=[
            pl.BlockSpec((tm, d), lambda i, j: (i, 0)),
            pl.BlockSpec((None, 6, d), lambda i, j: (i // blocks_per_seq, 0, 0)),
            pl.BlockSpec((1, d), lambda i, j: (0, 0)),
            pl.BlockSpec((d, tn), lambda i, j: (0, j)),
            pl.BlockSpec((d, LANES), lambda i, j: (0, 0)),
            pl.BlockSpec((d, LANES), lambda i, j: (0, 0)),
        ],
        out_specs=(pl.BlockSpec((tm, tn), lambda i, j: (i, j)),
                   pl.BlockSpec((tm, LANES), lambda i, j: (i, 0))),
        scratch_shapes=[pltpu.VMEM((tm, d), BF16)],
        compiler_params=_cparams(("arbitrary", "arbitrary"), 48),
        name="in_proj",
    )(x2d, mod, norm_w, w_main, wg_hi, wg_lo)


def _t5_causal_bucket(dist):
    max_exact = NUM_BUCKETS // 2
    d = np.maximum(dist, 1).astype(np.float32)
    large = max_exact + (np.log(d / max_exact) / np.log(REL_MAX_DIST / max_exact)
                         * (NUM_BUCKETS - max_exact)).astype(np.int32)
    large = np.minimum(large, NUM_BUCKETS - 1)
    return np.where(dist < max_exact, dist, large).astype(np.int32)


def _block_positions(dilation):
    g = ATTN_MAJOR // dilation
    plen = ATTN_BLOCK // g
    i = np.arange(ATTN_BLOCK)
    return g * (i % plen) + i // plen


def _attn_bias_tables(rel_bias):
    blk = ATTN_BLOCK
    tables = []
    for window, dilation in DILATED_PATTERNS:
        win = window // dilation
        loc = _block_positions(dilation)
        kpos = np.concatenate([loc, loc + blk])
        rel = loc[:, None] - kpos[None, :] + blk
        valid = (rel >= 0) & (rel <= win)
        bucket = _t5_causal_bucket(np.clip(rel, 0, win) * dilation)
        onehot = (jnp.asarray(bucket)[:, :, None] == jnp.arange(NUM_BUCKETS, dtype=jnp.int32)).astype(F32)
        bias = jnp.einsum('icb,bh->hic', onehot, rel_bias.astype(F32), precision=lax.Precision.HIGHEST) * LOG2E
        normal = jnp.where(valid[None], bias, MASK_VALUE)
        first = jnp.where((valid & (kpos >= blk)[None, :])[None], bias, MASK_VALUE)
        tables.append(jnp.stack([normal, first], axis=0))
    return jnp.stack(tables, axis=0)


def _attn_kernel(q_ref, k_ref, v_ref, qw_ref, kw_ref, bias_ref, o_ref,
                 stage, qs, ks, vs, acc_s, m_s, l_s, *, seq, unroll):
    blk = ATTN_BLOCK
    major = ATTN_MAJOR
    l16 = seq // major
    lane = lax.broadcasted_iota(jnp.int32, (1, LANES), 1)
    head0 = lane < ATTN_HEAD_DIM
    gi = lax.broadcasted_iota(jnp.int32, (LANES, LANES), 0) // ATTN_HEAD_DIM
    gj = lax.broadcasted_iota(jnp.int32, (LANES, LANES), 1) // ATTN_HEAD_DIM
    group_ones = (gi == gj).astype(BF16)
    ones_cols = jnp.ones((2 * blk, LANES), BF16)
    rows = 256

    def to_major(dst):
        def body(r, carry):
            for h in range(l16 // blk):
                dst[pl.ds(pl.multiple_of(r * l16 + h * blk, blk), blk), :] = (
                    stage[pl.ds(r + major * blk * h, blk, stride=major), :])
            return carry
        lax.fori_loop(0, major, body, 0)

    for src, dst, w_ref, scale in ((q_ref, qs, qw_ref, ATTN_HEAD_DIM ** -0.5 * LOG2E),
                                   (k_ref, ks, kw_ref, 1.0), (v_ref, vs, None, None)):
        def norm_body(c, carry, src=src, w_ref=w_ref, scale=scale):
            sl = pl.ds(pl.multiple_of(c * rows, rows), rows)
            xx = src[sl, :].astype(F32)
            if w_ref is not None:
                hi, lo = _split_bf16(xx * xx)
                ssq = _dot(hi, group_ones) + _dot(lo, group_ones)
                xx = xx * lax.rsqrt(ssq * (1.0 / ATTN_HEAD_DIM) + NORM_EPS) * w_ref[...] * scale
            stage[sl, :] = xx
            return carry
        lax.fori_loop(0, seq // rows, norm_body, 0, unroll=4)
        to_major(dst)

    for p, (_, dil) in enumerate(DILATED_PATTERNS):
        groups = major // dil
        plen = blk // groups
        nb_log2 = ((seq // dil) // blk).bit_length() - 1

        def starts(r_d, n, dil=dil, groups=groups, plen=plen):
            return [pl.multiple_of((j * dil + r_d) * l16 + n * plen, SUBLANES) for j in range(groups)]

        def load_blk(ref, st, plen=plen):
            return jnp.concatenate([ref[pl.ds(s, plen), :] for s in st], axis=0)

        def store_blk(ref, st, val, plen=plen):
            for j, s in enumerate(st):
                ref[pl.ds(s, plen), :] = val[j * plen:(j + 1) * plen]

        def body(it, carry, p=p, nb_log2=nb_log2, starts=starts, load_blk=load_blk, store_blk=store_blk):
            loaded = []
            for u in range(unroll):
                idx = it * unroll + u
                r_d = lax.shift_right_logical(idx, nb_log2)
                n = idx - lax.shift_left(r_d, nb_log2)
                cur = starts(r_d, n)
                prv = starts(r_d, jnp.maximum(n - 1, 0))
                first = (n == 0).astype(jnp.int32)
                qb = load_blk(qs, cur).astype(BF16)
                kw = jnp.concatenate([load_blk(ks, prv), load_blk(ks, cur)], axis=0).astype(BF16)
                vw = jnp.concatenate([load_blk(vs, prv), load_blk(vs, cur)], axis=0).astype(BF16)
                old = (load_blk(m_s, cur), load_blk(l_s, cur), load_blk(acc_s, cur)) if p > 0 else None
                loaded.append((cur, first, qb, kw, vw, old))
            results = []
            for cur, first, qb, kw, vw, old in loaded:
                vw_aug = jnp.concatenate([vw, ones_cols], axis=1)
                ms, res = [], []
                for h, hmask in enumerate((head0, jnp.logical_not(head0))):
                    qh = jnp.where(hmask, qb, jnp.zeros_like(qb))
                    s = _dot_nt(qh, kw) + bias_ref[p, first, h]
                    mh = jnp.max(s, axis=-1, keepdims=True)
                    ms.append(mh)
                    res.append(_dot(jnp.exp2(s - mh).astype(BF16), vw_aug))
                m_b = jnp.where(head0, ms[0], ms[1])
                a_b = jnp.where(head0, res[0][:, :LANES], res[1][:, :LANES])
                l_b = jnp.where(head0, res[0][:, LANES:], res[1][:, LANES:])
                if old is None:
                    results.append((cur, m_b, l_b, a_b))
                else:
                    m_o, l_o, a_o = old
                    m_new = jnp.maximum(m_o, m_b)
                    w_o = jnp.exp2(m_o - m_new)
                    w_b = jnp.exp2(m_b - m_new)
                    results.append((cur, m_new, w_o * l_o + w_b * l_b, w_o * a_o + w_b * a_b))
            for cur, m_new, l_new, a_new in results:
                store_blk(m_s, cur, m_new)
                store_blk(l_s, cur, l_new)
                store_blk(acc_s, cur, a_new)
            return carry

        lax.fori_loop(0, (seq // blk) // unroll, body, 0)

    def from_major(r, carry):
        for h in range(l16 // blk):
            sl = pl.ds(pl.multiple_of(r * l16 + h * blk, blk), blk)
            stage[pl.ds(r + major * blk * h, blk, stride=major), :] = acc_s[sl, :] / l_s[sl, :]
        return carry

    lax.fori_loop(0, major, from_major, 0)

    def out_body(c, carry):
        sl = pl.ds(pl.multiple_of(c * rows, rows), rows)
        o_ref[sl, :] = stage[sl, :].astype(o_ref.dtype)
        return carry

    lax.fori_loop(0, seq // rows, out_body, 0)


def _attention(proj, qw2, kw2, bias_tab, batch, seq, attn_width):
    t, n = proj.shape
    proj3 = proj.reshape(batch, seq, n)
    pairs = attn_width // LANES
    npat = len(DILATED_PATTERNS)
    kern = functools.partial(_attn_kernel, seq=seq, unroll=8)
    return pl.pallas_call(
        kern,
        out_shape=jax.ShapeDtypeStruct((batch, seq, attn_width), BF16),
        grid=(batch, pairs),
        in_specs=[
            pl.BlockSpec((None, seq, LANES), lambda b, h: (b, 0, h)),
            pl.BlockSpec((None, seq, LANES), lambda b, h: (b, 0, pairs + h)),
            pl.BlockSpec((None, seq, LANES), lambda b, h: (b, 0, 2 * pairs + h)),
            pl.BlockSpec((1, LANES), lambda b, h: (0, 0)),
            pl.BlockSpec((1, LANES), lambda b, h: (0, 0)),
            pl.BlockSpec((npat, 2, 2, ATTN_BLOCK, 2 * ATTN_BLOCK), lambda b, h: (0, 0, h, 0, 0)),
        ],
        out_specs=pl.BlockSpec((None, seq, LANES), lambda b, h: (b, 0, h)),
        scratch_shapes=[pltpu.VMEM((seq, LANES), F32) for _ in range(7)],
        compiler_params=_cparams(("arbitrary", "arbitrary"), 40),
        name="dilated_attention",
    )(proj3, proj3, proj3, qw2, kw2, bias_tab)


def _mlstm_kernel(q_ref, k_ref, v_ref, og_ref, g_ref, cw_ref, gb_ref, hw_ref, o_ref,
                  x_scr, c_scr, m_scr, *, rows, dk, dv):
    ch = MLSTM_CHUNK
    nheads = MLSTM_HEADS
    qk_w = nheads * dk
    aug = dv + LANES

    @pl.when(pl.program_id(1) == 0)
    def _():
        x_scr[0:SUBLANES, :] = jnp.zeros((SUBLANES, 2 * qk_w), F32)
        c_scr[...] = jnp.zeros_like(c_scr)
        m_scr[...] = jnp.zeros_like(m_scr)

    x_scr[SUBLANES:, 0:qk_w] = q_ref[...].astype(F32)
    x_scr[SUBLANES:, qk_w:] = k_ref[...].astype(F32)

    ri = lax.broadcasted_iota(jnp.int32, (ch, ch), 0)
    ci = lax.broadcasted_iota(jnp.int32, (ch, ch), 1)
    causal = ci <= ri
    tril = causal.astype(BF16)
    row8 = lax.broadcasted_iota(jnp.int32, (SUBLANES, 1), 0)
    ones_col = (lax.broadcasted_iota(jnp.int32, (ch, LANES), 1) == 0).astype(F32)
    k_scale = dk ** -0.5

    def chunk_body(c, carry):
        r0 = pl.multiple_of(c * ch, ch)
        prev8 = x_scr[pl.ds(r0, SUBLANES), :]
        cur = x_scr[pl.ds(r0 + SUBLANES, ch), :]
        conv = cur * cw_ref[CONV_WIDTH - 1:CONV_WIDTH, :]
        for sh in range(1, CONV_WIDTH):
            rolled = pltpu.roll(cur, sh, 0)
            head_rows = jnp.where(row8 < sh, pltpu.roll(prev8, sh, 0), rolled[0:SUBLANES])
            shifted = jnp.concatenate([head_rows, rolled[SUBLANES:]], axis=0)
            conv = conv + shifted * cw_ref[CONV_WIDTH - 1 - sh:CONV_WIDTH - sh, :]
        qk = conv * jax.nn.sigmoid(conv)

        gates = g_ref[pl.ds(r0, ch), :] + gb_ref[...]
        logf = jax.nn.log_sigmoid(gates)
        lf_hi, lf_lo = _split_bf16(logf)
        bcum = _dot(tril, lf_hi) + _dot(tril, lf_lo)
        gates_t = gates.T
        bcum_t = bcum.T

        for h in range(nheads):
            q = qk[:, h * dk:(h + 1) * dk]
            k = qk[:, qk_w + h * dk:qk_w + (h + 1) * dk] * k_scale
            v = v_ref[pl.ds(r0, ch), h * dv:(h + 1) * dv].astype(F32)
            v_aug = jnp.concatenate([v, ones_col], axis=1)
            i_col = gates[:, h:h + 1]
            b_col = bcum[:, nheads + h:nheads + h + 1]
            i_row = gates_t[h:h + 1, :]
            b_row = bcum_t[nheads + h:nheads + h + 1, :]
            m_prev = m_scr[h, 0:1, 0:1]
            c_prev = c_scr[h]

            dlog = jnp.where(causal, b_col - b_row + i_row, -jnp.inf)
            inter = b_col + m_prev
            m_t = jnp.maximum(inter, jnp.max(dlog, axis=-1, keepdims=True))
            qb = q.astype(BF16)
            kb = k.astype(BF16)
            sm = _dot_nt(qb, kb) * jnp.exp(dlog - m_t)
            e_inter = jnp.exp(inter - m_t)
            tot = e_inter * _dot(qb, c_prev.astype(BF16)) + _dot(sm.astype(BF16), v_aug.astype(BF16))
            num = tot[:, 0:dv]
            den = tot[:, dv:dv + 1]
            hcell = num / jnp.maximum(jnp.abs(den), jnp.exp(-m_t))

            g_last = b_col[ch - 1:ch, :]
            a_col = g_last - b_col + i_col
            m_new = jnp.maximum(g_last + m_prev, jnp.max(a_col, axis=0, keepdims=True))
            decay = jnp.exp(g_last + m_prev - m_new)
            w_col = jnp.exp(a_col - m_new)
            c_scr[h] = decay * c_prev + _dot_tn(kb, (w_col * v_aug).astype(BF16))
            m_scr[h] = jnp.broadcast_to(m_new, (SUBLANES, LANES))

            ms = jnp.mean(hcell * hcell, axis=-1, keepdims=True)
            hn = hcell * lax.rsqrt(ms + NORM_EPS) * hw_ref[:, h * dv:(h + 1) * dv]
            og = og_ref[pl.ds(r0, ch), h * dv:(h + 1) * dv].astype(F32)
            o_ref[pl.ds(r0, ch), h * dv:(h + 1) * dv] = (jax.nn.sigmoid(og) * hn).astype(o_ref.dtype)
        return carry

    lax.fori_loop(0, rows // ch, chunk_body, 0)
    x_scr[0:SUBLANES, :] = x_scr[rows:rows + SUBLANES, :]


def _mlstm(proj, gates, conv_w, gate_b_row, h_norm_w, batch, seq, attn_width, dk, dv):
    t, n = proj.shape
    proj3 = proj.reshape(batch, seq, n)
    gates3 = gates.reshape(batch, seq, LANES)
    nheads = MLSTM_HEADS
    qk_w = nheads * dk
    v_w = nheads * dv
    rows = 512
    q_blk = (3 * attn_width) // qk_w
    v_blk = (3 * attn_width + 2 * qk_w) // v_w
    kern = functools.partial(_mlstm_kernel, rows=rows, dk=dk, dv=dv)
    return pl.pallas_call(
        kern,
        out_shape=jax.ShapeDtypeStruct((batch, seq, v_w), BF16),
        grid=(batch, seq // rows),
        in_specs=[
            pl.BlockSpec((None, rows, qk_w), lambda b, j: (b, j, q_blk)),
            pl.BlockSpec((None, rows, qk_w), lambda b, j: (b, j, q_blk + 1)),
            pl.BlockSpec((None, rows, v_w), lambda b, j: (b, j, v_blk)),
            pl.BlockSpec((None, rows, v_w), lambda b, j: (b, j, v_blk + 1)),
            pl.BlockSpec((None, rows, LANES), lambda b, j: (b, j, 0)),
            pl.BlockSpec((CONV_WIDTH, 2 * qk_w), lambda b, j: (0, 0)),
            pl.BlockSpec((1, LANES), lambda b, j: (0, 0)),
            pl.BlockSpec((1, v_w), lambda b, j: (0, 0)),
        ],
        out_specs=pl.BlockSpec((None, rows, v_w), lambda b, j: (b, j, 0)),
        scratch_shapes=[
            pltpu.VMEM((rows + SUBLANES, 2 * qk_w), F32),
            pltpu.VMEM((nheads, dk, dv + LANES), F32),
            pltpu.VMEM((nheads, SUBLANES, LANES), F32),
        ],
        compiler_params=_cparams(("arbitrary", "arbitrary"), 40),
        name="mlstm",
    )(proj3, proj3, proj3, proj3, gates3, conv_w, gate_b_row, h_norm_w)


def _store_token_tiles(ref, base, h):
    m, w2 = h.shape
    w = w2 // 2
    tile_rows = w // LANES
    u = pltpu.bitcast(h.astype(BF16).astype(F32), jnp.uint32)
    packed = (u[:, :w] >> 16) | (u[:, w:] & jnp.uint32(0xFFFF0000))
    for k in range(tile_rows):
        ref[pl.ds(base + k, m, stride=tile_rows), :] = packed[:, k * LANES:(k + 1) * LANES]


def _load_token_tiles(ref, base, m, tile_rows):
    lo, hi = [], []
    for k in range(tile_rows):
        p = ref[pl.ds(base + k, m, stride=tile_rows), :]
        lo.append(pltpu.bitcast(p << 16, F32))
        hi.append(pltpu.bitcast(p & jnp.uint32(0xFFFF0000), F32))
    return jnp.concatenate(lo, axis=1), jnp.concatenate(hi, axis=1)


def _out_proj_kernel(a_ref, m_ref, w_ref, x_ref, mod_ref, nw_ref, wrh_ref, wrl_ref, br_ref,
                     xo_ref, hp_ref, lg_ref, *, attn_width):
    mix = _dot(a_ref[...], w_ref[0:attn_width, :]) + _dot(m_ref[...], w_ref[attn_width:, :])
    xn = x_ref[...] + mod_ref[2:3, :] * mix
    xo_ref[...] = xn
    h2 = _rms_modulate(xn, nw_ref[...], mod_ref[4:5, :], mod_ref[3:4, :])
    _store_token_tiles(hp_ref, 0, h2)
    h_hi, h_lo = _split_bf16(h2)
    lg_ref[...] = (_dot_nt(wrh_ref[...], h_hi) + _dot_nt(wrh_ref[...], h_lo)
                   + _dot_nt(wrl_ref[...], h_hi) + br_ref[...])


def _out_proj(attn2d, hm2d, w_out, x2d, mod, norm_w, wr_hi, wr_lo, br, seq):
    t, d = x2d.shape
    aw = attn2d.shape[1]
    mw = hm2d.shape[1]
    tm = 256
    blocks_per_seq = seq // tm
    tile_rows = d // 2 // LANES
    kern = functools.partial(_out_proj_kernel, attn_width=aw)
    return pl.pallas_call(
        kern,
        out_shape=(jax.ShapeDtypeStruct((t, d), F32),
                   jax.ShapeDtypeStruct((t * tile_rows, LANES), jnp.uint32),
                   jax.ShapeDtypeStruct((ROUTER_ROWS, t), F32)),
        grid=(t // tm,),
        in_specs=[
            pl.BlockSpec((tm, aw), lambda i: (i, 0)),
            pl.BlockSpec((tm, mw), lambda i: (i, 0)),
            pl.BlockSpec((aw + mw, d), lambda i: (0, 0)),
            pl.BlockSpec((tm, d), lambda i: (i, 0)),
            pl.BlockSpec((None, 6, d), lambda i: (i // blocks_per_seq, 0, 0)),
            pl.BlockSpec((1, d), lambda i: (0, 0)),
            pl.BlockSpec((ROUTER_ROWS, d), lambda i: (0, 0)),
            pl.BlockSpec((ROUTER_ROWS, d), lambda i: (0, 0)),
            pl.BlockSpec((ROUTER_ROWS, 1), lambda i: (0, 0)),
        ],
        out_specs=(pl.BlockSpec((tm, d), lambda i: (i, 0)),
                   pl.BlockSpec((tm * tile_rows, LANES), lambda i: (i, 0)),
                   pl.BlockSpec((ROUTER_ROWS, tm), lambda i: (0, i))),
        compiler_params=_cparams(("arbitrary",), 48),
        name="out_proj",
    )(attn2d, hm2d, w_out, x2d, mod, norm_w, wr_hi, wr_lo, br)


def _route_kernel(lg_ref, idx_ref, w_ref):
    ng, ne = MOE_GROUPS, MOE_EXPERTS_PER_GROUP
    lg = [lg_ref[g:g + 1, :] for g in range(ng)]
    best, gsel = lg[0], jnp.zeros_like(lg[0], dtype=jnp.int32)
    for g in range(1, ng):
        better = lg[g] > best
        best = jnp.where(better, lg[g], best)
        gsel = jnp.where(better, g, gsel)
    denom = jnp.exp(lg[0] - best)
    for g in range(1, ng):
        denom = denom + jnp.exp(lg[g] - best)
    p_group = 1.0 / denom

    le = []
    for e in range(ne):
        v = lg_ref[ng + e:ng + e + 1, :]
        for g in range(1, ng):
            v = jnp.where(gsel == g, lg_ref[ng + g * ne + e:ng + g * ne + e + 1, :], v)
        le.append(v)
    v1, i1 = le[0], jnp.zeros_like(gsel)
    for e in range(1, ne):
        better = le[e] > v1
        v1 = jnp.where(better, le[e], v1)
        i1 = jnp.where(better, e, i1)
    v2 = jnp.full_like(v1, -jnp.inf)
    i2 = jnp.zeros_like(gsel)
    for e in range(ne):
        better = jnp.logical_and(le[e] > v2, i1 != e)
        v2 = jnp.where(better, le[e], v2)
        i2 = jnp.where(better, e, i2)
    e2 = jnp.exp(v2 - v1)
    inv = 1.0 / (1.0 + e2)
    tn = gsel.shape[1]
    zero_i = jnp.zeros((SUBLANES - MOE_TOP_K, tn), jnp.int32)
    idx_ref[...] = jnp.concatenate([gsel * ne + i1, gsel * ne + i2, zero_i], axis=0)
    w_rows = jnp.concatenate([p_group * inv, p_group * (e2 * inv), jnp.zeros((LANES - MOE_TOP_K, tn), F32)], axis=0)
    for c in range(tn // LANES):
        w_ref[c * LANES:(c + 1) * LANES, :] = w_rows[:, c * LANES:(c + 1) * LANES].T


def _route(logits_t):
    rows, t = logits_t.shape
    tn = 2048
    return pl.pallas_call(
        _route_kernel,
        out_shape=(jax.ShapeDtypeStruct((SUBLANES, t), jnp.int32),
                   jax.ShapeDtypeStruct((t, LANES), F32)),
        grid=(t // tn,),
        in_specs=[pl.BlockSpec((rows, tn), lambda i: (0, i))],
        out_specs=(pl.BlockSpec((SUBLANES, tn), lambda i: (0, i)),
                   pl.BlockSpec((tn, LANES), lambda i: (i, 0))),
        compiler_params=_cparams(("arbitrary",), 32),
        name="route",
    )(logits_t)


def _dispatch_tables(eidx, t):
    m = MOE_ROWS
    a_total = MOE_TOP_K * t
    n_blocks = a_total // m + N_EXPERTS
    n_rows = n_blocks * m
    src_bits = (a_total - 1).bit_length()
    experts = jnp.arange(N_EXPERTS, dtype=jnp.int32)
    e_flat = eidx[:MOE_TOP_K].reshape(a_total)
    counts = jnp.sum((e_flat[:, None] == experts[None, :]).astype(jnp.int32), axis=0)
    padded = (counts + m - 1) // m * m
    pad_end = jnp.cumsum(padded)
    pad_start = pad_end - padded
    start = jnp.cumsum(counts) - counts
    n_used = (pad_end[-1] // m).astype(jnp.int32)
    packed = lax.sort(e_flat * (1 << src_bits) + jnp.arange(a_total, dtype=jnp.int32))
    order = packed & ((1 << src_bits) - 1)

    def lookup(idx, table):
        return jnp.sum(jnp.where(idx[:, None] == experts[None, :], table[None, :], 0), axis=1)

    blk = jnp.arange(n_blocks, dtype=jnp.int32)
    block_e = jnp.minimum(jnp.sum((blk[:, None] * m >= pad_end[None, :]).astype(jnp.int32), axis=1), N_EXPERTS - 1)
    block_off = blk * m - lookup(block_e, pad_start)
    block_valid = jnp.clip(lookup(block_e, counts) - block_off, 0, m)
    block_valid = jnp.where(blk < n_used, block_valid, 0)
    sorted_pos = (lookup(block_e, start) + block_off)[:, None] + jnp.arange(m, dtype=jnp.int32)[None, :]
    valid = jnp.arange(m, dtype=jnp.int32)[None, :] < block_valid[:, None]
    spare = a_total + jnp.arange(m, dtype=jnp.int32)[None, :]
    row_dst = jnp.where(valid, order[jnp.clip(sorted_pos, 0, a_total - 1)], spare)
    row_dst = jnp.concatenate([spare, row_dst, jnp.broadcast_to(spare, (MOE_TABLE_TAIL, m))], axis=0)
    block_e = jnp.where(blk < n_used, block_e, jnp.max(jnp.where(blk < n_used, block_e, 0)))
    return block_e, n_used.reshape(1), row_dst


def _moe_kernel(be_ref, nu_ref, idx_hbm, h_hbm, wg_ref, wu_ref, wd_ref, out_hbm,
                idx_s, xbuf, ybuf, sem_i, sem_g, sem_s, *, tokens, ff_chunk):
    m = MOE_ROWS
    i = pl.program_id(0)
    n_used = nu_ref[0]
    par = lax.rem(i, 2)
    other = 1 - par
    ff = wg_ref.shape[2]
    half = wg_ref.shape[1] // 2
    tr = half // LANES
    slot_rows = m * tr
    n_chunks = ff // ff_chunk
    per_chunk = m // n_chunks

    def tile_rows_at(row):
        return pl.ds(pl.multiple_of(row * tr, tr), tr)

    def idx_copy(table_row):
        s = jnp.bitwise_and(table_row, MOE_IDX_SLOTS - 1)
        return pltpu.make_async_copy(idx_hbm.at[table_row], idx_s.at[s], sem_i.at[s])

    def gather_row(buf_half, table_row, j):
        tok = jnp.bitwise_and(idx_s[jnp.bitwise_and(table_row, MOE_IDX_SLOTS - 1), j], tokens - 1)
        pltpu.make_async_copy(h_hbm.at[tile_rows_at(tok)], xbuf.at[tile_rows_at(buf_half * m + j)],
                              sem_g.at[buf_half]).start()

    def scatter_row(buf_half, table_row, j):
        dst = idx_s[jnp.bitwise_and(table_row, MOE_IDX_SLOTS - 1), j]
        pltpu.make_async_copy(ybuf.at[tile_rows_at(buf_half * m + j)], out_hbm.at[tile_rows_at(dst)],
                              sem_s.at[buf_half]).start()

    def half_rows(buf_half):
        return pl.ds(pl.multiple_of(buf_half * slot_rows, slot_rows), slot_rows)

    def wait_gather(buf_half):
        pltpu.make_async_copy(h_hbm.at[pl.ds(0, slot_rows)], xbuf.at[half_rows(buf_half)], sem_g.at[buf_half]).wait()

    def wait_scatter(buf_half):
        pltpu.make_async_copy(ybuf.at[half_rows(buf_half)], out_hbm.at[pl.ds(0, slot_rows)], sem_s.at[buf_half]).wait()

    @pl.when(i == 0)
    def _():
        ybuf[...] = jnp.zeros_like(ybuf)
        for row in range(3):
            idx_copy(row).start()
        for row in range(3):
            idx_copy(row).wait()
        idx_copy(3).start()

        def body(j, carry):
            gather_row(0, 1, j)
            return carry
        lax.fori_loop(0, m, body, 0, unroll=8)

    @pl.when(jnp.logical_and(i < n_used, i >= 1))
    def _():
        idx_copy(i + 2).wait()
        idx_copy(i + 3).start()
        wait_scatter(par)

    @pl.when(i < n_used)
    def _():
        wait_gather(par)
        base = pl.multiple_of(par * slot_rows, slot_rows)
        x_lo, x_hi = _load_token_tiles(xbuf, base, m, tr)
        x_lo, x_hi = x_lo.astype(BF16), x_hi.astype(BF16)
        acc = jnp.zeros((m, wd_ref.shape[2]), F32)
        n_dots = 5 * n_chunks
        issued = 0

        def issue_rows(dot_index):
            nonlocal issued
            upto = (m * (dot_index + 1)) // n_dots
            for j in range(issued, upto):
                gather_row(other, i + 2, j)
                scatter_row(other, i, j)
            issued = upto

        def dot_with_rows(dot_index, a, b):
            issue_rows(dot_index)
            return _dot(a, b)

        for c in range(n_chunks):
            cs = slice(c * ff_chunk, (c + 1) * ff_chunk)
            k = 5 * c
            g = dot_with_rows(k, x_lo, wg_ref[0, 0:half, cs]) + dot_with_rows(k + 1, x_hi, wg_ref[0, half:, cs])
            u = dot_with_rows(k + 2, x_lo, wu_ref[0, 0:half, cs]) + dot_with_rows(k + 3, x_hi, wu_ref[0, half:, cs])
            act = (g * jax.nn.sigmoid(g) * u).astype(BF16)
            acc = acc + dot_with_rows(k + 4, act, wd_ref[0, cs, :])
        _store_token_tiles(ybuf, base, acc)

        @pl.when(i == n_used - 1)
        def _():
            wait_scatter(other)

            def body(j, carry):
                scatter_row(par, i + 1, j)
                return carry
            lax.fori_loop(0, m, body, 0, unroll=8)
            wait_scatter(par)
            wait_gather(other)
            idx_copy(i + 3).wait()


def _moe(block_e, n_used, row_dst, h_packed, wg, wu, wd, layer):
    d, ff = wg.shape[2], wg.shape[3]
    tr = d // 2 // LANES
    t = h_packed.shape[0] // tr
    m = MOE_ROWS
    n_blocks = block_e.shape[0]
    kern = functools.partial(_moe_kernel, tokens=t, ff_chunk=256)
    grid_spec = pltpu.PrefetchScalarGridSpec(
        num_scalar_prefetch=2,
        grid=(n_blocks,),
        in_specs=[
            pl.BlockSpec(memory_space=pl.ANY),
            pl.BlockSpec(memory_space=pl.ANY),
            pl.BlockSpec((None, 1, d, ff), lambda i, be, nu: (layer, be[i], 0, 0)),
            pl.BlockSpec((None, 1, d, ff), lambda i, be, nu: (layer, be[i], 0, 0)),
            pl.BlockSpec((None, 1, ff, d), lambda i, be, nu: (layer, be[i], 0, 0)),
        ],
        out_specs=pl.BlockSpec(memory_space=pl.ANY),
        scratch_shapes=[
            pltpu.SMEM((MOE_IDX_SLOTS, m), jnp.int32),
            pltpu.VMEM((2 * m * tr, LANES), jnp.uint32),
            pltpu.VMEM((2 * m * tr, LANES), jnp.uint32),
            pltpu.SemaphoreType.DMA((MOE_IDX_SLOTS,)),
            pltpu.SemaphoreType.DMA((2,)),
            pltpu.SemaphoreType.DMA((2,)),
        ],
    )
    return pl.pallas_call(
        kern,
        out_shape=jax.ShapeDtypeStruct(((MOE_TOP_K * t + m) * tr, LANES), jnp.uint32),
        grid_spec=grid_spec,
        compiler_params=_cparams(("arbitrary",), 56),
        name="moe_experts",
    )(block_e, n_used, row_dst, h_packed, wg, wu, wd)


def _combine_kernel(x_ref, y0_ref, y1_ref, w_ref, mod_ref, o_ref):
    tm, d = x_ref.shape
    half = d // 2
    tr = half // LANES
    w0 = w_ref[:, 0:1]
    w1 = w_ref[:, 1:2]
    lo0, hi0 = _load_token_tiles(y0_ref, 0, tm, tr)
    lo1, hi1 = _load_token_tiles(y1_ref, 0, tm, tr)
    o_ref[:, 0:half] = x_ref[:, 0:half] + mod_ref[5:6, 0:half] * (w0 * lo0 + w1 * lo1)
    o_ref[:, half:] = x_ref[:, half:] + mod_ref[5:6, half:] * (w0 * hi0 + w1 * hi1)


def _combine(x2d, y, w_tok, mod, seq):
    t, d = x2d.shape
    tm = 512
    tr = d // 2 // LANES
    blocks_per_seq = seq // tm
    nblk = t // tm
    return pl.pallas_call(
        _combine_kernel,
        out_shape=jax.ShapeDtypeStruct((t, d), F32),
        grid=(nblk,),
        in_specs=[
            pl.BlockSpec((tm, d), lambda i: (i, 0)),
            pl.BlockSpec((tm * tr, LANES), lambda i: (i, 0)),
            pl.BlockSpec((tm * tr, LANES), lambda i: (nblk + i, 0)),
            pl.BlockSpec((tm, LANES), lambda i: (i, 0)),
            pl.BlockSpec((None, 6, d), lambda i: (i // blocks_per_seq, 0, 0)),
        ],
        out_specs=pl.BlockSpec((tm, d), lambda i: (i, 0)),
        compiler_params=_cparams(("arbitrary",), 48),
        name="combine",
    )(x2d, y, y, w_tok, mod)


def kernel(x, c, rel_bias, w_mod, b_mod, norm1_w, norm2_w, w_in, q_norm_w, k_norm_w, conv_w, gate_b, h_norm_w, w_out, router_group_w, router_group_b, router_expert_w, router_expert_b, w_gate, w_up, w_down):
    batch, seq, d = x.shape
    depth = w_mod.shape[0]
    t = batch * seq
    attn_width = d // 2
    mlstm_width = d - attn_width
    dv = mlstm_width // MLSTM_HEADS
    dk = dv // 2
    n_main = 3 * attn_width + 2 * MLSTM_HEADS * dk + 2 * mlstm_width
    assert dk == LANES and attn_width % LANES == 0 and seq % 1024 == 0 and t % MOE_ROWS == 0
    assert t & (t - 1) == 0, "the MoE row tables recover the token of assignment k * t + token with a bit mask"
    assert all((seq // dil) % ATTN_BLOCK == 0 and window // dil == ATTN_BLOCK for window, dil in DILATED_PATTERNS)

    mod_all = _modulation(c, w_mod, b_mod)
    bias_tab = _attn_bias_tables(rel_bias)
    x2d = x.reshape(t, d)
    wg_all, wu_all, wd_all = w_gate.astype(BF16), w_up.astype(BF16), w_down.astype(BF16)

    for l in range(depth):
        mod = mod_all[l]
        w_main = w_in[l, :, :n_main].astype(BF16)
        w_gates = jnp.zeros((d, LANES), F32).at[:, :2 * MLSTM_HEADS].set(w_in[l, :, n_main:])
        wg_hi, wg_lo = _split_bf16(w_gates)
        proj, gates = _in_proj(x2d, mod, norm1_w[l].reshape(1, d), w_main, wg_hi, wg_lo, seq)

        qw2 = jnp.tile(q_norm_w[l], LANES // ATTN_HEAD_DIM).reshape(1, LANES)
        kw2 = jnp.tile(k_norm_w[l], LANES // ATTN_HEAD_DIM).reshape(1, LANES)
        attn = _attention(proj, qw2, kw2, bias_tab, batch, seq, attn_width)

        gate_b_row = jnp.zeros((1, LANES), F32).at[0, :2 * MLSTM_HEADS].set(gate_b[l])
        hm = _mlstm(proj, gates, conv_w[l], gate_b_row, h_norm_w[l].reshape(1, mlstm_width),
                    batch, seq, attn_width, dk, dv)

        w_r = jnp.concatenate([router_group_w[l], router_expert_w[l]], axis=1).T
        w_r = jnp.zeros((ROUTER_ROWS, d), F32).at[:w_r.shape[0]].set(w_r)
        wr_hi, wr_lo = _split_bf16(w_r)
        b_r = jnp.concatenate([router_group_b[l], router_expert_b[l]])
        b_r = jnp.zeros((ROUTER_ROWS, 1), F32).at[:b_r.shape[0], 0].set(b_r)
        x2d, h_packed, logits_t = _out_proj(
            attn.reshape(t, attn_width), hm.reshape(t, mlstm_width), w_out[l].astype(BF16),
            x2d, mod, norm2_w[l].reshape(1, d), wr_hi, wr_lo, b_r, seq)

        eidx, w_tok = _route(logits_t)
        block_e, n_used, row_dst = _dispatch_tables(eidx, t)
        y = _moe(block_e, n_used, row_dst, h_packed, wg_all, wu_all, wd_all, l)
        x2d = _combine(x2d, y, w_tok, mod, seq)

    return x2d.reshape(batch, seq, d)
```

```python
import functools

import jax
import jax.numpy as jnp
import numpy as np
from jax import lax
from jax.experimental import pallas as pl
from jax.experimental.pallas import tpu as pltpu

F32 = jnp.float32
BF16 = jnp.bfloat16

NORM_EPS = 1e-6
ATTN_HEAD_DIM = 64
ATTN_BLOCK = 128
ATTN_MAJOR = 16
LOG2E = 1.4426950408889634
DILATED_PATTERNS = ((128, 1), (512, 4), (2048, 16))
NUM_BUCKETS = 32
REL_MAX_DIST = 2048
MLSTM_HEADS = 4
MLSTM_CHUNK = 128
CONV_WIDTH = 4
MOE_GROUPS = 4
MOE_EXPERTS_PER_GROUP = 8
N_EXPERTS = MOE_GROUPS * MOE_EXPERTS_PER_GROUP
MOE_TOP_K = 2
LANES = 128
SUBLANES = 8
ROUTER_ROWS = 40
MOE_ROWS = 256
MOE_IDX_SLOTS = 4
MOE_TABLE_TAIL = 2
MASK_VALUE = -1e30
MiB = 1024 * 1024


def _cparams(sem, vmem_mib):
    return pltpu.CompilerParams(dimension_semantics=sem, vmem_limit_bytes=vmem_mib * MiB)


def _split_bf16(a):
    hi = a.astype(BF16)
    lo = (a - hi.astype(F32)).astype(BF16)
    return hi, lo


def _dot(a, b):
    return jnp.dot(a, b, preferred_element_type=F32)


def _dot_nt(a, b):
    return lax.dot_general(a, b, (((1,), (1,)), ((), ())), preferred_element_type=F32)


def _dot_tn(a, b):
    return lax.dot_general(a, b, (((0,), (0,)), ((), ())), preferred_element_type=F32)


def _dot3(a_f32, w_hi, w_lo):
    a_hi, a_lo = _split_bf16(a_f32)
    return _dot(a_hi, w_hi) + _dot(a_lo, w_hi) + _dot(a_hi, w_lo)


def _rms_modulate(x, norm_w, scale, shift):
    ms = jnp.mean(x * x, axis=-1, keepdims=True)
    return x * lax.rsqrt(ms + NORM_EPS) * norm_w * (1.0 + scale) + shift


def _mod_kernel(c_ref, w_ref, b_ref, o_ref):
    c = c_ref[...]
    a = c * jax.nn.sigmoid(c)
    w_hi, w_lo = _split_bf16(w_ref[...])
    o_ref[...] = _dot3(a, w_hi, w_lo) + b_ref[...]


def _modulation(c, w_mod, b_mod):
    depth, d, n = w_mod.shape
    b = c.shape[0]
    bp = -(-b // SUBLANES) * SUBLANES
    cp = jnp.zeros((bp, d), F32).at[:b].set(c)
    tn = 1024
    out = pl.pallas_call(
        _mod_kernel,
        out_shape=jax.ShapeDtypeStruct((depth, bp, n), F32),
        grid=(depth, n // tn),
        in_specs=[
            pl.BlockSpec((bp, d), lambda l, j: (0, 0)),
            pl.BlockSpec((None, d, tn), lambda l, j: (l, 0, j)),
            pl.BlockSpec((None, 1, tn), lambda l, j: (l, 0, j)),
        ],
        out_specs=pl.BlockSpec((None, bp, tn), lambda l, j: (l, 0, j)),
        compiler_params=_cparams(("arbitrary", "arbitrary"), 40),
        name="modulation",
    )(cp, w_mod, b_mod.reshape(depth, 1, n))
    return out[:, :b].reshape(depth, b, 6, d)


def _in_proj_kernel(x_ref, mod_ref, nw_ref, w_ref, wgh_ref, wgl_ref, o_ref, g_ref, h_scr):
    @pl.when(pl.program_id(1) == 0)
    def _():
        h = _rms_modulate(x_ref[...], nw_ref[...], mod_ref[1:2, :], mod_ref[0:1, :])
        h_scr[...] = h.astype(BF16)
        g_ref[...] = _dot3(h, wgh_ref[...], wgl_ref[...])

    o_ref[...] = _dot(h_scr[...], w_ref[...]).astype(BF16)


def _in_proj(x2d, mod, norm_w, w_main, wg_hi, wg_lo, seq):
    t, d = x2d.shape
    n = w_main.shape[1]
    tm, tn = 1024, 512
    blocks_per_seq = seq // tm
    return pl.pallas_call(
        _in_proj_kernel,
        out_shape=(jax.ShapeDtypeStruct((t, n), BF16), jax.ShapeDtypeStruct((t, LANES), F32)),
        grid=(t // tm, n // tn),
        in_specs=[
            pl.BlockSpec((tm, d), lambda i, j: (i, 0)),
            pl.BlockSpec((None, 6, d), lambda i, j: (i // blocks_per_seq, 0, 0)),
            pl.BlockSpec((1, d), lambda i, j: (0, 0)),
            pl.BlockSpec((d, tn), lambda i, j: (0, j)),
            pl.BlockSpec((d, LANES), lambda i, j: (0, 0)),
            pl.BlockSpec((d, LANES), lambda i, j: (0, 0)),
        ],
        out_specs=(pl.BlockSpec((tm, tn), lambda i, j: (i, j)),
                   pl.BlockSpec((tm, LANES), lambda i, j: (i, 0))),
        scratch_shapes=[pltpu.VMEM((tm, d), BF16)],
        compiler_params=_cparams(("arbitrary", "arbitrary"), 48),
        name="in_proj",
    )(x2d, mod, norm_w, w_main, wg_hi, wg_lo)


def _t5_causal_bucket(dist):
    max_exact = NUM_BUCKETS // 2
    d = np.maximum(dist, 1).astype(np.float32)
    large = max_exact + (np.log(d / max_exact) / np.log(REL_MAX_DIST / max_exact)
                         * (NUM_BUCKETS - max_exact)).astype(np.int32)
    large = np.minimum(large, NUM_BUCKETS - 1)
    return np.where(dist < max_exact, dist, large).astype(np.int32)


def _block_positions(dilation):
    g = ATTN_MAJOR // dilation
    plen = ATTN_BLOCK // g
    i = np.arange(ATTN_BLOCK)
    return g * (i % plen) + i // plen


def _attn_bias_tables(rel_bias):
    blk = ATTN_BLOCK
    tables = []
    for window, dilation in DILATED_PATTERNS:
        win = window // dilation
        loc = _block_positions(dilation)
        kpos = np.concatenate([loc, loc + blk])
        rel = loc[:, None] - kpos[None, :] + blk
        valid = (rel >= 0) & (rel <= win)
        bucket = _t5_causal_bucket(np.clip(rel, 0, win) * dilation)
        onehot = (jnp.asarray(bucket)[:, :, None] == jnp.arange(NUM_BUCKETS, dtype=jnp.int32)).astype(F32)
        bias = jnp.einsum('icb,bh->hic', onehot, rel_bias.astype(F32), precision=lax.Precision.HIGHEST) * LOG2E
        normal = jnp.where(valid[None], bias, MASK_VALUE)
        first = jnp.where((valid & (kpos >= blk)[None, :])[None], bias, MASK_VALUE)
        tables.append(jnp.stack([normal, first], axis=0))
    return jnp.stack(tables, axis=0)


def _attn_kernel(q_ref, k_ref, v_ref, qw_ref, kw_ref, bias_ref, o_ref,
                 stage, qs, ks, vs, acc_s, m_s, l_s, *, seq, unroll):
    blk = ATTN_BLOCK
    major = ATTN_MAJOR
    l16 = seq // major
    lane = lax.broadcasted_iota(jnp.int32, (1, LANES), 1)
    head0 = lane < ATTN_HEAD_DIM
    gi = lax.broadcasted_iota(jnp.int32, (LANES, LANES), 0) // ATTN_HEAD_DIM
    gj = lax.broadcasted_iota(jnp.int32, (LANES, LANES), 1) // ATTN_HEAD_DIM
    group_ones = (gi == gj).astype(BF16)
    ones_cols = jnp.ones((2 * blk, LANES), BF16)
    rows = 256

    def to_major(dst):
        def body(r, carry):
            for h in range(l16 // blk):
                dst[pl.ds(pl.multiple_of(r * l16 + h * blk, blk), blk), :] = (
                    stage[pl.ds(r + major * blk * h, blk, stride=major), :])
            return carry
        lax.fori_loop(0, major, body, 0)

    for src, dst, w_ref, scale in ((q_ref, qs, qw_ref, ATTN_HEAD_DIM ** -0.5 * LOG2E),
                                   (k_ref, ks, kw_ref, 1.0), (v_ref, vs, None, None)):
        def norm_body(c, carry, src=src, w_ref=w_ref, scale=scale):
            sl = pl.ds(pl.multiple_of(c * rows, rows), rows)
            xx = src[sl, :].astype(F32)
            if w_ref is not None:
                hi, lo = _split_bf16(xx * xx)
                ssq = _dot(hi, group_ones) + _dot(lo, group_ones)
                xx = xx * lax.rsqrt(ssq * (1.0 / ATTN_HEAD_DIM) + NORM_EPS) * w_ref[...] * scale
            stage[sl, :] = xx
            return carry
        lax.fori_loop(0, seq // rows, norm_body, 0, unroll=4)
        to_major(dst)

    for p, (_, dil) in enumerate(DILATED_PATTERNS):
        groups = major // dil
        plen = blk // groups
        nb_log2 = ((seq // dil) // blk).bit_length() - 1

        def starts(r_d, n, dil=dil, groups=groups, plen=plen):
            return [pl.multiple_of((j * dil + r_d) * l16 + n * plen, SUBLANES) for j in range(groups)]

        def load_blk(ref, st, plen=plen):
            return jnp.concatenate([ref[pl.ds(s, plen), :] for s in st], axis=0)

        def store_blk(ref, st, val, plen=plen):
            for j, s in enumerate(st):
                ref[pl.ds(s, plen), :] = val[j * plen:(j + 1) * plen]

        def body(it, carry, p=p, nb_log2=nb_log2, starts=starts, load_blk=load_blk, store_blk=store_blk):
            loaded = []
            for u in range(unroll):
                idx = it * unroll + u
                r_d = lax.shift_right_logical(idx, nb_log2)
                n = idx - lax.shift_left(r_d, nb_log2)
                cur = starts(r_d, n)
                prv = starts(r_d, jnp.maximum(n - 1, 0))
                first = (n == 0).astype(jnp.int32)
                qb = load_blk(qs, cur).astype(BF16)
                kw = jnp.concatenate([load_blk(ks, prv), load_blk(ks, cur)], axis=0).astype(BF16)
                vw = jnp.concatenate([load_blk(vs, prv), load_blk(vs, cur)], axis=0).astype(BF16)
                old = (load_blk(m_s, cur), load_blk(l_s, cur), load_blk(acc_s, cur)) if p > 0 else None
                loaded.append((cur, first, qb, kw, vw, old))
            results = []
            for cur, first, qb, kw, vw, old in loaded:
                vw_aug = jnp.concatenate([vw, ones_cols], axis=1)
                ms, res = [], []
                for h, hmask in enumerate((head0, jnp.logical_not(head0))):
                    qh = jnp.where(hmask, qb, jnp.zeros_like(qb))
                    s = _dot_nt(qh, kw) + bias_ref[p, first, h]
                    mh = jnp.max(s, axis=-1, keepdims=True)
                    ms.append(mh)
                    res.append(_dot(jnp.exp2(s - mh).astype(BF16), vw_aug))
                m_b = jnp.where(head0, ms[0], ms[1])
                a_b = jnp.where(head0, res[0][:, :LANES], res[1][:, :LANES])
                l_b = jnp.where(head0, res[0][:, LANES:], res[1][:, LANES:])
                if old is None:
                    results.append((cur, m_b, l_b, a_b))
                else:
                    m_o, l_o, a_o = old
                    m_new = jnp.maximum(m_o, m_b)
                    w_o = jnp.exp2(m_o - m_new)
                    w_b = jnp.exp2(m_b - m_new)
                    results.append((cur, m_new, w_o * l_o + w_b * l_b, w_o * a_o + w_b * a_b))
            for cur, m_new, l_new, a_new in results:
                store_blk(m_s, cur, m_new)
                store_blk(l_s, cur, l_new)
                store_blk(acc_s, cur, a_new)
            return carry

        lax.fori_loop(0, (seq // blk) // unroll, body, 0)

    def from_major(r, carry):
        for h in range(l16 // blk):
            sl = pl.ds(pl.multiple_of(r * l16 + h * blk, blk), blk)
            stage[pl.ds(r + major * blk * h, blk, stride=major), :] = acc_s[sl, :] / l_s[sl, :]
        return carry

    lax.fori_loop(0, major, from_major, 0)

    def out_body(c, carry):
        sl = pl.ds(pl.multiple_of(c * rows, rows), rows)
        o_ref[sl, :] = stage[sl, :].astype(o_ref.dtype)
        return carry

    lax.fori_loop(0, seq // rows, out_body, 0)


def _attention(proj, qw2, kw2, bias_tab, batch, seq, attn_width):
    t, n = proj.shape
    proj3 = proj.reshape(batch, seq, n)
    pairs = attn_width // LANES
    npat = len(DILATED_PATTERNS)
    kern = functools.partial(_attn_kernel, seq=seq, unroll=8)
    return pl.pallas_call(
        kern,
        out_shape=jax.ShapeDtypeStruct((batch, seq, attn_width), BF16),
        grid=(batch, pairs),
        in_specs=[
            pl.BlockSpec((None, seq, LANES), lambda b, h: (b, 0, h)),
            pl.BlockSpec((None, seq, LANES), lambda b, h: (b, 0, pairs + h)),
            pl.BlockSpec((None, seq, LANES), lambda b, h: (b, 0, 2 * pairs + h)),
            pl.BlockSpec((1, LANES), lambda b, h: (0, 0)),
            pl.BlockSpec((1, LANES), lambda b, h: (0, 0)),
            pl.BlockSpec((npat, 2, 2, ATTN_BLOCK, 2 * ATTN_BLOCK), lambda b, h: (0, 0, h, 0, 0)),
        ],
        out_specs=pl.BlockSpec((None, seq, LANES), lambda b, h: (b, 0, h)),
        scratch_shapes=[pltpu.VMEM((seq, LANES), F32) for _ in range(7)],
        compiler_params=_cparams(("arbitrary", "arbitrary"), 40),
        name="dilated_attention",
    )(proj3, proj3, proj3, qw2, kw2, bias_tab)


def _mlstm_kernel(q_ref, k_ref, v_ref, og_ref, g_ref, cw_ref, gb_ref, hw_ref, o_ref,
                  x_scr, c_scr, m_scr, *, rows, dk, dv):
    ch = MLSTM_CHUNK
    nheads = MLSTM_HEADS
    qk_w = nheads * dk
    aug = dv + LANES

    @pl.when(pl.program_id(1) == 0)
    def _():
        x_scr[0:SUBLANES, :] = jnp.zeros((SUBLANES, 2 * qk_w), F32)
        c_scr[...] = jnp.zeros_like(c_scr)
        m_scr[...] = jnp.zeros_like(m_scr)

    x_scr[SUBLANES:, 0:qk_w] = q_ref[...].astype(F32)
    x_scr[SUBLANES:, qk_w:] = k_ref[...].astype(F32)

    ri = lax.broadcasted_iota(jnp.int32, (ch, ch), 0)
    ci = lax.broadcasted_iota(jnp.int32, (ch, ch), 1)
    causal = ci <= ri
    tril = causal.astype(BF16)
    row8 = lax.broadcasted_iota(jnp.int32, (SUBLANES, 1), 0)
    ones_col = (lax.broadcasted_iota(jnp.int32, (ch, LANES), 1) == 0).astype(F32)
    k_scale = dk ** -0.5

    def chunk_body(c, carry):
        r0 = pl.multiple_of(c * ch, ch)
        prev8 = x_scr[pl.ds(r0, SUBLANES), :]
        cur = x_scr[pl.ds(r0 + SUBLANES, ch), :]
        conv = cur * cw_ref[CONV_WIDTH - 1:CONV_WIDTH, :]
        for sh in range(1, CONV_WIDTH):
            rolled = pltpu.roll(cur, sh, 0)
            head_rows = jnp.where(row8 < sh, pltpu.roll(prev8, sh, 0), rolled[0:SUBLANES])
            shifted = jnp.concatenate([head_rows, rolled[SUBLANES:]], axis=0)
            conv = conv + shifted * cw_ref[CONV_WIDTH - 1 - sh:CONV_WIDTH - sh, :]
        qk = conv * jax.nn.sigmoid(conv)

        gates = g_ref[pl.ds(r0, ch), :] + gb_ref[...]
        logf = jax.nn.log_sigmoid(gates)
        lf_hi, lf_lo = _split_bf16(logf)
        bcum = _dot(tril, lf_hi) + _dot(tril, lf_lo)
        gates_t = gates.T
        bcum_t = bcum.T

        for h in range(nheads):
            q = qk[:, h * dk:(h + 1) * dk]
            k = qk[:, qk_w + h * dk:qk_w + (h + 1) * dk] * k_scale
            v = v_ref[pl.ds(r0, ch), h * dv:(h + 1) * dv].astype(F32)
            v_aug = jnp.concatenate([v, ones_col], axis=1)
            i_col = gates[:, h:h + 1]
            b_col = bcum[:, nheads + h:nheads + h + 1]
            i_row = gates_t[h:h + 1, :]
            b_row = bcum_t[nheads + h:nheads + h + 1, :]
            m_prev = m_scr[h, 0:1, 0:1]
            c_prev = c_scr[h]

            dlog = jnp.where(causal, b_col - b_row + i_row, -jnp.inf)
            inter = b_col + m_prev
            m_t = jnp.maximum(inter, jnp.max(dlog, axis=-1, keepdims=True))
            qb = q.astype(BF16)
            kb = k.astype(BF16)
            sm = _dot_nt(qb, kb) * jnp.exp(dlog - m_t)
            e_inter = jnp.exp(inter - m_t)
            tot = e_inter * _dot(qb, c_prev.astype(BF16)) + _dot(sm.astype(BF16), v_aug.astype(BF16))
            num = tot[:, 0:dv]
            den = tot[:, dv:dv + 1]
            hcell = num / jnp.maximum(jnp.abs(den), jnp.exp(-m_t))

            g_last = b_col[ch - 1:ch, :]
            a_col = g_last - b_col + i_col
            m_new = jnp.maximum(g_last + m_prev, jnp.max(a_col, axis=0, keepdims=True))
            decay = jnp.exp(g_last + m_prev - m_new)
            w_col = jnp.exp(a_col - m_new)
            c_scr[h] = decay * c_prev + _dot_tn(kb, (w_col * v_aug).astype(BF16))
            m_scr[h] = jnp.broadcast_to(m_new, (SUBLANES, LANES))

            ms = jnp.mean(hcell * hcell, axis=-1, keepdims=True)
            hn = hcell * lax.rsqrt(ms + NORM_EPS) * hw_ref[:, h * dv:(h + 1) * dv]
            og = og_ref[pl.ds(r0, ch), h * dv:(h + 1) * dv].astype(F32)
            o_ref[pl.ds(r0, ch), h * dv:(h + 1) * dv] = (jax.nn.sigmoid(og) * hn).astype(o_ref.dtype)
        return carry

    lax.fori_loop(0, rows // ch, chunk_body, 0)
    x_scr[0:SUBLANES, :] = x_scr[rows:rows + SUBLANES, :]


def _mlstm(proj, gates, conv_w, gate_b_row, h_norm_w, batch, seq, attn_width, dk, dv):
    t, n = proj.shape
    proj3 = proj.reshape(batch, seq, n)
    gates3 = gates.reshape(batch, seq, LANES)
    nheads = MLSTM_HEADS
    qk_w = nheads * dk
    v_w = nheads * dv
    rows = 512
    q_blk = (3 * attn_width) // qk_w
    v_blk = (3 * attn_width + 2 * qk_w) // v_w
    kern = functools.partial(_mlstm_kernel, rows=rows, dk=dk, dv=dv)
    return pl.pallas_call(
        kern,
        out_shape=jax.ShapeDtypeStruct((batch, seq, v_w), BF16),
        grid=(batch, seq // rows),
        in_specs=[
            pl.BlockSpec((None, rows, qk_w), lambda b, j: (b, j, q_blk)),
            pl.BlockSpec((None, rows, qk_w), lambda b, j: (b, j, q_blk + 1)),
            pl.BlockSpec((None, rows, v_w), lambda b, j: (b, j, v_blk)),
            pl.BlockSpec((None, rows, v_w), lambda b, j: (b, j, v_blk + 1)),
            pl.BlockSpec((None, rows, LANES), lambda b, j: (b, j, 0)),
            pl.BlockSpec((CONV_WIDTH, 2 * qk_w), lambda b, j: (0, 0)),
            pl.BlockSpec((1, LANES), lambda b, j: (0, 0)),
            pl.BlockSpec((1, v_w), lambda b, j: (0, 0)),
        ],
        out_specs=pl.BlockSpec((None, rows, v_w), lambda b, j: (b, j, 0)),
        scratch_shapes=[
            pltpu.VMEM((rows + SUBLANES, 2 * qk_w), F32),
            pltpu.VMEM((nheads, dk, dv + LANES), F32),
            pltpu.VMEM((nheads, SUBLANES, LANES), F32),
        ],
        compiler_params=_cparams(("arbitrary", "arbitrary"), 40),
        name="mlstm",
    )(proj3, proj3, proj3, proj3, gates3, conv_w, gate_b_row, h_norm_w)


def _store_token_tiles(ref, base, h):
    m, w2 = h.shape
    w = w2 // 2
    tile_rows = w // LANES
    u = pltpu.bitcast(h.astype(BF16).astype(F32), jnp.uint32)
    packed = (u[:, :w] >> 16) | (u[:, w:] & jnp.uint32(0xFFFF0000))
    for k in range(tile_rows):
        ref[pl.ds(base + k, m, stride=tile_rows), :] = packed[:, k * LANES:(k + 1) * LANES]


def _load_token_tiles(ref, base, m, tile_rows):
    lo, hi = [], []
    for k in range(tile_rows):
        p = ref[pl.ds(base + k, m, stride=tile_rows), :]
        lo.append(pltpu.bitcast(p << 16, F32))
        hi.append(pltpu.bitcast(p & jnp.uint32(0xFFFF0000), F32))
    return jnp.concatenate(lo, axis=1), jnp.concatenate(hi, axis=1)


def _out_proj_kernel(a_ref, m_ref, w_ref, x_ref, mod_ref, nw_ref, wrh_ref, wrl_ref, br_ref,
                     xo_ref, hp_ref, lg_ref, *, attn_width, sub_rows):
    tm, d = x_ref.shape
    tile_rows = d // 2 // LANES
    half = d // 2
    col = 4 * LANES
    for s in range(tm // sub_rows):
        rs = slice(s * sub_rows, (s + 1) * sub_rows)
        a_blk, m_blk = a_ref[rs, :], m_ref[rs, :]
        ssq = jnp.zeros((sub_rows, 1), F32)
        for c in range(d // col):
            cs = slice(c * col, (c + 1) * col)
            mix = _dot(a_blk, w_ref[0:attn_width, cs]) + _dot(m_blk, w_ref[attn_width:, cs])
            xn = x_ref[rs, cs] + mod_ref[2:3, cs] * mix
            xo_ref[rs, cs] = xn
            ssq = ssq + jnp.sum(xn * xn, axis=-1, keepdims=True)
        rstd = lax.rsqrt(ssq * (1.0 / d) + NORM_EPS)
        logits = br_ref[...]
        for k in range(tile_rows):
            parts = []
            for lo in (k * LANES, half + k * LANES):
                ks = slice(lo, lo + LANES)
                hk = xo_ref[rs, ks] * rstd * nw_ref[:, ks] * (1.0 + mod_ref[4:5, ks]) + mod_ref[3:4, ks]
                h_hi, h_lo = _split_bf16(hk)
                logits = logits + (_dot_nt(wrh_ref[:, ks], h_hi) + _dot_nt(wrh_ref[:, ks], h_lo)
                                   + _dot_nt(wrl_ref[:, ks], h_hi))
                parts.append(pltpu.bitcast(h_hi.astype(F32), jnp.uint32))
            packed = (parts[0] >> 16) | (parts[1] & jnp.uint32(0xFFFF0000))
            hp_ref[pl.ds(s * sub_rows * tile_rows + k, sub_rows, stride=tile_rows), :] = packed
        lg_ref[:, rs] = logits


def _out_proj(attn2d, hm2d, w_out, x2d, mod, norm_w, wr_hi, wr_lo, br, seq):
    t, d = x2d.shape
    aw = attn2d.shape[1]
    mw = hm2d.shape[1]
    tm = 512
    blocks_per_seq = seq // tm
    tile_rows = d // 2 // LANES
    kern = functools.partial(_out_proj_kernel, attn_width=aw, sub_rows=512)
    return pl.pallas_call(
        kern,
        out_shape=(jax.ShapeDtypeStruct((t, d), F32),
                   jax.ShapeDtypeStruct((t * tile_rows, LANES), jnp.uint32),
                   jax.ShapeDtypeStruct((ROUTER_ROWS, t), F32)),
        grid=(t // tm,),
        in_specs=[
            pl.BlockSpec((tm, aw), lambda i: (i, 0)),
            pl.BlockSpec((tm, mw), lambda i: (i, 0)),
            pl.BlockSpec((aw + mw, d), lambda i: (0, 0)),
            pl.BlockSpec((tm, d), lambda i: (i, 0)),
            pl.BlockSpec((None, 6, d), lambda i: (i // blocks_per_seq, 0, 0)),
            pl.BlockSpec((1, d), lambda i: (0, 0)),
            pl.BlockSpec((ROUTER_ROWS, d), lambda i: (0, 0)),
            pl.BlockSpec((ROUTER_ROWS, d), lambda i: (0, 0)),
            pl.BlockSpec((ROUTER_ROWS, 1), lambda i: (0, 0)),
        ],
        out_specs=(pl.BlockSpec((tm, d), lambda i: (i, 0)),
                   pl.BlockSpec((tm * tile_rows, LANES), lambda i: (i, 0)),
                   pl.BlockSpec((ROUTER_ROWS, tm), lambda i: (0, i))),
        compiler_params=_cparams(("arbitrary",), 48),
        name="out_proj",
    )(attn2d, hm2d, w_out, x2d, mod, norm_w, wr_hi, wr_lo, br)


def _route_kernel(lg_ref, idx_ref, w_ref):
    ng, ne = MOE_GROUPS, MOE_EXPERTS_PER_GROUP
    lg = [lg_ref[g:g + 1, :] for g in range(ng)]
    best, gsel = lg[0], jnp.zeros_like(lg[0], dtype=jnp.int32)
    for g in range(1, ng):
        better = lg[g] > best
        best = jnp.where(better, lg[g], best)
        gsel = jnp.where(better, g, gsel)
    denom = jnp.exp(lg[0] - best)
    for g in range(1, ng):
        denom = denom + jnp.exp(lg[g] - best)
    p_group = 1.0 / denom

    le = []
    for e in range(ne):
        v = lg_ref[ng + e:ng + e + 1, :]
        for g in range(1, ng):
            v = jnp.where(gsel == g, lg_ref[ng + g * ne + e:ng + g * ne + e + 1, :], v)
        le.append(v)
    v1, i1 = le[0], jnp.zeros_like(gsel)
    for e in range(1, ne):
        better = le[e] > v1
        v1 = jnp.where(better, le[e], v1)
        i1 = jnp.where(better, e, i1)
    v2 = jnp.full_like(v1, -jnp.inf)
    i2 = jnp.zeros_like(gsel)
    for e in range(ne):
        better = jnp.logical_and(le[e] > v2, i1 != e)
        v2 = jnp.where(better, le[e], v2)
        i2 = jnp.where(better, e, i2)
    e2 = jnp.exp(v2 - v1)
    inv = 1.0 / (1.0 + e2)
    tn = gsel.shape[1]
    zero_i = jnp.zeros((SUBLANES - MOE_TOP_K, tn), jnp.int32)
    idx_ref[...] = jnp.concatenate([gsel * ne + i1, gsel * ne + i2, zero_i], axis=0)
    w_rows = jnp.concatenate([p_group * inv, p_group * (e2 * inv), jnp.zeros((LANES - MOE_TOP_K, tn), F32)], axis=0)
    for c in range(tn // LANES):
        w_ref[c * LANES:(c + 1) * LANES, :] = w_rows[:, c * LANES:(c + 1) * LANES].T


def _route(logits_t):
    rows, t = logits_t.shape
    tn = 2048
    return pl.pallas_call(
        _route_kernel,
        out_shape=(jax.ShapeDtypeStruct((SUBLANES, t), jnp.int32),
                   jax.ShapeDtypeStruct((t, LANES), F32)),
        grid=(t // tn,),
        in_specs=[pl.BlockSpec((rows, tn), lambda i: (0, i))],
        out_specs=(pl.BlockSpec((SUBLANES, tn), lambda i: (0, i)),
                   pl.BlockSpec((tn, LANES), lambda i: (i, 0))),
        compiler_params=_cparams(("arbitrary",), 32),
        name="route",
    )(logits_t)


def _dispatch_tables(eidx, t):
    m = MOE_ROWS
    a_total = MOE_TOP_K * t
    n_blocks = a_total // m + N_EXPERTS
    n_rows = n_blocks * m
    src_bits = (a_total - 1).bit_length()
    experts = jnp.arange(N_EXPERTS, dtype=jnp.int32)
    e_flat = eidx[:MOE_TOP_K].reshape(a_total)
    counts = jnp.sum((e_flat[:, None] == experts[None, :]).astype(jnp.int32), axis=0)
    padded = (counts + m - 1) // m * m
    pad_end = jnp.cumsum(padded)
    pad_start = pad_end - padded
    start = jnp.cumsum(counts) - counts
    n_used = (pad_end[-1] // m).astype(jnp.int32)
    packed = lax.sort(e_flat * (1 << src_bits) + jnp.arange(a_total, dtype=jnp.int32))
    order = packed & ((1 << src_bits) - 1)

    def lookup(idx, table):
        return jnp.sum(jnp.where(idx[:, None] == experts[None, :], table[None, :], 0), axis=1)

    blk = jnp.arange(n_blocks, dtype=jnp.int32)
    block_e = jnp.minimum(jnp.sum((blk[:, None] * m >= pad_end[None, :]).astype(jnp.int32), axis=1), N_EXPERTS - 1)
    block_off = blk * m - lookup(block_e, pad_start)
    block_valid = jnp.clip(lookup(block_e, counts) - block_off, 0, m)
    block_valid = jnp.where(blk < n_used, block_valid, 0)
    sorted_pos = (lookup(block_e, start) + block_off)[:, None] + jnp.arange(m, dtype=jnp.int32)[None, :]
    valid = jnp.arange(m, dtype=jnp.int32)[None, :] < block_valid[:, None]
    spare = a_total + jnp.arange(m, dtype=jnp.int32)[None, :]
    row_dst = jnp.where(valid, order[jnp.clip(sorted_pos, 0, a_total - 1)], spare)
    row_dst = jnp.concatenate([spare, row_dst, jnp.broadcast_to(spare, (MOE_TABLE_TAIL, m))], axis=0)
    block_e = jnp.where(blk < n_used, block_e, jnp.max(jnp.where(blk < n_used, block_e, 0)))
    block_first = jnp.concatenate([jnp.ones((1,), jnp.int32), (block_e[1:] != block_e[:-1]).astype(jnp.int32)])
    used_e = jnp.where(blk < n_used, block_e, N_EXPERTS)
    later = jnp.where(used_e[None, :] > block_e[:, None], used_e[None, :], N_EXPERTS)
    block_next = jnp.min(later, axis=1)
    block_next = jnp.where(block_next < N_EXPERTS, block_next, -1).astype(jnp.int32)
    return block_e, block_first, block_next, n_used.reshape(1), row_dst


def _moe_kernel(be_ref, first_ref, next_ref, nu_ref, idx_hbm, h_hbm, wg_hbm, wu_hbm, wd_hbm, out_hbm,
                idx_s, xbuf, ybuf, wg_f, wu_f, wd_f, wgb, wub, wdb, sem_i, sem_g, sem_s, sem_w,
                *, tokens, layer, ff_chunk):
    m = MOE_ROWS
    i = pl.program_id(0)
    n_used = nu_ref[0]
    par = lax.rem(i, 2)
    other = 1 - par
    ff = wgb.shape[1]
    half = wgb.shape[0] // 2
    tr = half // LANES
    slot_rows = m * tr
    n_chunks = ff // ff_chunk
    active = i < n_used

    def weight_copies(expert):
        return [pltpu.make_async_copy(src.at[layer, expert], dst, sem_w.at[k])
                for k, (src, dst) in enumerate(((wg_hbm, wg_f), (wu_hbm, wu_f), (wd_hbm, wd_f)))]

    def tile_rows_at(row):
        return pl.ds(pl.multiple_of(row * tr, tr), tr)

    def idx_copy(table_row):
        s = jnp.bitwise_and(table_row, MOE_IDX_SLOTS - 1)
        return pltpu.make_async_copy(idx_hbm.at[table_row], idx_s.at[s], sem_i.at[s])

    def gather_row(buf_half, table_row, j):
        tok = jnp.bitwise_and(idx_s[jnp.bitwise_and(table_row, MOE_IDX_SLOTS - 1), j], tokens - 1)
        pltpu.make_async_copy(h_hbm.at[tile_rows_at(tok)], xbuf.at[tile_rows_at(buf_half * m + j)],
                              sem_g.at[buf_half]).start()

    def scatter_row(buf_half, table_row, j):
        dst = idx_s[jnp.bitwise_and(table_row, MOE_IDX_SLOTS - 1), j]
        pltpu.make_async_copy(ybuf.at[tile_rows_at(buf_half * m + j)], out_hbm.at[tile_rows_at(dst)],
                              sem_s.at[buf_half]).start()

    def half_rows(buf_half):
        return pl.ds(pl.multiple_of(buf_half * slot_rows, slot_rows), slot_rows)

    def wait_gather(buf_half):
        pltpu.make_async_copy(h_hbm.at[pl.ds(0, slot_rows)], xbuf.at[half_rows(buf_half)], sem_g.at[buf_half]).wait()

    def wait_scatter(buf_half):
        pltpu.make_async_copy(ybuf.at[half_rows(buf_half)], out_hbm.at[pl.ds(0, slot_rows)], sem_s.at[buf_half]).wait()

    base = pl.multiple_of(par * slot_rows, slot_rows)

    @pl.when(i == 0)
    def _():
        ybuf[...] = jnp.zeros_like(ybuf)
        for cp in weight_copies(be_ref[0]):
            cp.start()
        for row in range(3):
            idx_copy(row).start()
        for row in range(3):
            idx_copy(row).wait()
        idx_copy(3).start()

        def body(j, carry):
            gather_row(0, 1, j)
            return carry
        lax.fori_loop(0, m, body, 0, unroll=8)

    @pl.when(jnp.logical_and(active, first_ref[i] == 1))
    def _():
        for cp in weight_copies(be_ref[i]):
            cp.wait()
        rows = 256
        for src, dst in ((wg_f, wgb), (wu_f, wub), (wd_f, wdb)):
            def cast_body(r, carry, src=src, dst=dst):
                sl = pl.ds(pl.multiple_of(r * rows, rows), rows)
                dst[sl, :] = src[sl, :].astype(BF16)
                return carry
            lax.fori_loop(0, src.shape[0] // rows, cast_body, 0)

        @pl.when(next_ref[i] >= 0)
        def _():
            for cp in weight_copies(next_ref[i]):
                cp.start()

    @pl.when(jnp.logical_and(active, i >= 1))
    def _():
        idx_copy(i + 2).wait()
        idx_copy(i + 3).start()
        wait_scatter(par)

    @pl.when(active)
    def _():
        wait_gather(par)
        x_lo, x_hi = _load_token_tiles(xbuf, base, m, tr)
        x_lo, x_hi = x_lo.astype(BF16), x_hi.astype(BF16)
        n_dots = 5 * n_chunks
        issued = 0

        def issue_rows(dot_index):
            nonlocal issued
            upto = (m * (dot_index + 1)) // n_dots
            for j in range(issued, upto):
                gather_row(other, i + 2, j)
                scatter_row(other, i, j)
            issued = upto

        def dot_with_rows(dot_index, a, b):
            issue_rows(dot_index)
            return _dot(a, b)

        for c in range(n_chunks):
            cs = slice(c * ff_chunk, (c + 1) * ff_chunk)
            k = 5 * c
            g = dot_with_rows(k, x_lo, wgb[0:half, cs]) + dot_with_rows(k + 1, x_hi, wgb[half:, cs])
            u = dot_with_rows(k + 2, x_lo, wub[0:half, cs]) + dot_with_rows(k + 3, x_hi, wub[half:, cs])
            act = (g * jax.nn.sigmoid(g) * u).astype(BF16)
            part = dot_with_rows(k + 4, act, wdb[cs, :])
            acc = part if c == 0 else acc + part
        _store_token_tiles(ybuf, base, acc)

        @pl.when(i == n_used - 1)
        def _():
            wait_scatter(other)

            def body(j, carry):
                scatter_row(par, i + 1, j)
                return carry
            lax.fori_loop(0, m, body, 0, unroll=8)
            wait_scatter(par)
            wait_gather(other)
            idx_copy(i + 3).wait()


def _moe(block_e, block_first, block_next, n_used, row_dst, h_packed, wg, wu, wd, layer):
    d, ff = wg.shape[2], wg.shape[3]
    tr = d // 2 // LANES
    t = h_packed.shape[0] // tr
    m = MOE_ROWS
    n_blocks = block_e.shape[0]
    kern = functools.partial(_moe_kernel, tokens=t, layer=layer, ff_chunk=256)
    grid_spec = pltpu.PrefetchScalarGridSpec(
        num_scalar_prefetch=4,
        grid=(n_blocks,),
        in_specs=[pl.BlockSpec(memory_space=pl.ANY) for _ in range(5)],
        out_specs=pl.BlockSpec(memory_space=pl.ANY),
        scratch_shapes=[
            pltpu.SMEM((MOE_IDX_SLOTS, m), jnp.int32),
            pltpu.VMEM((2 * m * tr, LANES), jnp.uint32),
            pltpu.VMEM((2 * m * tr, LANES), jnp.uint32),
            pltpu.VMEM((d, ff), F32),
            pltpu.VMEM((d, ff), F32),
            pltpu.VMEM((ff, d), F32),
            pltpu.VMEM((d, ff), BF16),
            pltpu.VMEM((d, ff), BF16),
            pltpu.VMEM((ff, d), BF16),
            pltpu.SemaphoreType.DMA((MOE_IDX_SLOTS,)),
            pltpu.SemaphoreType.DMA((2,)),
            pltpu.SemaphoreType.DMA((2,)),
            pltpu.SemaphoreType.DMA((3,)),
        ],
    )
    return pl.pallas_call(
        kern,
        out_shape=jax.ShapeDtypeStruct(((MOE_TOP_K * t + m) * tr, LANES), jnp.uint32),
        grid_spec=grid_spec,
        compiler_params=_cparams(("arbitrary",), 58),
        name="moe_experts",
    )(block_e, block_first, block_next, n_used, row_dst, h_packed, wg, wu, wd)


def _combine_kernel(x_ref, y0_ref, y1_ref, w_ref, mod_ref, o_ref):
    tm, d = x_ref.shape
    half = d // 2
    tr = half // LANES
    w0 = w_ref[:, 0:1]
    w1 = w_ref[:, 1:2]
    lo0, hi0 = _load_token_tiles(y0_ref, 0, tm, tr)
    lo1, hi1 = _load_token_tiles(y1_ref, 0, tm, tr)
    o_ref[:, 0:half] = x_ref[:, 0:half] + mod_ref[5:6, 0:half] * (w0 * lo0 + w1 * lo1)
    o_ref[:, half:] = x_ref[:, half:] + mod_ref[5:6, half:] * (w0 * hi0 + w1 * hi1)


def _combine(x2d, y, w_tok, mod, seq):
    t, d = x2d.shape
    tm = 512
    tr = d // 2 // LANES
    blocks_per_seq = seq // tm
    nblk = t // tm
    return pl.pallas_call(
        _combine_kernel,
        out_shape=jax.ShapeDtypeStruct((t, d), F32),
        grid=(nblk,),
        in_specs=[
            pl.BlockSpec((tm, d), lambda i: (i, 0)),
            pl.BlockSpec((tm * tr, LANES), lambda i: (i, 0)),
            pl.BlockSpec((tm * tr, LANES), lambda i: (nblk + i, 0)),
            pl.BlockSpec((tm, LANES), lambda i: (i, 0)),
            pl.BlockSpec((None, 6, d), lambda i: (i // blocks_per_seq, 0, 0)),
        ],
        out_specs=pl.BlockSpec((tm, d), lambda i: (i, 0)),
        compiler_params=_cparams(("arbitrary",), 48),
        name="combine",
    )(x2d, y, y, w_tok, mod)


def kernel(x, c, rel_bias, w_mod, b_mod, norm1_w, norm2_w, w_in, q_norm_w, k_norm_w, conv_w, gate_b, h_norm_w, w_out, router_group_w, router_group_b, router_expert_w, router_expert_b, w_gate, w_up, w_down):
    batch, seq, d = x.shape
    depth = w_mod.shape[0]
    t = batch * seq
    attn_width = d // 2
    mlstm_width = d - attn_width
    dv = mlstm_width // MLSTM_HEADS
    dk = dv // 2
    n_main = 3 * attn_width + 2 * MLSTM_HEADS * dk + 2 * mlstm_width
    assert dk == LANES and attn_width % LANES == 0 and seq % 1024 == 0 and t % MOE_ROWS == 0
    assert t & (t - 1) == 0, "the MoE row tables recover the token of assignment k * t + token with a bit mask"
    assert all((seq // dil) % ATTN_BLOCK == 0 and window // dil == ATTN_BLOCK for window, dil in DILATED_PATTERNS)

    mod_all = _modulation(c, w_mod, b_mod)
    bias_tab = _attn_bias_tables(rel_bias)
    x2d = x.reshape(t, d)

    for l in range(depth):
        mod = mod_all[l]
        w_main = w_in[l, :, :n_main].astype(BF16)
        w_gates = jnp.zeros((d, LANES), F32).at[:, :2 * MLSTM_HEADS].set(w_in[l, :, n_main:])
        wg_hi, wg_lo = _split_bf16(w_gates)
        proj, gates = _in_proj(x2d, mod, norm1_w[l].reshape(1, d), w_main, wg_hi, wg_lo, seq)

        qw2 = jnp.tile(q_norm_w[l], LANES // ATTN_HEAD_DIM).reshape(1, LANES)
        kw2 = jnp.tile(k_norm_w[l], LANES // ATTN_HEAD_DIM).reshape(1, LANES)
        attn = _attention(proj, qw2, kw2, bias_tab, batch, seq, attn_width)

        gate_b_row = jnp.zeros((1, LANES), F32).at[0, :2 * MLSTM_HEADS].set(gate_b[l])
        hm = _mlstm(proj, gates, conv_w[l], gate_b_row, h_norm_w[l].reshape(1, mlstm_width),
                    batch, seq, attn_width, dk, dv)

        w_r = jnp.concatenate([router_group_w[l], router_expert_w[l]], axis=1).T
        w_r = jnp.zeros((ROUTER_ROWS, d), F32).at[:w_r.shape[0]].set(w_r)
        wr_hi, wr_lo = _split_bf16(w_r)
        b_r = jnp.concatenate([router_group_b[l], router_expert_b[l]])
        b_r = jnp.zeros((ROUTER_ROWS, 1), F32).at[:b_r.shape[0], 0].set(b_r)
        x2d, h_packed, logits_t = _out_proj(
            attn.reshape(t, attn_width), hm.reshape(t, mlstm_width), w_out[l].astype(BF16),
            x2d, mod, norm2_w[l].reshape(1, d), wr_hi, wr_lo, b_r, seq)

        eidx, w_tok = _route(logits_t)
        block_e, block_first, block_next, n_used, row_dst = _dispatch_tables(eidx, t)
        y = _moe(block_e, block_first, block_next, n_used, row_dst, h_packed, w_gate, w_up, w_down, l)
        x2d = _combine(x2d, y, w_tok, mod, seq)

    return x2d.reshape(batch, seq, d)
```

```python
import functools

import jax
import jax.numpy as jnp
import numpy as np
from jax import lax
from jax.experimental import pallas as pl
from jax.experimental.pallas import tpu as pltpu

F32 = jnp.float32
BF16 = jnp.bfloat16

NORM_EPS = 1e-6
ATTN_HEAD_DIM = 64
ATTN_BLOCK = 128
ATTN_MAJOR = 16
LOG2E = 1.4426950408889634
DILATED_PATTERNS = ((128, 1), (512, 4), (2048, 16))
NUM_BUCKETS = 32
REL_MAX_DIST = 2048
MLSTM_HEADS = 4
MLSTM_CHUNK = 128
CONV_WIDTH = 4
MOE_GROUPS = 4
MOE_EXPERTS_PER_GROUP = 8
N_EXPERTS = MOE_GROUPS * MOE_EXPERTS_PER_GROUP
MOE_TOP_K = 2
LANES = 128
SUBLANES = 8
ROUTER_ROWS = 40
MOE_ROWS = 256
MOE_IDX_SLOTS = 4
MOE_TABLE_TAIL = 2
MASK_VALUE = -1e30
MiB = 1024 * 1024


def _cparams(sem, vmem_mib):
    return pltpu.CompilerParams(dimension_semantics=sem, vmem_limit_bytes=vmem_mib * MiB)


def _split_bf16(a):
    hi = a.astype(BF16)
    lo = (a - hi.astype(F32)).astype(BF16)
    return hi, lo


def _dot(a, b):
    return jnp.dot(a, b, preferred_element_type=F32)


def _dot_nt(a, b):
    return lax.dot_general(a, b, (((1,), (1,)), ((), ())), preferred_element_type=F32)


def _dot_tn(a, b):
    return lax.dot_general(a, b, (((0,), (0,)), ((), ())), preferred_element_type=F32)


def _dot3(a_f32, w_hi, w_lo):
    a_hi, a_lo = _split_bf16(a_f32)
    return _dot(a_hi, w_hi) + _dot(a_lo, w_hi) + _dot(a_hi, w_lo)


def _rms_modulate(x, norm_w, scale, shift):
    ms = jnp.mean(x * x, axis=-1, keepdims=True)
    return x * lax.rsqrt(ms + NORM_EPS) * norm_w * (1.0 + scale) + shift


def _mod_kernel(c_ref, w_ref, b_ref, o_ref):
    c = c_ref[...]
    a = c * jax.nn.sigmoid(c)
    w_hi, w_lo = _split_bf16(w_ref[...])
    o_ref[...] = _dot3(a, w_hi, w_lo) + b_ref[...]


def _modulation(c, w_mod, b_mod):
    depth, d, n = w_mod.shape
    b = c.shape[0]
    bp = -(-b // SUBLANES) * SUBLANES
    cp = jnp.zeros((bp, d), F32).at[:b].set(c)
    tn = 1024
    out = pl.pallas_call(
        _mod_kernel,
        out_shape=jax.ShapeDtypeStruct((depth, bp, n), F32),
        grid=(depth, n // tn),
        in_specs=[
            pl.BlockSpec((bp, d), lambda l, j: (0, 0)),
            pl.BlockSpec((None, d, tn), lambda l, j: (l, 0, j)),
            pl.BlockSpec((None, 1, tn), lambda l, j: (l, 0, j)),
        ],
        out_specs=pl.BlockSpec((None, bp, tn), lambda l, j: (l, 0, j)),
        compiler_params=_cparams(("arbitrary", "arbitrary"), 40),
        name="modulation",
    )(cp, w_mod, b_mod.reshape(depth, 1, n))
    return out[:, :b].reshape(depth, b, 6, d)


def _in_proj_kernel(x_ref, mod_ref, nw_ref, w_ref, wg_ref, o_ref, g_ref, h_scr):
    @pl.when(pl.program_id(1) == 0)
    def _():
        h = _rms_modulate(x_ref[...], nw_ref[...], mod_ref[1:2, :], mod_ref[0:1, :])
        h_scr[...] = h.astype(BF16)
        g_ref[...] = _dot(h_scr[...], wg_ref[...])

    o_ref[...] = _dot(h_scr[...], w_ref[...]).astype(BF16)


def _in_proj(x2d, mod, norm_w, w_main, w_gates, seq):
    t, d = x2d.shape
    n = w_main.shape[1]
    tm, tn = 1024, 512
    blocks_per_seq = seq // tm
    return pl.pallas_call(
        _in_proj_kernel,
        out_shape=(jax.ShapeDtypeStruct((t, n), BF16), jax.ShapeDtypeStruct((t, LANES), F32)),
        grid=(t // tm, n // tn),
        in_specs=[
            pl.BlockSpec((tm, d), lambda i, j: (i, 0)),
            pl.BlockSpec((None, 6, d), lambda i, j: (i // blocks_per_seq, 0, 0)),
            pl.BlockSpec((1, d), lambda i, j: (0, 0)),
            pl.BlockSpec((d, tn), lambda i, j: (0, j)),
            pl.BlockSpec((d, LANES), lambda i, j: (0, 0)),
        ],
        out_specs=(pl.BlockSpec((tm, tn), lambda i, j: (i, j)),
                   pl.BlockSpec((tm, LANES), lambda i, j: (i, 0))),
        scratch_shapes=[pltpu.VMEM((tm, d), BF16)],
        compiler_params=_cparams(("arbitrary", "arbitrary"), 48),
        name="in_proj",
    )(x2d, mod, norm_w, w_main, w_gates)


def _t5_causal_bucket(dist):
    max_exact = NUM_BUCKETS // 2
    d = np.maximum(dist, 1).astype(np.float32)
    large = max_exact + (np.log(d / max_exact) / np.log(REL_MAX_DIST / max_exact)
                         * (NUM_BUCKETS - max_exact)).astype(np.int32)
    large = np.minimum(large, NUM_BUCKETS - 1)
    return np.where(dist < max_exact, dist, large).astype(np.int32)


def _block_positions(dilation):
    g = ATTN_MAJOR // dilation
    plen = ATTN_BLOCK // g
    i = np.arange(ATTN_BLOCK)
    return g * (i % plen) + i // plen


def _attn_bias_tables(rel_bias):
    blk = ATTN_BLOCK
    tables = []
    for window, dilation in DILATED_PATTERNS:
        win = window // dilation
        loc = _block_positions(dilation)
        kpos = np.concatenate([loc, loc + blk])
        rel = loc[:, None] - kpos[None, :] + blk
        valid = (rel >= 0) & (rel <= win)
        bucket = _t5_causal_bucket(np.clip(rel, 0, win) * dilation)
        onehot = (jnp.asarray(bucket)[:, :, None] == jnp.arange(NUM_BUCKETS, dtype=jnp.int32)).astype(F32)
        bias = jnp.einsum('icb,bh->hic', onehot, rel_bias.astype(F32), precision=lax.Precision.HIGHEST) * LOG2E
        normal = jnp.where(valid[None], bias, MASK_VALUE)
        first = jnp.where((valid & (kpos >= blk)[None, :])[None], bias, MASK_VALUE)
        tables.append(jnp.stack([normal, first], axis=0))
    return jnp.stack(tables, axis=0)


def _attn_kernel(q_ref, k_ref, v_ref, qw_ref, kw_ref, bias_ref, o_ref,
                 stage, stage2, qs, ks, vs, acc_s, m_s, l_s, *, seq, unroll):
    blk = ATTN_BLOCK
    major = ATTN_MAJOR
    l16 = seq // major
    lane = lax.broadcasted_iota(jnp.int32, (1, LANES), 1)
    head0 = lane < ATTN_HEAD_DIM
    gi = lax.broadcasted_iota(jnp.int32, (LANES, LANES), 0) // ATTN_HEAD_DIM
    gj = lax.broadcasted_iota(jnp.int32, (LANES, LANES), 1) // ATTN_HEAD_DIM
    group_ones = (gi == gj).astype(BF16)
    ones_cols = jnp.ones((2 * blk, LANES), BF16)
    rows = 256

    quarter = seq // 4

    def to_major(dst):
        def pass1(cidx, carry):
            for r_lo in range(4):
                stage2[pl.ds(pl.multiple_of(r_lo * quarter + cidx * rows, rows), rows), :] = (
                    stage[pl.ds(r_lo + 4 * rows * cidx, rows, stride=4), :])
            return carry
        lax.fori_loop(0, quarter // rows, pass1, 0)

        def pass2(r, carry):
            src0 = jnp.bitwise_and(r, 3) * quarter + lax.shift_right_logical(r, 2)
            dst[pl.ds(pl.multiple_of(r * l16, l16), l16), :] = stage2[pl.ds(src0, l16, stride=4), :]
            return carry
        lax.fori_loop(0, major, pass2, 0)

    for src, dst, w_ref, scale in ((q_ref, qs, qw_ref, ATTN_HEAD_DIM ** -0.5 * LOG2E),
                                   (k_ref, ks, kw_ref, 1.0), (v_ref, vs, None, None)):
        def norm_body(c, carry, src=src, w_ref=w_ref, scale=scale):
            sl = pl.ds(pl.multiple_of(c * rows, rows), rows)
            xx = src[sl, :].astype(F32)
            if w_ref is not None:
                hi, lo = _split_bf16(xx * xx)
                ssq = _dot(hi, group_ones) + _dot(lo, group_ones)
                xx = xx * lax.rsqrt(ssq * (1.0 / ATTN_HEAD_DIM) + NORM_EPS) * w_ref[...] * scale
            stage[sl, :] = xx
            return carry
        lax.fori_loop(0, seq // rows, norm_body, 0, unroll=4)
        to_major(dst)

    for p, (_, dil) in enumerate(DILATED_PATTERNS):
        groups = major // dil
        plen = blk // groups
        nb_log2 = ((seq // dil) // blk).bit_length() - 1

        def starts(r_d, n, dil=dil, groups=groups, plen=plen):
            return [pl.multiple_of((j * dil + r_d) * l16 + n * plen, SUBLANES) for j in range(groups)]

        def load_blk(ref, st, plen=plen):
            return jnp.concatenate([ref[pl.ds(s, plen), :] for s in st], axis=0)

        def store_blk(ref, st, val, plen=plen):
            for j, s in enumerate(st):
                ref[pl.ds(s, plen), :] = val[j * plen:(j + 1) * plen]

        def body(it, carry, p=p, nb_log2=nb_log2, starts=starts, load_blk=load_blk, store_blk=store_blk):
            loaded = []
            for u in range(unroll):
                idx = it * unroll + u
                r_d = lax.shift_right_logical(idx, nb_log2)
                n = idx - lax.shift_left(r_d, nb_log2)
                cur = starts(r_d, n)
                prv = starts(r_d, jnp.maximum(n - 1, 0))
                first = (n == 0).astype(jnp.int32)
                qb = load_blk(qs, cur).astype(BF16)
                kw = jnp.concatenate([load_blk(ks, prv), load_blk(ks, cur)], axis=0).astype(BF16)
                vw = jnp.concatenate([load_blk(vs, prv), load_blk(vs, cur)], axis=0).astype(BF16)
                old = (load_blk(m_s, cur), load_blk(l_s, cur), load_blk(acc_s, cur)) if p > 0 else None
                loaded.append((cur, first, qb, kw, vw, old))
            results = []
            for cur, first, qb, kw, vw, old in loaded:
                vw_aug = jnp.concatenate([vw, ones_cols], axis=1)
                ms, res = [], []
                for h, hmask in enumerate((head0, jnp.logical_not(head0))):
                    qh = jnp.where(hmask, qb, jnp.zeros_like(qb))
                    s = _dot_nt(qh, kw) + bias_ref[p, first, h]
                    mh = jnp.max(s, axis=-1, keepdims=True)
                    ms.append(mh)
                    res.append(_dot(jnp.exp2(s - mh).astype(BF16), vw_aug))
                m_b = jnp.where(head0, ms[0], ms[1])
                a_b = jnp.where(head0, res[0][:, :LANES], res[1][:, :LANES])
                l_b = jnp.where(head0, res[0][:, LANES:], res[1][:, LANES:])
                if old is None:
                    results.append((cur, m_b, l_b, a_b))
                else:
                    m_o, l_o, a_o = old
                    m_new = jnp.maximum(m_o, m_b)
                    w_o = jnp.exp2(m_o - m_new)
                    w_b = jnp.exp2(m_b - m_new)
                    results.append((cur, m_new, w_o * l_o + w_b * l_b, w_o * a_o + w_b * a_b))
            for cur, m_new, l_new, a_new in results:
                store_blk(m_s, cur, m_new)
                store_blk(l_s, cur, l_new)
                store_blk(acc_s, cur, a_new)
            return carry

        lax.fori_loop(0, (seq // blk) // unroll, body, 0)

    def from_major(r, carry):
        sl = pl.ds(pl.multiple_of(r * l16, l16), l16)
        dst0 = jnp.bitwise_and(r, 3) * quarter + lax.shift_right_logical(r, 2)
        stage2[pl.ds(dst0, l16, stride=4), :] = acc_s[sl, :] / l_s[sl, :]
        return carry

    lax.fori_loop(0, major, from_major, 0)

    def to_natural(cidx, carry):
        for r_lo in range(4):
            stage[pl.ds(r_lo + 4 * rows * cidx, rows, stride=4), :] = (
                stage2[pl.ds(pl.multiple_of(r_lo * quarter + cidx * rows, rows), rows), :])
        return carry

    lax.fori_loop(0, quarter // rows, to_natural, 0)

    def out_body(c, carry):
        sl = pl.ds(pl.multiple_of(c * rows, rows), rows)
        o_ref[sl, :] = stage[sl, :].astype(o_ref.dtype)
        return carry

    lax.fori_loop(0, seq // rows, out_body, 0)


def _attention(proj, qw2, kw2, bias_tab, batch, seq, attn_width):
    t, n = proj.shape
    proj3 = proj.reshape(batch, seq, n)
    pairs = attn_width // LANES
    npat = len(DILATED_PATTERNS)
    kern = functools.partial(_attn_kernel, seq=seq, unroll=8)
    return pl.pallas_call(
        kern,
        out_shape=jax.ShapeDtypeStruct((batch, seq, attn_width), BF16),
        grid=(batch, pairs),
        in_specs=[
            pl.BlockSpec((None, seq, LANES), lambda b, h: (b, 0, h)),
            pl.BlockSpec((None, seq, LANES), lambda b, h: (b, 0, pairs + h)),
            pl.BlockSpec((None, seq, LANES), lambda b, h: (b, 0, 2 * pairs + h)),
            pl.BlockSpec((1, LANES), lambda b, h: (0, 0)),
            pl.BlockSpec((1, LANES), lambda b, h: (0, 0)),
            pl.BlockSpec((npat, 2, 2, ATTN_BLOCK, 2 * ATTN_BLOCK), lambda b, h: (0, 0, h, 0, 0)),
        ],
        out_specs=pl.BlockSpec((None, seq, LANES), lambda b, h: (b, 0, h)),
        scratch_shapes=[pltpu.VMEM((seq, LANES), F32) for _ in range(8)],
        compiler_params=_cparams(("arbitrary", "arbitrary"), 40),
        name="dilated_attention",
    )(proj3, proj3, proj3, qw2, kw2, bias_tab)


def _mlstm_kernel(q_ref, k_ref, v_ref, og_ref, g_ref, cw_ref, gb_ref, hw_ref, o_ref,
                  x_scr, c_scr, m_scr, *, rows, dk, dv):
    ch = MLSTM_CHUNK
    nheads = MLSTM_HEADS
    qk_w = nheads * dk
    aug = dv + LANES

    @pl.when(pl.program_id(1) == 0)
    def _():
        x_scr[0:SUBLANES, :] = jnp.zeros((SUBLANES, 2 * qk_w), F32)
        c_scr[...] = jnp.zeros_like(c_scr)
        m_scr[...] = jnp.zeros_like(m_scr)

    x_scr[SUBLANES:, 0:qk_w] = q_ref[...].astype(F32)
    x_scr[SUBLANES:, qk_w:] = k_ref[...].astype(F32)

    ri = lax.broadcasted_iota(jnp.int32, (ch, ch), 0)
    ci = lax.broadcasted_iota(jnp.int32, (ch, ch), 1)
    causal = ci <= ri
    tril = causal.astype(BF16)
    row8 = lax.broadcasted_iota(jnp.int32, (SUBLANES, 1), 0)
    ones_col = (lax.broadcasted_iota(jnp.int32, (ch, LANES), 1) == 0).astype(F32)
    k_scale = dk ** -0.5

    def chunk_body(c, carry):
        r0 = pl.multiple_of(c * ch, ch)
        prev8 = x_scr[pl.ds(r0, SUBLANES), :]
        cur = x_scr[pl.ds(r0 + SUBLANES, ch), :]
        conv = cur * cw_ref[CONV_WIDTH - 1:CONV_WIDTH, :]
        for sh in range(1, CONV_WIDTH):
            rolled = pltpu.roll(cur, sh, 0)
            head_rows = jnp.where(row8 < sh, pltpu.roll(prev8, sh, 0), rolled[0:SUBLANES])
            shifted = jnp.concatenate([head_rows, rolled[SUBLANES:]], axis=0)
            conv = conv + shifted * cw_ref[CONV_WIDTH - 1 - sh:CONV_WIDTH - sh, :]
        qk = conv * jax.nn.sigmoid(conv)

        gates = g_ref[pl.ds(r0, ch), :] + gb_ref[...]
        logf = jax.nn.log_sigmoid(gates)
        lf_hi, lf_lo = _split_bf16(logf)
        bcum = _dot(tril, lf_hi) + _dot(tril, lf_lo)
        gates_t = gates.T
        bcum_t = bcum.T

        for h in range(nheads):
            q = qk[:, h * dk:(h + 1) * dk]
            k = qk[:, qk_w + h * dk:qk_w + (h + 1) * dk] * k_scale
            v = v_ref[pl.ds(r0, ch), h * dv:(h + 1) * dv].astype(F32)
            v_aug = jnp.concatenate([v, ones_col], axis=1)
            i_col = gates[:, h:h + 1]
            b_col = bcum[:, nheads + h:nheads + h + 1]
            i_row = gates_t[h:h + 1, :]
            b_row = bcum_t[nheads + h:nheads + h + 1, :]
            m_prev = m_scr[h, 0:1, 0:1]
            c_prev = c_scr[h]

            dlog = jnp.where(causal, b_col - b_row + i_row, -jnp.inf)
            inter = b_col + m_prev
            m_t = jnp.maximum(inter, jnp.max(dlog, axis=-1, keepdims=True))
            qb = q.astype(BF16)
            kb = k.astype(BF16)
            sm = _dot_nt(qb, kb) * jnp.exp(dlog - m_t)
            e_inter = jnp.exp(inter - m_t)
            tot = e_inter * _dot(qb, c_prev.astype(BF16)) + _dot(sm.astype(BF16), v_aug.astype(BF16))
            num = tot[:, 0:dv]
            den = tot[:, dv:dv + 1]
            hcell = num / jnp.maximum(jnp.abs(den), jnp.exp(-m_t))

            g_last = b_col[ch - 1:ch, :]
            a_col = g_last - b_col + i_col
            m_new = jnp.maximum(g_last + m_prev, jnp.max(a_col, axis=0, keepdims=True))
            decay = jnp.exp(g_last + m_prev - m_new)
            w_col = jnp.exp(a_col - m_new)
            c_scr[h] = decay * c_prev + _dot_tn(kb, (w_col * v_aug).astype(BF16))
            m_scr[h] = jnp.broadcast_to(m_new, (SUBLANES, LANES))

            ms = jnp.mean(hcell * hcell, axis=-1, keepdims=True)
            hn = hcell * lax.rsqrt(ms + NORM_EPS) * hw_ref[:, h * dv:(h + 1) * dv]
            og = og_ref[pl.ds(r0, ch), h * dv:(h + 1) * dv].astype(F32)
            o_ref[pl.ds(r0, ch), h * dv:(h + 1) * dv] = (jax.nn.sigmoid(og) * hn).astype(o_ref.dtype)
        return carry

    lax.fori_loop(0, rows // ch, chunk_body, 0)
    x_scr[0:SUBLANES, :] = x_scr[rows:rows + SUBLANES, :]


def _mlstm(proj, gates, conv_w, gate_b_row, h_norm_w, batch, seq, attn_width, dk, dv):
    t, n = proj.shape
    proj3 = proj.reshape(batch, seq, n)
    gates3 = gates.reshape(batch, seq, LANES)
    nheads = MLSTM_HEADS
    qk_w = nheads * dk
    v_w = nheads * dv
    rows = 512
    q_blk = (3 * attn_width) // qk_w
    v_blk = (3 * attn_width + 2 * qk_w) // v_w
    kern = functools.partial(_mlstm_kernel, rows=rows, dk=dk, dv=dv)
    return pl.pallas_call(
        kern,
        out_shape=jax.ShapeDtypeStruct((batch, seq, v_w), BF16),
        grid=(batch, seq // rows),
        in_specs=[
            pl.BlockSpec((None, rows, qk_w), lambda b, j: (b, j, q_blk)),
            pl.BlockSpec((None, rows, qk_w), lambda b, j: (b, j, q_blk + 1)),
            pl.BlockSpec((None, rows, v_w), lambda b, j: (b, j, v_blk)),
            pl.BlockSpec((None, rows, v_w), lambda b, j: (b, j, v_blk + 1)),
            pl.BlockSpec((None, rows, LANES), lambda b, j: (b, j, 0)),
            pl.BlockSpec((CONV_WIDTH, 2 * qk_w), lambda b, j: (0, 0)),
            pl.BlockSpec((1, LANES), lambda b, j: (0, 0)),
            pl.BlockSpec((1, v_w), lambda b, j: (0, 0)),
        ],
        out_specs=pl.BlockSpec((None, rows, v_w), lambda b, j: (b, j, 0)),
        scratch_shapes=[
            pltpu.VMEM((rows + SUBLANES, 2 * qk_w), F32),
            pltpu.VMEM((nheads, dk, dv + LANES), F32),
            pltpu.VMEM((nheads, SUBLANES, LANES), F32),
        ],
        compiler_params=_cparams(("arbitrary", "arbitrary"), 40),
        name="mlstm",
    )(proj3, proj3, proj3, proj3, gates3, conv_w, gate_b_row, h_norm_w)


def _store_token_tiles(ref, base, h):
    m, w2 = h.shape
    w = w2 // 2
    tile_rows = w // LANES
    u = pltpu.bitcast(h.astype(BF16).astype(F32), jnp.uint32)
    packed = (u[:, :w] >> 16) | (u[:, w:] & jnp.uint32(0xFFFF0000))
    for k in range(tile_rows):
        ref[pl.ds(base + k, m, stride=tile_rows), :] = packed[:, k * LANES:(k + 1) * LANES]


def _load_token_tiles(ref, base, m, tile_rows):
    lo, hi = [], []
    for k in range(tile_rows):
        p = ref[pl.ds(base + k, m, stride=tile_rows), :]
        lo.append(pltpu.bitcast(p << 16, F32))
        hi.append(pltpu.bitcast(p & jnp.uint32(0xFFFF0000), F32))
    return jnp.concatenate(lo, axis=1), jnp.concatenate(hi, axis=1)


def _out_proj_kernel(a_ref, m_ref, w_ref, x_ref, mod_ref, nw_ref, wrh_ref, wrl_ref, br_ref,
                     xo_ref, hp_ref, lg_ref, *, attn_width, sub_rows):
    tm, d = x_ref.shape
    tile_rows = d // 2 // LANES
    half = d // 2
    col = 4 * LANES
    for s in range(tm // sub_rows):
        rs = slice(s * sub_rows, (s + 1) * sub_rows)
        a_blk, m_blk = a_ref[rs, :], m_ref[rs, :]
        ssq = jnp.zeros((sub_rows, 1), F32)
        for c in range(d // col):
            cs = slice(c * col, (c + 1) * col)
            mix = _dot(a_blk, w_ref[0:attn_width, cs]) + _dot(m_blk, w_ref[attn_width:, cs])
            xn = x_ref[rs, cs] + mod_ref[2:3, cs] * mix
            xo_ref[rs, cs] = xn
            ssq = ssq + jnp.sum(xn * xn, axis=-1, keepdims=True)
        rstd = lax.rsqrt(ssq * (1.0 / d) + NORM_EPS)
        logits = br_ref[...]
        for k in range(tile_rows):
            parts = []
            for lo in (k * LANES, half + k * LANES):
                ks = slice(lo, lo + LANES)
                hk = xo_ref[rs, ks] * rstd * nw_ref[:, ks] * (1.0 + mod_ref[4:5, ks]) + mod_ref[3:4, ks]
                h_hi, h_lo = _split_bf16(hk)
                logits = logits + (_dot_nt(wrh_ref[:, ks], h_hi) + _dot_nt(wrh_ref[:, ks], h_lo)
                                   + _dot_nt(wrl_ref[:, ks], h_hi))
                parts.append(pltpu.bitcast(h_hi.astype(F32), jnp.uint32))
            packed = (parts[0] >> 16) | (parts[1] & jnp.uint32(0xFFFF0000))
            hp_ref[pl.ds(s * sub_rows * tile_rows + k, sub_rows, stride=tile_rows), :] = packed
        lg_ref[:, rs] = logits


def _out_proj(attn2d, hm2d, w_out, x2d, mod, norm_w, wr_hi, wr_lo, br, seq):
    t, d = x2d.shape
    aw = attn2d.shape[1]
    mw = hm2d.shape[1]
    tm = 512
    blocks_per_seq = seq // tm
    tile_rows = d // 2 // LANES
    kern = functools.partial(_out_proj_kernel, attn_width=aw, sub_rows=512)
    return pl.pallas_call(
        kern,
        out_shape=(jax.ShapeDtypeStruct((t, d), F32),
                   jax.ShapeDtypeStruct((t * tile_rows, LANES), jnp.uint32),
                   jax.ShapeDtypeStruct((ROUTER_ROWS, t), F32)),
        grid=(t // tm,),
        in_specs=[
            pl.BlockSpec((tm, aw), lambda i: (i, 0)),
            pl.BlockSpec((tm, mw), lambda i: (i, 0)),
            pl.BlockSpec((aw + mw, d), lambda i: (0, 0)),
            pl.BlockSpec((tm, d), lambda i: (i, 0)),
            pl.BlockSpec((None, 6, d), lambda i: (i // blocks_per_seq, 0, 0)),
            pl.BlockSpec((1, d), lambda i: (0, 0)),
            pl.BlockSpec((ROUTER_ROWS, d), lambda i: (0, 0)),
            pl.BlockSpec((ROUTER_ROWS, d), lambda i: (0, 0)),
            pl.BlockSpec((ROUTER_ROWS, 1), lambda i: (0, 0)),
        ],
        out_specs=(pl.BlockSpec((tm, d), lambda i: (i, 0)),
                   pl.BlockSpec((tm * tile_rows, LANES), lambda i: (i, 0)),
                   pl.BlockSpec((ROUTER_ROWS, tm), lambda i: (0, i))),
        compiler_params=_cparams(("arbitrary",), 48),
        name="out_proj",
    )(attn2d, hm2d, w_out, x2d, mod, norm_w, wr_hi, wr_lo, br)


def _route_kernel(lg_ref, idx_ref, w_ref):
    ng, ne = MOE_GROUPS, MOE_EXPERTS_PER_GROUP
    lg = [lg_ref[g:g + 1, :] for g in range(ng)]
    best, gsel = lg[0], jnp.zeros_like(lg[0], dtype=jnp.int32)
    for g in range(1, ng):
        better = lg[g] > best
        best = jnp.where(better, lg[g], best)
        gsel = jnp.where(better, g, gsel)
    denom = jnp.exp(lg[0] - best)
    for g in range(1, ng):
        denom = denom + jnp.exp(lg[g] - best)
    p_group = 1.0 / denom

    le = []
    for e in range(ne):
        v = lg_ref[ng + e:ng + e + 1, :]
        for g in range(1, ng):
            v = jnp.where(gsel == g, lg_ref[ng + g * ne + e:ng + g * ne + e + 1, :], v)
        le.append(v)
    v1, i1 = le[0], jnp.zeros_like(gsel)
    for e in range(1, ne):
        better = le[e] > v1
        v1 = jnp.where(better, le[e], v1)
        i1 = jnp.where(better, e, i1)
    v2 = jnp.full_like(v1, -jnp.inf)
    i2 = jnp.zeros_like(gsel)
    for e in range(ne):
        better = jnp.logical_and(le[e] > v2, i1 != e)
        v2 = jnp.where(better, le[e], v2)
        i2 = jnp.where(better, e, i2)
    e2 = jnp.exp(v2 - v1)
    inv = 1.0 / (1.0 + e2)
    tn = gsel.shape[1]
    zero_i = jnp.zeros((SUBLANES - MOE_TOP_K, tn), jnp.int32)
    idx_ref[...] = jnp.concatenate([gsel * ne + i1, gsel * ne + i2, zero_i], axis=0)
    w_rows = jnp.concatenate([p_group * inv, p_group * (e2 * inv), jnp.zeros((LANES - MOE_TOP_K, tn), F32)], axis=0)
    for c in range(tn // LANES):
        w_ref[c * LANES:(c + 1) * LANES, :] = w_rows[:, c * LANES:(c + 1) * LANES].T


def _route(logits_t):
    rows, t = logits_t.shape
    tn = 2048
    return pl.pallas_call(
        _route_kernel,
        out_shape=(jax.ShapeDtypeStruct((SUBLANES, t), jnp.int32),
                   jax.ShapeDtypeStruct((t, LANES), F32)),
        grid=(t // tn,),
        in_specs=[pl.BlockSpec((rows, tn), lambda i: (0, i))],
        out_specs=(pl.BlockSpec((SUBLANES, tn), lambda i: (0, i)),
                   pl.BlockSpec((tn, LANES), lambda i: (i, 0))),
        compiler_params=_cparams(("arbitrary",), 32),
        name="route",
    )(logits_t)


def _dispatch_tables(eidx, t):
    m = MOE_ROWS
    a_total = MOE_TOP_K * t
    n_blocks = a_total // m + N_EXPERTS
    n_rows = n_blocks * m
    src_bits = (a_total - 1).bit_length()
    experts = jnp.arange(N_EXPERTS, dtype=jnp.int32)
    e_flat = eidx[:MOE_TOP_K].reshape(a_total)
    counts = jnp.sum((e_flat[:, None] == experts[None, :]).astype(jnp.int32), axis=0)
    padded = (counts + m - 1) // m * m
    pad_end = jnp.cumsum(padded)
    pad_start = pad_end - padded
    start = jnp.cumsum(counts) - counts
    n_used = (pad_end[-1] // m).astype(jnp.int32)
    packed = lax.sort(e_flat * (1 << src_bits) + jnp.arange(a_total, dtype=jnp.int32))
    order = packed & ((1 << src_bits) - 1)

    def lookup(idx, table):
        return jnp.sum(jnp.where(idx[:, None] == experts[None, :], table[None, :], 0), axis=1)

    blk = jnp.arange(n_blocks, dtype=jnp.int32)
    block_e = jnp.minimum(jnp.sum((blk[:, None] * m >= pad_end[None, :]).astype(jnp.int32), axis=1), N_EXPERTS - 1)
    block_off = blk * m - lookup(block_e, pad_start)
    block_valid = jnp.clip(lookup(block_e, counts) - block_off, 0, m)
    block_valid = jnp.where(blk < n_used, block_valid, 0)
    sorted_pos = (lookup(block_e, start) + block_off)[:, None] + jnp.arange(m, dtype=jnp.int32)[None, :]
    valid = jnp.arange(m, dtype=jnp.int32)[None, :] < block_valid[:, None]
    spare = a_total + jnp.arange(m, dtype=jnp.int32)[None, :]
    row_dst = jnp.where(valid, order[jnp.clip(sorted_pos, 0, a_total - 1)], spare)
    row_dst = jnp.concatenate([spare, row_dst, jnp.broadcast_to(spare, (MOE_TABLE_TAIL, m))], axis=0)
    block_e = jnp.where(blk < n_used, block_e, jnp.max(jnp.where(blk < n_used, block_e, 0)))
    block_first = jnp.concatenate([jnp.ones((1,), jnp.int32), (block_e[1:] != block_e[:-1]).astype(jnp.int32)])
    used_e = jnp.where(blk < n_used, block_e, N_EXPERTS)
    later = jnp.where(used_e[None, :] > block_e[:, None], used_e[None, :], N_EXPERTS)
    block_next = jnp.min(later, axis=1)
    block_next = jnp.where(block_next < N_EXPERTS, block_next, -1).astype(jnp.int32)
    return block_e, block_first, block_next, n_used.reshape(1), row_dst


def _moe_kernel(be_ref, first_ref, next_ref, nu_ref, idx_hbm, h_hbm, wg_hbm, wu_hbm, wd_hbm, out_hbm,
                idx_s, xbuf, ybuf, wg_f, wu_f, wd_f, wgb, wub, wdb, sem_i, sem_g, sem_s, sem_w,
                *, tokens, layer, ff_chunk):
    m = MOE_ROWS
    i = pl.program_id(0)
    n_used = nu_ref[0]
    par = lax.rem(i, 2)
    other = 1 - par
    ff = wgb.shape[1]
    half = wgb.shape[0] // 2
    tr = half // LANES
    slot_rows = m * tr
    n_chunks = ff // ff_chunk
    active = i < n_used

    def weight_copies(expert):
        return [pltpu.make_async_copy(src.at[layer, expert], dst, sem_w.at[k])
                for k, (src, dst) in enumerate(((wg_hbm, wg_f), (wu_hbm, wu_f), (wd_hbm, wd_f)))]

    def tile_rows_at(row):
        return pl.ds(pl.multiple_of(row * tr, tr), tr)

    def idx_copy(table_row):
        s = jnp.bitwise_and(table_row, MOE_IDX_SLOTS - 1)
        return pltpu.make_async_copy(idx_hbm.at[table_row], idx_s.at[s], sem_i.at[s])

    def gather_row(buf_half, table_row, j):
        tok = jnp.bitwise_and(idx_s[jnp.bitwise_and(table_row, MOE_IDX_SLOTS - 1), j], tokens - 1)
        pltpu.make_async_copy(h_hbm.at[tile_rows_at(tok)], xbuf.at[tile_rows_at(buf_half * m + j)],
                              sem_g.at[buf_half]).start()

    def scatter_row(buf_half, table_row, j):
        dst = idx_s[jnp.bitwise_and(table_row, MOE_IDX_SLOTS - 1), j]
        pltpu.make_async_copy(ybuf.at[tile_rows_at(buf_half * m + j)], out_hbm.at[tile_rows_at(dst)],
                              sem_s.at[buf_half]).start()

    def half_rows(buf_half):
        return pl.ds(pl.multiple_of(buf_half * slot_rows, slot_rows), slot_rows)

    def wait_gather(buf_half):
        pltpu.make_async_copy(h_hbm.at[pl.ds(0, slot_rows)], xbuf.at[half_rows(buf_half)], sem_g.at[buf_half]).wait()

    def wait_scatter(buf_half):
        pltpu.make_async_copy(ybuf.at[half_rows(buf_half)], out_hbm.at[pl.ds(0, slot_rows)], sem_s.at[buf_half]).wait()

    base = pl.multiple_of(par * slot_rows, slot_rows)

    @pl.when(i == 0)
    def _():
        ybuf[...] = jnp.zeros_like(ybuf)
        for cp in weight_copies(be_ref[0]):
            cp.start(priority=1)
        for row in range(3):
            idx_copy(row).start()
        for row in range(3):
            idx_copy(row).wait()
        idx_copy(3).start()

        def body(j, carry):
            gather_row(0, 1, j)
            scatter_row(0, 0, j)
            return carry
        lax.fori_loop(0, m, body, 0, unroll=8)

    @pl.when(jnp.logical_and(active, first_ref[i] == 1))
    def _():
        for cp in weight_copies(be_ref[i]):
            cp.wait()
        rows = 256
        for src, dst in ((wg_f, wgb), (wu_f, wub), (wd_f, wdb)):
            def cast_body(r, carry, src=src, dst=dst):
                sl = pl.ds(pl.multiple_of(r * rows, rows), rows)
                dst[sl, :] = src[sl, :].astype(BF16)
                return carry
            lax.fori_loop(0, src.shape[0] // rows, cast_body, 0)

        @pl.when(next_ref[i] >= 0)
        def _():
            for cp in weight_copies(next_ref[i]):
                cp.start(priority=1)

    @pl.when(jnp.logical_and(active, i >= 1))
    def _():
        idx_copy(i + 2).wait()
        idx_copy(i + 3).start()

    @pl.when(active)
    def _():
        wait_gather(par)
        x_lo, x_hi = _load_token_tiles(xbuf, base, m, tr)
        x_lo, x_hi = x_lo.astype(BF16), x_hi.astype(BF16)
        gather_dots = (5 * n_chunks) // 2
        issued = 0

        def dot_with_rows(dot_index, a, b):
            nonlocal issued
            upto = min(m, (m * (dot_index + 1)) // gather_dots)
            for j in range(issued, upto):
                gather_row(other, i + 2, j)
            issued = upto
            return _dot(a, b)

        for c in range(n_chunks):
            cs = slice(c * ff_chunk, (c + 1) * ff_chunk)
            k = 5 * c
            g = dot_with_rows(k, x_lo, wgb[0:half, cs]) + dot_with_rows(k + 1, x_hi, wgb[half:, cs])
            u = dot_with_rows(k + 2, x_lo, wub[0:half, cs]) + dot_with_rows(k + 3, x_hi, wub[half:, cs])
            act = (g * jax.nn.sigmoid(g) * u).astype(BF16)
            part = dot_with_rows(k + 4, act, wdb[cs, :])
            acc = part if c == 0 else acc + part
        wait_scatter(par)
        for j in range(m):
            scatter_row(other, i, j)
        _store_token_tiles(ybuf, base, acc)

        @pl.when(i == n_used - 1)
        def _():
            wait_scatter(other)

            def body(j, carry):
                scatter_row(par, i + 1, j)
                return carry
            lax.fori_loop(0, m, body, 0, unroll=8)
            wait_scatter(par)
            wait_gather(other)
            idx_copy(i + 3).wait()


def _moe(block_e, block_first, block_next, n_used, row_dst, h_packed, wg, wu, wd, layer):
    d, ff = wg.shape[2], wg.shape[3]
    tr = d // 2 // LANES
    t = h_packed.shape[0] // tr
    m = MOE_ROWS
    n_blocks = block_e.shape[0]
    kern = functools.partial(_moe_kernel, tokens=t, layer=layer, ff_chunk=256)
    grid_spec = pltpu.PrefetchScalarGridSpec(
        num_scalar_prefetch=4,
        grid=(n_blocks,),
        in_specs=[pl.BlockSpec(memory_space=pl.ANY) for _ in range(5)],
        out_specs=pl.BlockSpec(memory_space=pl.ANY),
        scratch_shapes=[
            pltpu.SMEM((MOE_IDX_SLOTS, m), jnp.int32),
            pltpu.VMEM((2 * m * tr, LANES), jnp.uint32),
            pltpu.VMEM((2 * m * tr, LANES), jnp.uint32),
            pltpu.VMEM((d, ff), F32),
            pltpu.VMEM((d, ff), F32),
            pltpu.VMEM((ff, d), F32),
            pltpu.VMEM((d, ff), BF16),
            pltpu.VMEM((d, ff), BF16),
            pltpu.VMEM((ff, d), BF16),
            pltpu.SemaphoreType.DMA((MOE_IDX_SLOTS,)),
            pltpu.SemaphoreType.DMA((2,)),
            pltpu.SemaphoreType.DMA((2,)),
            pltpu.SemaphoreType.DMA((3,)),
        ],
    )
    return pl.pallas_call(
        kern,
        out_shape=jax.ShapeDtypeStruct(((MOE_TOP_K * t + m) * tr, LANES), jnp.uint32),
        grid_spec=grid_spec,
        compiler_params=_cparams(("arbitrary",), 58),
        name="moe_experts",
    )(block_e, block_first, block_next, n_used, row_dst, h_packed, wg, wu, wd)


def _combine_kernel(x_ref, y0_ref, y1_ref, w_ref, mod_ref, o_ref):
    tm, d = x_ref.shape
    half = d // 2
    tr = half // LANES
    w0 = w_ref[:, 0:1]
    w1 = w_ref[:, 1:2]
    lo0, hi0 = _load_token_tiles(y0_ref, 0, tm, tr)
    lo1, hi1 = _load_token_tiles(y1_ref, 0, tm, tr)
    o_ref[:, 0:half] = x_ref[:, 0:half] + mod_ref[5:6, 0:half] * (w0 * lo0 + w1 * lo1)
    o_ref[:, half:] = x_ref[:, half:] + mod_ref[5:6, half:] * (w0 * hi0 + w1 * hi1)


def _combine(x2d, y, w_tok, mod, seq):
    t, d = x2d.shape
    tm = 512
    tr = d // 2 // LANES
    blocks_per_seq = seq // tm
    nblk = t // tm
    return pl.pallas_call(
        _combine_kernel,
        out_shape=jax.ShapeDtypeStruct((t, d), F32),
        grid=(nblk,),
        in_specs=[
            pl.BlockSpec((tm, d), lambda i: (i, 0)),
            pl.BlockSpec((tm * tr, LANES), lambda i: (i, 0)),
            pl.BlockSpec((tm * tr, LANES), lambda i: (nblk + i, 0)),
            pl.BlockSpec((tm, LANES), lambda i: (i, 0)),
            pl.BlockSpec((None, 6, d), lambda i: (i // blocks_per_seq, 0, 0)),
        ],
        out_specs=pl.BlockSpec((tm, d), lambda i: (i, 0)),
        compiler_params=_cparams(("arbitrary",), 48),
        name="combine",
    )(x2d, y, y, w_tok, mod)


def kernel(x, c, rel_bias, w_mod, b_mod, norm1_w, norm2_w, w_in, q_norm_w, k_norm_w, conv_w, gate_b, h_norm_w, w_out, router_group_w, router_group_b, router_expert_w, router_expert_b, w_gate, w_up, w_down):
    batch, seq, d = x.shape
    depth = w_mod.shape[0]
    t = batch * seq
    attn_width = d // 2
    mlstm_width = d - attn_width
    dv = mlstm_width // MLSTM_HEADS
    dk = dv // 2
    n_main = 3 * attn_width + 2 * MLSTM_HEADS * dk + 2 * mlstm_width
    assert dk == LANES and attn_width % LANES == 0 and seq % 1024 == 0 and t % MOE_ROWS == 0
    assert t & (t - 1) == 0, "the MoE row tables recover the token of assignment k * t + token with a bit mask"
    assert all((seq // dil) % ATTN_BLOCK == 0 and window // dil == ATTN_BLOCK for window, dil in DILATED_PATTERNS)

    mod_all = _modulation(c, w_mod, b_mod)
    bias_tab = _attn_bias_tables(rel_bias)
    x2d = x.reshape(t, d)

    for l in range(depth):
        mod = mod_all[l]
        w_main = w_in[l, :, :n_main].astype(BF16)
        w_gates = jnp.zeros((d, LANES), F32).at[:, :2 * MLSTM_HEADS].set(w_in[l, :, n_main:])
        proj, gates = _in_proj(x2d, mod, norm1_w[l].reshape(1, d), w_main, w_gates.astype(BF16), seq)

        qw2 = jnp.tile(q_norm_w[l], LANES // ATTN_HEAD_DIM).reshape(1, LANES)
        kw2 = jnp.tile(k_norm_w[l], LANES // ATTN_HEAD_DIM).reshape(1, LANES)
        attn = _attention(proj, qw2, kw2, bias_tab, batch, seq, attn_width)

        gate_b_row = jnp.zeros((1, LANES), F32).at[0, :2 * MLSTM_HEADS].set(gate_b[l])
        hm = _mlstm(proj, gates, conv_w[l], gate_b_row, h_norm_w[l].reshape(1, mlstm_width),
                    batch, seq, attn_width, dk, dv)

        w_r = jnp.concatenate([router_group_w[l], router_expert_w[l]], axis=1).T
        w_r = jnp.zeros((ROUTER_ROWS, d), F32).at[:w_r.shape[0]].set(w_r)
        wr_hi, wr_lo = _split_bf16(w_r)
        b_r = jnp.concatenate([router_group_b[l], router_expert_b[l]])
        b_r = jnp.zeros((ROUTER_ROWS, 1), F32).at[:b_r.shape[0], 0].set(b_r)
        x2d, h_packed, logits_t = _out_proj(
            attn.reshape(t, attn_width), hm.reshape(t, mlstm_width), w_out[l].astype(BF16),
            x2d, mod, norm2_w[l].reshape(1, d), wr_hi, wr_lo, b_r, seq)

        eidx, w_tok = _route(logits_t)
        block_e, block_first, block_next, n_used, row_dst = _dispatch_tables(eidx, t)
        y = _moe(block_e, block_first, block_next, n_used, row_dst, h_packed, w_gate, w_up, w_down, l)
        x2d = _combine(x2d, y, w_tok, mod, seq)

    return x2d.reshape(batch, seq, d)
```

```python
import functools

import jax
import jax.numpy as jnp
import numpy as np
from jax import lax
from jax.experimental import pallas as pl
from jax.experimental.pallas import tpu as pltpu

F32 = jnp.float32
BF16 = jnp.bfloat16

NORM_EPS = 1e-6
ATTN_HEAD_DIM = 64
ATTN_BLOCK = 128
ATTN_MAJOR = 16
LOG2E = 1.4426950408889634
DILATED_PATTERNS = ((128, 1), (512, 4), (2048, 16))
NUM_BUCKETS = 32
REL_MAX_DIST = 2048
MLSTM_HEADS = 4
MLSTM_CHUNK = 128
CONV_WIDTH = 4
MOE_GROUPS = 4
MOE_EXPERTS_PER_GROUP = 8
N_EXPERTS = MOE_GROUPS * MOE_EXPERTS_PER_GROUP
MOE_TOP_K = 2
LANES = 128
SUBLANES = 8
ROUTER_ROWS = 40
MOE_ROWS = 256
MOE_IDX_SLOTS = 4
MOE_TABLE_TAIL = 2
MASK_VALUE = -1e30
MiB = 1024 * 1024


def _cparams(sem, vmem_mib):
    return pltpu.CompilerParams(dimension_semantics=sem, vmem_limit_bytes=vmem_mib * MiB)


def _split_bf16(a):
    hi = a.astype(BF16)
    lo = (a - hi.astype(F32)).astype(BF16)
    return hi, lo


def _dot(a, b):
    return jnp.dot(a, b, preferred_element_type=F32)


def _dot_nt(a, b):
    return lax.dot_general(a, b, (((1,), (1,)), ((), ())), preferred_element_type=F32)


def _dot_tn(a, b):
    return lax.dot_general(a, b, (((0,), (0,)), ((), ())), preferred_element_type=F32)


def _dot3(a_f32, w_hi, w_lo):
    a_hi, a_lo = _split_bf16(a_f32)
    return _dot(a_hi, w_hi) + _dot(a_lo, w_hi) + _dot(a_hi, w_lo)


def _rms_modulate(x, norm_w, scale, shift):
    ms = jnp.mean(x * x, axis=-1, keepdims=True)
    return x * lax.rsqrt(ms + NORM_EPS) * norm_w * (1.0 + scale) + shift


def _mod_kernel(c_ref, w_ref, b_ref, o_ref):
    c = c_ref[...]
    a = c * jax.nn.sigmoid(c)
    w_hi, w_lo = _split_bf16(w_ref[...])
    o_ref[...] = _dot3(a, w_hi, w_lo) + b_ref[...]


def _modulation(c, w_mod, b_mod):
    depth, d, n = w_mod.shape
    b = c.shape[0]
    bp = -(-b // SUBLANES) * SUBLANES
    cp = jnp.zeros((bp, d), F32).at[:b].set(c)
    tn = 1024
    out = pl.pallas_call(
        _mod_kernel,
        out_shape=jax.ShapeDtypeStruct((depth, bp, n), F32),
        grid=(depth, n // tn),
        in_specs=[
            pl.BlockSpec((bp, d), lambda l, j: (0, 0)),
            pl.BlockSpec((None, d, tn), lambda l, j: (l, 0, j)),
            pl.BlockSpec((None, 1, tn), lambda l, j: (l, 0, j)),
        ],
        out_specs=pl.BlockSpec((None, bp, tn), lambda l, j: (l, 0, j)),
        compiler_params=_cparams(("arbitrary", "arbitrary"), 40),
        name="modulation",
    )(cp, w_mod, b_mod.reshape(depth, 1, n))
    return out[:, :b].reshape(depth, b, 6, d)


def _in_proj_kernel(x_ref, mod_ref, nw_ref, w_ref, wg_ref, o_ref, g_ref, h_scr):
    @pl.when(pl.program_id(1) == 0)
    def _():
        h = _rms_modulate(x_ref[...], nw_ref[...], mod_ref[1:2, :], mod_ref[0:1, :])
        h_scr[...] = h.astype(BF16)
        g_ref[...] = _dot(h_scr[...], wg_ref[...])

    o_ref[...] = _dot(h_scr[...], w_ref[...]).astype(BF16)


def _in_proj(x2d, mod, norm_w, w_main, w_gates, seq):
    t, d = x2d.shape
    n = w_main.shape[1]
    tm, tn = 1024, 512
    blocks_per_seq = seq // tm
    return pl.pallas_call(
        _in_proj_kernel,
        out_shape=(jax.ShapeDtypeStruct((t, n), BF16), jax.ShapeDtypeStruct((t, LANES), F32)),
        grid=(t // tm, n // tn),
        in_specs=[
            pl.BlockSpec((tm, d), lambda i, j: (i, 0)),
            pl.BlockSpec((None, 6, d), lambda i, j: (i // blocks_per_seq, 0, 0)),
            pl.BlockSpec((1, d), lambda i, j: (0, 0)),
            pl.BlockSpec((d, tn), lambda i, j: (0, j)),
            pl.BlockSpec((d, LANES), lambda i, j: (0, 0)),
        ],
        out_specs=(pl.BlockSpec((tm, tn), lambda i, j: (i, j)),
                   pl.BlockSpec((tm, LANES), lambda i, j: (i, 0))),
        scratch_shapes=[pltpu.VMEM((tm, d), BF16)],
        compiler_params=_cparams(("arbitrary", "arbitrary"), 48),
        name="in_proj",
    )(x2d, mod, norm_w, w_main, w_gates)


def _t5_causal_bucket(dist):
    max_exact = NUM_BUCKETS // 2
    d = np.maximum(dist, 1).astype(np.float32)
    large = max_exact + (np.log(d / max_exact) / np.log(REL_MAX_DIST / max_exact)
                         * (NUM_BUCKETS - max_exact)).astype(np.int32)
    large = np.minimum(large, NUM_BUCKETS - 1)
    return np.where(dist < max_exact, dist, large).astype(np.int32)


def _block_positions(dilation):
    g = ATTN_MAJOR // dilation
    plen = ATTN_BLOCK // g
    i = np.arange(ATTN_BLOCK)
    return g * (i % plen) + i // plen


def _attn_bias_tables(rel_bias):
    blk = ATTN_BLOCK
    tables = []
    for window, dilation in DILATED_PATTERNS:
        win = window // dilation
        loc = _block_positions(dilation)
        kpos = np.concatenate([loc, loc + blk])
        rel = loc[:, None] - kpos[None, :] + blk
        valid = (rel >= 0) & (rel <= win)
        bucket = _t5_causal_bucket(np.clip(rel, 0, win) * dilation)
        onehot = (jnp.asarray(bucket)[:, :, None] == jnp.arange(NUM_BUCKETS, dtype=jnp.int32)).astype(F32)
        bias = jnp.einsum('icb,bh->hic', onehot, rel_bias.astype(F32), precision=lax.Precision.HIGHEST) * LOG2E
        normal = jnp.where(valid[None], bias, MASK_VALUE)
        first = jnp.where((valid & (kpos >= blk)[None, :])[None], bias, MASK_VALUE)
        tables.append(jnp.stack([normal, first], axis=0))
    return jnp.stack(tables, axis=0)


def _attn_kernel(q_ref, k_ref, v_ref, qw_ref, kw_ref, bias_ref, o_ref,
                 stage, stage2, qs, ks, vs, acc_s, m_s, l_s, *, seq, unroll):
    blk = ATTN_BLOCK
    major = ATTN_MAJOR
    l16 = seq // major
    lane = lax.broadcasted_iota(jnp.int32, (1, LANES), 1)
    head0 = lane < ATTN_HEAD_DIM
    gi = lax.broadcasted_iota(jnp.int32, (LANES, LANES), 0) // ATTN_HEAD_DIM
    gj = lax.broadcasted_iota(jnp.int32, (LANES, LANES), 1) // ATTN_HEAD_DIM
    group_ones = (gi == gj).astype(BF16)
    ones_cols = jnp.ones((2 * blk, LANES), BF16)
    rows = 256

    quarter = seq // 4

    def to_major(dst):
        def pass1(cidx, carry):
            for r_lo in range(4):
                stage2[pl.ds(pl.multiple_of(r_lo * quarter + cidx * rows, rows), rows), :] = (
                    stage[pl.ds(r_lo + 4 * rows * cidx, rows, stride=4), :])
            return carry
        lax.fori_loop(0, quarter // rows, pass1, 0)

        def pass2(r, carry):
            src0 = jnp.bitwise_and(r, 3) * quarter + lax.shift_right_logical(r, 2)
            dst[pl.ds(pl.multiple_of(r * l16, l16), l16), :] = stage2[pl.ds(src0, l16, stride=4), :]
            return carry
        lax.fori_loop(0, major, pass2, 0)

    for src, dst, w_ref, scale in ((q_ref, qs, qw_ref, ATTN_HEAD_DIM ** -0.5 * LOG2E),
                                   (k_ref, ks, kw_ref, 1.0), (v_ref, vs, None, None)):
        def norm_body(c, carry, src=src, w_ref=w_ref, scale=scale):
            sl = pl.ds(pl.multiple_of(c * rows, rows), rows)
            xx = src[sl, :].astype(F32)
            if w_ref is not None:
                hi, lo = _split_bf16(xx * xx)
                ssq = _dot(hi, group_ones) + _dot(lo, group_ones)
                xx = xx * lax.rsqrt(ssq * (1.0 / ATTN_HEAD_DIM) + NORM_EPS) * w_ref[...] * scale
            stage[sl, :] = xx
            return carry
        lax.fori_loop(0, seq // rows, norm_body, 0, unroll=4)
        to_major(dst)

    for p, (_, dil) in enumerate(DILATED_PATTERNS):
        groups = major // dil
        plen = blk // groups
        nb_log2 = ((seq // dil) // blk).bit_length() - 1

        def starts(r_d, n, dil=dil, groups=groups, plen=plen):
            return [pl.multiple_of((j * dil + r_d) * l16 + n * plen, SUBLANES) for j in range(groups)]

        def load_blk(ref, st, plen=plen):
            return jnp.concatenate([ref[pl.ds(s, plen), :] for s in st], axis=0)

        def store_blk(ref, st, val, plen=plen):
            for j, s in enumerate(st):
                ref[pl.ds(s, plen), :] = val[j * plen:(j + 1) * plen]

        def body(it, carry, p=p, nb_log2=nb_log2, starts=starts, load_blk=load_blk, store_blk=store_blk):
            loaded = []
            for u in range(unroll):
                idx = it * unroll + u
                r_d = lax.shift_right_logical(idx, nb_log2)
                n = idx - lax.shift_left(r_d, nb_log2)
                cur = starts(r_d, n)
                prv = starts(r_d, jnp.maximum(n - 1, 0))
                first = (n == 0).astype(jnp.int32)
                qb = load_blk(qs, cur).astype(BF16)
                kw = jnp.concatenate([load_blk(ks, prv), load_blk(ks, cur)], axis=0).astype(BF16)
                vw = jnp.concatenate([load_blk(vs, prv), load_blk(vs, cur)], axis=0).astype(BF16)
                old = (load_blk(m_s, cur), load_blk(l_s, cur), load_blk(acc_s, cur)) if p > 0 else None
                loaded.append((cur, first, qb, kw, vw, old))
            results = []
            for cur, first, qb, kw, vw, old in loaded:
                vw_aug = jnp.concatenate([vw, ones_cols], axis=1)
                ms, res = [], []
                for h, hmask in enumerate((head0, jnp.logical_not(head0))):
                    qh = jnp.where(hmask, qb, jnp.zeros_like(qb))
                    s = _dot_nt(qh, kw) + bias_ref[p, first, h]
                    mh = jnp.max(s, axis=-1, keepdims=True)
                    ms.append(mh)
                    res.append(_dot(jnp.exp2(s - mh).astype(BF16), vw_aug))
                m_b = jnp.where(head0, ms[0], ms[1])
                a_b = jnp.where(head0, res[0][:, :LANES], res[1][:, :LANES])
                l_b = jnp.where(head0, res[0][:, LANES:], res[1][:, LANES:])
                if old is None:
                    results.append((cur, m_b, l_b, a_b))
                else:
                    m_o, l_o, a_o = old
                    m_new = jnp.maximum(m_o, m_b)
                    w_o = jnp.exp2(m_o - m_new)
                    w_b = jnp.exp2(m_b - m_new)
                    results.append((cur, m_new, w_o * l_o + w_b * l_b, w_o * a_o + w_b * a_b))
            for cur, m_new, l_new, a_new in results:
                store_blk(m_s, cur, m_new)
                store_blk(l_s, cur, l_new)
                store_blk(acc_s, cur, a_new)
            return carry

        lax.fori_loop(0, (seq // blk) // unroll, body, 0)

    def from_major(r, carry):
        sl = pl.ds(pl.multiple_of(r * l16, l16), l16)
        dst0 = jnp.bitwise_and(r, 3) * quarter + lax.shift_right_logical(r, 2)
        stage2[pl.ds(dst0, l16, stride=4), :] = acc_s[sl, :] / l_s[sl, :]
        return carry

    lax.fori_loop(0, major, from_major, 0)

    def to_natural(cidx, carry):
        for r_lo in range(4):
            stage[pl.ds(r_lo + 4 * rows * cidx, rows, stride=4), :] = (
                stage2[pl.ds(pl.multiple_of(r_lo * quarter + cidx * rows, rows), rows), :])
        return carry

    lax.fori_loop(0, quarter // rows, to_natural, 0)

    def out_body(c, carry):
        sl = pl.ds(pl.multiple_of(c * rows, rows), rows)
        o_ref[sl, :] = stage[sl, :].astype(o_ref.dtype)
        return carry

    lax.fori_loop(0, seq // rows, out_body, 0)


def _attention(proj, qw2, kw2, bias_tab, batch, seq, attn_width):
    t, n = proj.shape
    proj3 = proj.reshape(batch, seq, n)
    pairs = attn_width // LANES
    npat = len(DILATED_PATTERNS)
    kern = functools.partial(_attn_kernel, seq=seq, unroll=8)
    return pl.pallas_call(
        kern,
        out_shape=jax.ShapeDtypeStruct((batch, seq, attn_width), BF16),
        grid=(batch, pairs),
        in_specs=[
            pl.BlockSpec((None, seq, LANES), lambda b, h: (b, 0, h)),
            pl.BlockSpec((None, seq, LANES), lambda b, h: (b, 0, pairs + h)),
            pl.BlockSpec((None, seq, LANES), lambda b, h: (b, 0, 2 * pairs + h)),
            pl.BlockSpec((1, LANES), lambda b, h: (0, 0)),
            pl.BlockSpec((1, LANES), lambda b, h: (0, 0)),
            pl.BlockSpec((npat, 2, 2, ATTN_BLOCK, 2 * ATTN_BLOCK), lambda b, h: (0, 0, h, 0, 0)),
        ],
        out_specs=pl.BlockSpec((None, seq, LANES), lambda b, h: (b, 0, h)),
        scratch_shapes=[pltpu.VMEM((seq, LANES), F32) for _ in range(8)],
        compiler_params=_cparams(("arbitrary", "arbitrary"), 40),
        name="dilated_attention",
    )(proj3, proj3, proj3, qw2, kw2, bias_tab)


def _mlstm_kernel(q_ref, k_ref, v_ref, og_ref, g_ref, cw_ref, gb_ref, hw_ref, o_ref,
                  x_scr, c_scr, m_scr, *, rows, dk, dv):
    ch = MLSTM_CHUNK
    nheads = MLSTM_HEADS
    qk_w = nheads * dk
    aug = dv + LANES

    @pl.when(pl.program_id(1) == 0)
    def _():
        x_scr[0:SUBLANES, :] = jnp.zeros((SUBLANES, 2 * qk_w), F32)
        c_scr[...] = jnp.zeros_like(c_scr)
        m_scr[...] = jnp.zeros_like(m_scr)

    x_scr[SUBLANES:, 0:qk_w] = q_ref[...].astype(F32)
    x_scr[SUBLANES:, qk_w:] = k_ref[...].astype(F32)

    ri = lax.broadcasted_iota(jnp.int32, (ch, ch), 0)
    ci = lax.broadcasted_iota(jnp.int32, (ch, ch), 1)
    causal = ci <= ri
    tril = causal.astype(BF16)
    row8 = lax.broadcasted_iota(jnp.int32, (SUBLANES, 1), 0)
    ones_col = (lax.broadcasted_iota(jnp.int32, (ch, LANES), 1) == 0).astype(F32)
    k_scale = dk ** -0.5

    def chunk_body(c, carry):
        r0 = pl.multiple_of(c * ch, ch)
        prev8 = x_scr[pl.ds(r0, SUBLANES), :]
        cur = x_scr[pl.ds(r0 + SUBLANES, ch), :]
        conv = cur * cw_ref[CONV_WIDTH - 1:CONV_WIDTH, :]
        for sh in range(1, CONV_WIDTH):
            rolled = pltpu.roll(cur, sh, 0)
            head_rows = jnp.where(row8 < sh, pltpu.roll(prev8, sh, 0), rolled[0:SUBLANES])
            shifted = jnp.concatenate([head_rows, rolled[SUBLANES:]], axis=0)
            conv = conv + shifted * cw_ref[CONV_WIDTH - 1 - sh:CONV_WIDTH - sh, :]
        qk = conv * jax.nn.sigmoid(conv)

        gates = g_ref[pl.ds(r0, ch), :] + gb_ref[...]
        logf = jax.nn.log_sigmoid(gates)
        lf_hi, lf_lo = _split_bf16(logf)
        bcum = _dot(tril, lf_hi) + _dot(tril, lf_lo)
        gates_t = gates.T
        bcum_t = bcum.T

        for h in range(nheads):
            q = qk[:, h * dk:(h + 1) * dk]
            k = qk[:, qk_w + h * dk:qk_w + (h + 1) * dk] * k_scale
            v = v_ref[pl.ds(r0, ch), h * dv:(h + 1) * dv].astype(F32)
            v_aug = jnp.concatenate([v, ones_col], axis=1)
            i_col = gates[:, h:h + 1]
            b_col = bcum[:, nheads + h:nheads + h + 1]
            i_row = gates_t[h:h + 1, :]
            b_row = bcum_t[nheads + h:nheads + h + 1, :]
            m_prev = m_scr[h, 0:1, 0:1]
            c_prev = c_scr[h]

            dlog = jnp.where(causal, b_col - b_row + i_row, -jnp.inf)
            inter = b_col + m_prev
            m_t = jnp.maximum(inter, jnp.max(dlog, axis=-1, keepdims=True))
            qb = q.astype(BF16)
            kb = k.astype(BF16)
            sm = _dot_nt(qb, kb) * jnp.exp(dlog - m_t)
            e_inter = jnp.exp(inter - m_t)
            tot = e_inter * _dot(qb, c_prev.astype(BF16)) + _dot(sm.astype(BF16), v_aug.astype(BF16))
            num = tot[:, 0:dv]
            den = tot[:, dv:dv + 1]
            hcell = num / jnp.maximum(jnp.abs(den), jnp.exp(-m_t))

            g_last = b_col[ch - 1:ch, :]
            a_col = g_last - b_col + i_col
            m_new = jnp.maximum(g_last + m_prev, jnp.max(a_col, axis=0, keepdims=True))
            decay = jnp.exp(g_last + m_prev - m_new)
            w_col = jnp.exp(a_col - m_new)
            c_scr[h] = decay * c_prev + _dot_tn(kb, (w_col * v_aug).astype(BF16))
            m_scr[h] = jnp.broadcast_to(m_new, (SUBLANES, LANES))

            ms = jnp.mean(hcell * hcell, axis=-1, keepdims=True)
            hn = hcell * lax.rsqrt(ms + NORM_EPS) * hw_ref[:, h * dv:(h + 1) * dv]
            og = og_ref[pl.ds(r0, ch), h * dv:(h + 1) * dv].astype(F32)
            o_ref[pl.ds(r0, ch), h * dv:(h + 1) * dv] = (jax.nn.sigmoid(og) * hn).astype(o_ref.dtype)
        return carry

    lax.fori_loop(0, rows // ch, chunk_body, 0, unroll=2)
    x_scr[0:SUBLANES, :] = x_scr[rows:rows + SUBLANES, :]


def _mlstm(proj, gates, conv_w, gate_b_row, h_norm_w, batch, seq, attn_width, dk, dv):
    t, n = proj.shape
    proj3 = proj.reshape(batch, seq, n)
    gates3 = gates.reshape(batch, seq, LANES)
    nheads = MLSTM_HEADS
    qk_w = nheads * dk
    v_w = nheads * dv
    rows = 512
    q_blk = (3 * attn_width) // qk_w
    v_blk = (3 * attn_width + 2 * qk_w) // v_w
    kern = functools.partial(_mlstm_kernel, rows=rows, dk=dk, dv=dv)
    return pl.pallas_call(
        kern,
        out_shape=jax.ShapeDtypeStruct((batch, seq, v_w), BF16),
        grid=(batch, seq // rows),
        in_specs=[
            pl.BlockSpec((None, rows, qk_w), lambda b, j: (b, j, q_blk)),
            pl.BlockSpec((None, rows, qk_w), lambda b, j: (b, j, q_blk + 1)),
            pl.BlockSpec((None, rows, v_w), lambda b, j: (b, j, v_blk)),
            pl.BlockSpec((None, rows, v_w), lambda b, j: (b, j, v_blk + 1)),
            pl.BlockSpec((None, rows, LANES), lambda b, j: (b, j, 0)),
            pl.BlockSpec((CONV_WIDTH, 2 * qk_w), lambda b, j: (0, 0)),
            pl.BlockSpec((1, LANES), lambda b, j: (0, 0)),
            pl.BlockSpec((1, v_w), lambda b, j: (0, 0)),
        ],
        out_specs=pl.BlockSpec((None, rows, v_w), lambda b, j: (b, j, 0)),
        scratch_shapes=[
            pltpu.VMEM((rows + SUBLANES, 2 * qk_w), F32),
            pltpu.VMEM((nheads, dk, dv + LANES), F32),
            pltpu.VMEM((nheads, SUBLANES, LANES), F32),
        ],
        compiler_params=_cparams(("arbitrary", "arbitrary"), 40),
        name="mlstm",
    )(proj3, proj3, proj3, proj3, gates3, conv_w, gate_b_row, h_norm_w)


def _store_token_tiles(ref, base, h):
    m, w2 = h.shape
    w = w2 // 2
    tile_rows = w // LANES
    u = pltpu.bitcast(h.astype(BF16).astype(F32), jnp.uint32)
    packed = (u[:, :w] >> 16) | (u[:, w:] & jnp.uint32(0xFFFF0000))
    for k in range(tile_rows):
        ref[pl.ds(base + k, m, stride=tile_rows), :] = packed[:, k * LANES:(k + 1) * LANES]


def _load_token_tiles(ref, base, m, tile_rows):
    lo, hi = [], []
    for k in range(tile_rows):
        p = ref[pl.ds(base + k, m, stride=tile_rows), :]
        lo.append(pltpu.bitcast(p << 16, F32))
        hi.append(pltpu.bitcast(p & jnp.uint32(0xFFFF0000), F32))
    return jnp.concatenate(lo, axis=1), jnp.concatenate(hi, axis=1)


def _out_proj_kernel(a_ref, m_ref, w_ref, x_ref, mod_ref, nw_ref, wrh_ref, wrl_ref, br_ref,
                     xo_ref, hp_ref, lg_ref, *, attn_width, sub_rows):
    tm, d = x_ref.shape
    tile_rows = d // 2 // LANES
    half = d // 2
    col = 4 * LANES
    for s in range(tm // sub_rows):
        rs = slice(s * sub_rows, (s + 1) * sub_rows)
        a_blk, m_blk = a_ref[rs, :], m_ref[rs, :]
        ssq = jnp.zeros((sub_rows, 1), F32)
        for c in range(d // col):
            cs = slice(c * col, (c + 1) * col)
            mix = _dot(a_blk, w_ref[0:attn_width, cs]) + _dot(m_blk, w_ref[attn_width:, cs])
            xn = x_ref[rs, cs] + mod_ref[2:3, cs] * mix
            xo_ref[rs, cs] = xn
            ssq = ssq + jnp.sum(xn * xn, axis=-1, keepdims=True)
        rstd = lax.rsqrt(ssq * (1.0 / d) + NORM_EPS)
        logits = br_ref[...]
        for k in range(tile_rows):
            parts = []
            for lo in (k * LANES, half + k * LANES):
                ks = slice(lo, lo + LANES)
                hk = xo_ref[rs, ks] * rstd * nw_ref[:, ks] * (1.0 + mod_ref[4:5, ks]) + mod_ref[3:4, ks]
                h_hi, h_lo = _split_bf16(hk)
                logits = logits + (_dot_nt(wrh_ref[:, ks], h_hi) + _dot_nt(wrh_ref[:, ks], h_lo)
                                   + _dot_nt(wrl_ref[:, ks], h_hi))
                parts.append(pltpu.bitcast(h_hi.astype(F32), jnp.uint32))
            packed = (parts[0] >> 16) | (parts[1] & jnp.uint32(0xFFFF0000))
            hp_ref[pl.ds(s * sub_rows * tile_rows + k, sub_rows, stride=tile_rows), :] = packed
        lg_ref[:, rs] = logits


def _out_proj(attn2d, hm2d, w_out, x2d, mod, norm_w, wr_hi, wr_lo, br, seq):
    t, d = x2d.shape
    aw = attn2d.shape[1]
    mw = hm2d.shape[1]
    tm = 512
    blocks_per_seq = seq // tm
    tile_rows = d // 2 // LANES
    kern = functools.partial(_out_proj_kernel, attn_width=aw, sub_rows=512)
    return pl.pallas_call(
        kern,
        out_shape=(jax.ShapeDtypeStruct((t, d), F32),
                   jax.ShapeDtypeStruct((t * tile_rows, LANES), jnp.uint32),
                   jax.ShapeDtypeStruct((ROUTER_ROWS, t), F32)),
        grid=(t // tm,),
        in_specs=[
            pl.BlockSpec((tm, aw), lambda i: (i, 0)),
            pl.BlockSpec((tm, mw), lambda i: (i, 0)),
            pl.BlockSpec((aw + mw, d), lambda i: (0, 0)),
            pl.BlockSpec((tm, d), lambda i: (i, 0)),
            pl.BlockSpec((None, 6, d), lambda i: (i // blocks_per_seq, 0, 0)),
            pl.BlockSpec((1, d), lambda i: (0, 0)),
            pl.BlockSpec((ROUTER_ROWS, d), lambda i: (0, 0)),
            pl.BlockSpec((ROUTER_ROWS, d), lambda i: (0, 0)),
            pl.BlockSpec((ROUTER_ROWS, 1), lambda i: (0, 0)),
        ],
        out_specs=(pl.BlockSpec((tm, d), lambda i: (i, 0)),
                   pl.BlockSpec((tm * tile_rows, LANES), lambda i: (i, 0)),
                   pl.BlockSpec((ROUTER_ROWS, tm), lambda i: (0, i))),
        compiler_params=_cparams(("arbitrary",), 48),
        name="out_proj",
    )(attn2d, hm2d, w_out, x2d, mod, norm_w, wr_hi, wr_lo, br)


def _route_kernel(lg_ref, idx_ref, w_ref):
    ng, ne = MOE_GROUPS, MOE_EXPERTS_PER_GROUP
    lg = [lg_ref[g:g + 1, :] for g in range(ng)]
    best, gsel = lg[0], jnp.zeros_like(lg[0], dtype=jnp.int32)
    for g in range(1, ng):
        better = lg[g] > best
        best = jnp.where(better, lg[g], best)
        gsel = jnp.where(better, g, gsel)
    denom = jnp.exp(lg[0] - best)
    for g in range(1, ng):
        denom = denom + jnp.exp(lg[g] - best)
    p_group = 1.0 / denom

    le = []
    for e in range(ne):
        v = lg_ref[ng + e:ng + e + 1, :]
        for g in range(1, ng):
            v = jnp.where(gsel == g, lg_ref[ng + g * ne + e:ng + g * ne + e + 1, :], v)
        le.append(v)
    v1, i1 = le[0], jnp.zeros_like(gsel)
    for e in range(1, ne):
        better = le[e] > v1
        v1 = jnp.where(better, le[e], v1)
        i1 = jnp.where(better, e, i1)
    v2 = jnp.full_like(v1, -jnp.inf)
    i2 = jnp.zeros_like(gsel)
    for e in range(ne):
        better = jnp.logical_and(le[e] > v2, i1 != e)
        v2 = jnp.where(better, le[e], v2)
        i2 = jnp.where(better, e, i2)
    e2 = jnp.exp(v2 - v1)
    inv = 1.0 / (1.0 + e2)
    tn = gsel.shape[1]
    zero_i = jnp.zeros((SUBLANES - MOE_TOP_K, tn), jnp.int32)
    idx_ref[...] = jnp.concatenate([gsel * ne + i1, gsel * ne + i2, zero_i], axis=0)
    w_rows = jnp.concatenate([p_group * inv, p_group * (e2 * inv), jnp.zeros((LANES - MOE_TOP_K, tn), F32)], axis=0)
    for c in range(tn // LANES):
        w_ref[c * LANES:(c + 1) * LANES, :] = w_rows[:, c * LANES:(c + 1) * LANES].T


def _route(logits_t):
    rows, t = logits_t.shape
    tn = 2048
    return pl.pallas_call(
        _route_kernel,
        out_shape=(jax.ShapeDtypeStruct((SUBLANES, t), jnp.int32),
                   jax.ShapeDtypeStruct((t, LANES), F32)),
        grid=(t // tn,),
        in_specs=[pl.BlockSpec((rows, tn), lambda i: (0, i))],
        out_specs=(pl.BlockSpec((SUBLANES, tn), lambda i: (0, i)),
                   pl.BlockSpec((tn, LANES), lambda i: (i, 0))),
        compiler_params=_cparams(("arbitrary",), 32),
        name="route",
    )(logits_t)


def _dispatch_tables(eidx, t):
    m = MOE_ROWS
    a_total = MOE_TOP_K * t
    n_blocks = a_total // m + N_EXPERTS
    n_rows = n_blocks * m
    src_bits = (a_total - 1).bit_length()
    experts = jnp.arange(N_EXPERTS, dtype=jnp.int32)
    e_flat = eidx[:MOE_TOP_K].reshape(a_total)
    counts = jnp.sum((e_flat[:, None] == experts[None, :]).astype(jnp.int32), axis=0)
    padded = (counts + m - 1) // m * m
    pad_end = jnp.cumsum(padded)
    pad_start = pad_end - padded
    start = jnp.cumsum(counts) - counts
    n_used = (pad_end[-1] // m).astype(jnp.int32)
    packed = lax.sort(e_flat * (1 << src_bits) + jnp.arange(a_total, dtype=jnp.int32))
    order = packed & ((1 << src_bits) - 1)

    def lookup(idx, table):
        return jnp.sum(jnp.where(idx[:, None] == experts[None, :], table[None, :], 0), axis=1)

    blk = jnp.arange(n_blocks, dtype=jnp.int32)
    block_e = jnp.minimum(jnp.sum((blk[:, None] * m >= pad_end[None, :]).astype(jnp.int32), axis=1), N_EXPERTS - 1)
    block_off = blk * m - lookup(block_e, pad_start)
    block_valid = jnp.clip(lookup(block_e, counts) - block_off, 0, m)
    block_valid = jnp.where(blk < n_used, block_valid, 0)
    sorted_pos = (lookup(block_e, start) + block_off)[:, None] + jnp.arange(m, dtype=jnp.int32)[None, :]
    valid = jnp.arange(m, dtype=jnp.int32)[None, :] < block_valid[:, None]
    all_blk = jnp.arange(-1, n_blocks + MOE_TABLE_TAIL + 1, dtype=jnp.int32)
    all_blk = all_blk.at[-1].set(0)
    spare = a_total + jnp.bitwise_and(all_blk, 1)[:, None] * m + jnp.arange(m, dtype=jnp.int32)[None, :]
    row_dst = jnp.where(valid, order[jnp.clip(sorted_pos, 0, a_total - 1)], spare[1:n_blocks + 1])
    row_dst = jnp.concatenate([spare[:1], row_dst, spare[n_blocks + 1:]], axis=0)
    block_e = jnp.where(blk < n_used, block_e, jnp.max(jnp.where(blk < n_used, block_e, 0)))
    block_first = jnp.concatenate([jnp.ones((1,), jnp.int32), (block_e[1:] != block_e[:-1]).astype(jnp.int32)])
    used_e = jnp.where(blk < n_used, block_e, N_EXPERTS)
    later = jnp.where(used_e[None, :] > block_e[:, None], used_e[None, :], N_EXPERTS)
    block_next = jnp.min(later, axis=1)
    block_next = jnp.where(block_next < N_EXPERTS, block_next, -1).astype(jnp.int32)
    return block_e, block_first, block_next, n_used.reshape(1), row_dst


def _moe_kernel(be_ref, first_ref, next_ref, nu_ref, idx_hbm, h_hbm, wg_hbm, wu_hbm, wd_hbm, out_hbm,
                idx_s, xbuf, ybuf, wg_f, wu_f, wd_f, wgb, wub, wdb, sem_i, sem_g, sem_s, sem_w,
                *, tokens, layer, ff_chunk):
    m = MOE_ROWS
    i = pl.program_id(0)
    n_used = nu_ref[0]
    par = lax.rem(i, 2)
    other = 1 - par
    ff = wgb.shape[1]
    half = wgb.shape[0] // 2
    tr = half // LANES
    slot_rows = m * tr
    n_chunks = ff // ff_chunk
    active = i < n_used

    def weight_copies(expert):
        return [pltpu.make_async_copy(src.at[layer, expert], dst, sem_w.at[k])
                for k, (src, dst) in enumerate(((wg_hbm, wg_f), (wu_hbm, wu_f), (wd_hbm, wd_f)))]

    def tile_rows_at(row):
        return pl.ds(pl.multiple_of(row * tr, tr), tr)

    def idx_copy(table_row):
        s = jnp.bitwise_and(table_row, MOE_IDX_SLOTS - 1)
        return pltpu.make_async_copy(idx_hbm.at[table_row], idx_s.at[s], sem_i.at[s])

    def gather_row(buf_half, table_row, j):
        tok = jnp.bitwise_and(idx_s[jnp.bitwise_and(table_row, MOE_IDX_SLOTS - 1), j], tokens - 1)
        pltpu.make_async_copy(h_hbm.at[tile_rows_at(tok)], xbuf.at[tile_rows_at(buf_half * m + j)],
                              sem_g.at[buf_half]).start()

    def scatter_row(buf_half, table_row, j):
        dst = idx_s[jnp.bitwise_and(table_row, MOE_IDX_SLOTS - 1), j]
        pltpu.make_async_copy(ybuf.at[tile_rows_at(buf_half * m + j)], out_hbm.at[tile_rows_at(dst)],
                              sem_s.at[buf_half]).start()

    def half_rows(buf_half):
        return pl.ds(pl.multiple_of(buf_half * slot_rows, slot_rows), slot_rows)

    def wait_gather(buf_half):
        pltpu.make_async_copy(h_hbm.at[pl.ds(0, slot_rows)], xbuf.at[half_rows(buf_half)], sem_g.at[buf_half]).wait()

    def wait_scatter(buf_half):
        pltpu.make_async_copy(ybuf.at[half_rows(buf_half)], out_hbm.at[pl.ds(0, slot_rows)], sem_s.at[buf_half]).wait()

    base = pl.multiple_of(par * slot_rows, slot_rows)

    @pl.when(i == 0)
    def _():
        ybuf[...] = jnp.zeros_like(ybuf)
        for cp in weight_copies(be_ref[0]):
            cp.start(priority=1)
        last_row = idx_hbm.shape[0] - 1
        idx_copy(last_row).start()
        idx_copy(last_row).wait()

        def spare_body(j, carry):
            scatter_row(0, last_row, j)
            return carry
        lax.fori_loop(0, m, spare_body, 0, unroll=8)
        for row in range(3):
            idx_copy(row).start()
        for row in range(3):
            idx_copy(row).wait()
        idx_copy(3).start()

        def body(j, carry):
            gather_row(0, 1, j)
            return carry
        lax.fori_loop(0, m, body, 0, unroll=8)

    @pl.when(jnp.logical_and(active, first_ref[i] == 1))
    def _():
        for cp in weight_copies(be_ref[i]):
            cp.wait()
        rows = 256
        for src, dst in ((wg_f, wgb), (wu_f, wub), (wd_f, wdb)):
            def cast_body(r, carry, src=src, dst=dst):
                sl = pl.ds(pl.multiple_of(r * rows, rows), rows)
                dst[sl, :] = src[sl, :].astype(BF16)
                return carry
            lax.fori_loop(0, src.shape[0] // rows, cast_body, 0)

        @pl.when(next_ref[i] >= 0)
        def _():
            for cp in weight_copies(next_ref[i]):
                cp.start(priority=1)

    @pl.when(jnp.logical_and(active, i >= 1))
    def _():
        idx_copy(i + 2).wait()
        idx_copy(i + 3).start()

    @pl.when(active)
    def _():
        wait_gather(par)
        for j in range(m):
            scatter_row(other, i, j)
        x_lo, x_hi = _load_token_tiles(xbuf, base, m, tr)
        x_lo, x_hi = x_lo.astype(BF16), x_hi.astype(BF16)
        gather_dots = (5 * n_chunks) // 2
        issued = 0

        def dot_with_rows(dot_index, a, b):
            nonlocal issued
            upto = min(m, (m * (dot_index + 1)) // gather_dots)
            for j in range(issued, upto):
                gather_row(other, i + 2, j)
            issued = upto
            return _dot(a, b)

        for c in range(n_chunks):
            cs = slice(c * ff_chunk, (c + 1) * ff_chunk)
            k = 5 * c
            g = dot_with_rows(k, x_lo, wgb[0:half, cs]) + dot_with_rows(k + 1, x_hi, wgb[half:, cs])
            u = dot_with_rows(k + 2, x_lo, wub[0:half, cs]) + dot_with_rows(k + 3, x_hi, wub[half:, cs])
            act = (g * jax.nn.sigmoid(g) * u).astype(BF16)
            part = dot_with_rows(k + 4, act, wdb[cs, :])
            acc = part if c == 0 else acc + part
        wait_scatter(par)
        _store_token_tiles(ybuf, base, acc)

        @pl.when(i == n_used - 1)
        def _():
            wait_scatter(other)

            def body(j, carry):
                scatter_row(par, i + 1, j)
                return carry
            lax.fori_loop(0, m, body, 0, unroll=8)
            wait_scatter(par)
            wait_gather(other)
            idx_copy(i + 3).wait()


def _moe(block_e, block_first, block_next, n_used, row_dst, h_packed, wg, wu, wd, layer):
    d, ff = wg.shape[2], wg.shape[3]
    tr = d // 2 // LANES
    t = h_packed.shape[0] // tr
    m = MOE_ROWS
    n_blocks = block_e.shape[0]
    kern = functools.partial(_moe_kernel, tokens=t, layer=layer, ff_chunk=256)
    grid_spec = pltpu.PrefetchScalarGridSpec(
        num_scalar_prefetch=4,
        grid=(n_blocks,),
        in_specs=[pl.BlockSpec(memory_space=pl.ANY) for _ in range(5)],
        out_specs=pl.BlockSpec(memory_space=pl.ANY),
        scratch_shapes=[
            pltpu.SMEM((MOE_IDX_SLOTS, m), jnp.int32),
            pltpu.VMEM((2 * m * tr, LANES), jnp.uint32),
            pltpu.VMEM((2 * m * tr, LANES), jnp.uint32),
            pltpu.VMEM((d, ff), F32),
            pltpu.VMEM((d, ff), F32),
            pltpu.VMEM((ff, d), F32),
            pltpu.VMEM((d, ff), BF16),
            pltpu.VMEM((d, ff), BF16),
            pltpu.VMEM((ff, d), BF16),
            pltpu.SemaphoreType.DMA((MOE_IDX_SLOTS,)),
            pltpu.SemaphoreType.DMA((2,)),
            pltpu.SemaphoreType.DMA((2,)),
            pltpu.SemaphoreType.DMA((3,)),
        ],
    )
    return pl.pallas_call(
        kern,
        out_shape=jax.ShapeDtypeStruct(((MOE_TOP_K * t + 2 * m) * tr, LANES), jnp.uint32),
        grid_spec=grid_spec,
        compiler_params=_cparams(("arbitrary",), 58),
        name="moe_experts",
    )(block_e, block_first, block_next, n_used, row_dst, h_packed, wg, wu, wd)


def _combine_kernel(x_ref, y0_ref, y1_ref, w_ref, mod_ref, o_ref):
    tm, d = x_ref.shape
    half = d // 2
    tr = half // LANES
    w0 = w_ref[:, 0:1]
    w1 = w_ref[:, 1:2]
    lo0, hi0 = _load_token_tiles(y0_ref, 0, tm, tr)
    lo1, hi1 = _load_token_tiles(y1_ref, 0, tm, tr)
    o_ref[:, 0:half] = x_ref[:, 0:half] + mod_ref[5:6, 0:half] * (w0 * lo0 + w1 * lo1)
    o_ref[:, half:] = x_ref[:, half:] + mod_ref[5:6, half:] * (w0 * hi0 + w1 * hi1)


def _combine(x2d, y, w_tok, mod, seq):
    t, d = x2d.shape
    tm = 512
    tr = d // 2 // LANES
    blocks_per_seq = seq // tm
    nblk = t // tm
    return pl.pallas_call(
        _combine_kernel,
        out_shape=jax.ShapeDtypeStruct((t, d), F32),
        grid=(nblk,),
        in_specs=[
            pl.BlockSpec((tm, d), lambda i: (i, 0)),
            pl.BlockSpec((tm * tr, LANES), lambda i: (i, 0)),
            pl.BlockSpec((tm * tr, LANES), lambda i: (nblk + i, 0)),
            pl.BlockSpec((tm, LANES), lambda i: (i, 0)),
            pl.BlockSpec((None, 6, d), lambda i: (i // blocks_per_seq, 0, 0)),
        ],
        out_specs=pl.BlockSpec((tm, d), lambda i: (i, 0)),
        compiler_params=_cparams(("arbitrary",), 48),
        name="combine",
    )(x2d, y, y, w_tok, mod)


def kernel(x, c, rel_bias, w_mod, b_mod, norm1_w, norm2_w, w_in, q_norm_w, k_norm_w, conv_w, gate_b, h_norm_w, w_out, router_group_w, router_group_b, router_expert_w, router_expert_b, w_gate, w_up, w_down):
    batch, seq, d = x.shape
    depth = w_mod.shape[0]
    t = batch * seq
    attn_width = d // 2
    mlstm_width = d - attn_width
    dv = mlstm_width // MLSTM_HEADS
    dk = dv // 2
    n_main = 3 * attn_width + 2 * MLSTM_HEADS * dk + 2 * mlstm_width
    assert dk == LANES and attn_width % LANES == 0 and seq % 1024 == 0 and t % MOE_ROWS == 0
    assert t & (t - 1) == 0, "the MoE row tables recover the token of assignment k * t + token with a bit mask"
    assert all((seq // dil) % ATTN_BLOCK == 0 and window // dil == ATTN_BLOCK for window, dil in DILATED_PATTERNS)

    mod_all = _modulation(c, w_mod, b_mod)
    bias_tab = _attn_bias_tables(rel_bias)
    x2d = x.reshape(t, d)

    for l in range(depth):
        mod = mod_all[l]
        w_main = w_in[l, :, :n_main].astype(BF16)
        w_gates = jnp.zeros((d, LANES), F32).at[:, :2 * MLSTM_HEADS].set(w_in[l, :, n_main:])
        proj, gates = _in_proj(x2d, mod, norm1_w[l].reshape(1, d), w_main, w_gates.astype(BF16), seq)

        qw2 = jnp.tile(q_norm_w[l], LANES // ATTN_HEAD_DIM).reshape(1, LANES)
        kw2 = jnp.tile(k_norm_w[l], LANES // ATTN_HEAD_DIM).reshape(1, LANES)
        attn = _attention(proj, qw2, kw2, bias_tab, batch, seq, attn_width)

        gate_b_row = jnp.zeros((1, LANES), F32).at[0, :2 * MLSTM_HEADS].set(gate_b[l])
        hm = _mlstm(proj, gates, conv_w[l], gate_b_row, h_norm_w[l].reshape(1, mlstm_width),
                    batch, seq, attn_width, dk, dv)

        w_r = jnp.concatenate([router_group_w[l], router_expert_w[l]], axis=1).T
        w_r = jnp.zeros((ROUTER_ROWS, d), F32).at[:w_r.shape[0]].set(w_r)
        wr_hi, wr_lo = _split_bf16(w_r)
        b_r = jnp.concatenate([router_group_b[l], router_expert_b[l]])
        b_r = jnp.zeros((ROUTER_ROWS, 1), F32).at[:b_r.shape[0], 0].set(b_r)
        x2d, h_packed, logits_t = _out_proj(
            attn.reshape(t, attn_width), hm.reshape(t, mlstm_width), w_out[l].astype(BF16),
            x2d, mod, norm2_w[l].reshape(1, d), wr_hi, wr_lo, b_r, seq)

        eidx, w_tok = _route(logits_t)
        block_e, block_first, block_next, n_used, row_dst = _dispatch_tables(eidx, t)
        y = _moe(block_e, block_first, block_next, n_used, row_dst, h_packed, w_gate, w_up, w_down, l)
        x2d = _combine(x2d, y, w_tok, mod, seq)

    return x2d.reshape(batch, seq, d)
```

```python
import functools

import jax
import jax.numpy as jnp
import numpy as np
from jax import lax
from jax.experimental import pallas as pl
from jax.experimental.pallas import tpu as pltpu

F32 = jnp.float32
BF16 = jnp.bfloat16

NORM_EPS = 1e-6
ATTN_HEAD_DIM = 64
ATTN_BLOCK = 128
ATTN_MAJOR = 16
LOG2E = 1.4426950408889634
DILATED_PATTERNS = ((128, 1), (512, 4), (2048, 16))
NUM_BUCKETS = 32
REL_MAX_DIST = 2048
MLSTM_HEADS = 4
MLSTM_CHUNK = 128
CONV_WIDTH = 4
MOE_GROUPS = 4
MOE_EXPERTS_PER_GROUP = 8
N_EXPERTS = MOE_GROUPS * MOE_EXPERTS_PER_GROUP
MOE_TOP_K = 2
LANES = 128
SUBLANES = 8
ROUTER_ROWS = 40
MOE_ROWS = 256
MOE_IDX_SLOTS = 4
MOE_TABLE_TAIL = 2
MASK_VALUE = -1e30
MiB = 1024 * 1024


def _cparams(sem, vmem_mib):
    return pltpu.CompilerParams(dimension_semantics=sem, vmem_limit_bytes=vmem_mib * MiB)


def _split_bf16(a):
    hi = a.astype(BF16)
    lo = (a - hi.astype(F32)).astype(BF16)
    return hi, lo


def _dot(a, b):
    return jnp.dot(a, b, preferred_element_type=F32)


def _dot_nt(a, b):
    return lax.dot_general(a, b, (((1,), (1,)), ((), ())), preferred_element_type=F32)


def _dot_tn(a, b):
    return lax.dot_general(a, b, (((0,), (0,)), ((), ())), preferred_element_type=F32)


def _dot3(a_f32, w_hi, w_lo):
    a_hi, a_lo = _split_bf16(a_f32)
    return _dot(a_hi, w_hi) + _dot(a_lo, w_hi) + _dot(a_hi, w_lo)


def _rms_modulate(x, norm_w, scale, shift):
    ms = jnp.mean(x * x, axis=-1, keepdims=True)
    return x * lax.rsqrt(ms + NORM_EPS) * norm_w * (1.0 + scale) + shift


def _mod_kernel(c_ref, w_ref, b_ref, o_ref):
    c = c_ref[...]
    a = c * jax.nn.sigmoid(c)
    w_hi, w_lo = _split_bf16(w_ref[...])
    o_ref[...] = _dot3(a, w_hi, w_lo) + b_ref[...]


def _modulation(c, w_mod, b_mod):
    depth, d, n = w_mod.shape
    b = c.shape[0]
    bp = -(-b // SUBLANES) * SUBLANES
    cp = jnp.zeros((bp, d), F32).at[:b].set(c)
    tn = 1024
    out = pl.pallas_call(
        _mod_kernel,
        out_shape=jax.ShapeDtypeStruct((depth, bp, n), F32),
        grid=(depth, n // tn),
        in_specs=[
            pl.BlockSpec((bp, d), lambda l, j: (0, 0)),
            pl.BlockSpec((None, d, tn), lambda l, j: (l, 0, j)),
            pl.BlockSpec((None, 1, tn), lambda l, j: (l, 0, j)),
        ],
        out_specs=pl.BlockSpec((None, bp, tn), lambda l, j: (l, 0, j)),
        compiler_params=_cparams(("arbitrary", "arbitrary"), 40),
        name="modulation",
    )(cp, w_mod, b_mod.reshape(depth, 1, n))
    return out[:, :b].reshape(depth, b, 6, d)


def _in_proj_kernel(x_ref, mod_ref, nw_ref, w_ref, wg_ref, o_ref, g_ref, h_scr):
    @pl.when(pl.program_id(1) == 0)
    def _():
        h = _rms_modulate(x_ref[...], nw_ref[...], mod_ref[1:2, :], mod_ref[0:1, :])
        h_scr[...] = h.astype(BF16)
        g_ref[...] = _dot(h_scr[...], wg_ref[...])

    o_ref[...] = _dot(h_scr[...], w_ref[...]).astype(BF16)


def _in_proj(x2d, mod, norm_w, w_main, w_gates, seq):
    t, d = x2d.shape
    n = w_main.shape[1]
    tm, tn = 1024, 1024
    blocks_per_seq = seq // tm
    return pl.pallas_call(
        _in_proj_kernel,
        out_shape=(jax.ShapeDtypeStruct((t, n), BF16), jax.ShapeDtypeStruct((t, LANES), F32)),
        grid=(t // tm, n // tn),
        in_specs=[
            pl.BlockSpec((tm, d), lambda i, j: (i, 0)),
            pl.BlockSpec((None, 6, d), lambda i, j: (i // blocks_per_seq, 0, 0)),
            pl.BlockSpec((1, d), lambda i, j: (0, 0)),
            pl.BlockSpec((d, tn), lambda i, j: (0, j)),
            pl.BlockSpec((d, LANES), lambda i, j: (0, 0)),
        ],
        out_specs=(pl.BlockSpec((tm, tn), lambda i, j: (i, j)),
                   pl.BlockSpec((tm, LANES), lambda i, j: (i, 0))),
        scratch_shapes=[pltpu.VMEM((tm, d), BF16)],
        compiler_params=_cparams(("arbitrary", "arbitrary"), 48),
        name="in_proj",
    )(x2d, mod, norm_w, w_main, w_gates)


def _t5_causal_bucket(dist):
    max_exact = NUM_BUCKETS // 2
    d = np.maximum(dist, 1).astype(np.float32)
    large = max_exact + (np.log(d / max_exact) / np.log(REL_MAX_DIST / max_exact)
                         * (NUM_BUCKETS - max_exact)).astype(np.int32)
    large = np.minimum(large, NUM_BUCKETS - 1)
    return np.where(dist < max_exact, dist, large).astype(np.int32)


def _block_positions(dilation):
    g = ATTN_MAJOR // dilation
    plen = ATTN_BLOCK // g
    i = np.arange(ATTN_BLOCK)
    return g * (i % plen) + i // plen


def _attn_bias_tables(rel_bias):
    blk = ATTN_BLOCK
    tables = []
    for window, dilation in DILATED_PATTERNS:
        win = window // dilation
        loc = _block_positions(dilation)
        kpos = np.concatenate([loc, loc + blk])
        rel = loc[:, None] - kpos[None, :] + blk
        valid = (rel >= 0) & (rel <= win)
        bucket = _t5_causal_bucket(np.clip(rel, 0, win) * dilation)
        onehot = (jnp.asarray(bucket)[:, :, None] == jnp.arange(NUM_BUCKETS, dtype=jnp.int32)).astype(F32)
        bias = jnp.einsum('icb,bh->hic', onehot, rel_bias.astype(F32), precision=lax.Precision.HIGHEST) * LOG2E
        normal = jnp.where(valid[None], bias, MASK_VALUE)
        first = jnp.where((valid & (kpos >= blk)[None, :])[None], bias, MASK_VALUE)
        tables.append(jnp.stack([normal, first], axis=0))
    return jnp.stack(tables, axis=0)


def _attn_kernel(q_ref, k_ref, v_ref, qw_ref, kw_ref, bias_ref, o_ref,
                 stage, stage2, qs, ks, vs, acc_s, m_s, l_s, *, seq, unroll):
    blk = ATTN_BLOCK
    major = ATTN_MAJOR
    l16 = seq // major
    lane = lax.broadcasted_iota(jnp.int32, (1, LANES), 1)
    head0 = lane < ATTN_HEAD_DIM
    gi = lax.broadcasted_iota(jnp.int32, (LANES, LANES), 0) // ATTN_HEAD_DIM
    gj = lax.broadcasted_iota(jnp.int32, (LANES, LANES), 1) // ATTN_HEAD_DIM
    group_ones = (gi == gj).astype(BF16)
    ones_cols = jnp.ones((2 * blk, LANES), BF16)
    rows = 256

    quarter = seq // 4

    def to_major(dst):
        def pass1(cidx, carry):
            for r_lo in range(4):
                stage2[pl.ds(pl.multiple_of(r_lo * quarter + cidx * rows, rows), rows), :] = (
                    stage[pl.ds(r_lo + 4 * rows * cidx, rows, stride=4), :])
            return carry
        lax.fori_loop(0, quarter // rows, pass1, 0)

        def pass2(r, carry):
            src0 = jnp.bitwise_and(r, 3) * quarter + lax.shift_right_logical(r, 2)
            dst[pl.ds(pl.multiple_of(r * l16, l16), l16), :] = stage2[pl.ds(src0, l16, stride=4), :]
            return carry
        lax.fori_loop(0, major, pass2, 0)

    for src, dst, w_ref, scale in ((q_ref, qs, qw_ref, ATTN_HEAD_DIM ** -0.5 * LOG2E),
                                   (k_ref, ks, kw_ref, 1.0), (v_ref, vs, None, None)):
        def norm_body(c, carry, src=src, w_ref=w_ref, scale=scale):
            sl = pl.ds(pl.multiple_of(c * rows, rows), rows)
            xx = src[sl, :].astype(F32)
            if w_ref is not None:
                hi, lo = _split_bf16(xx * xx)
                ssq = _dot(hi, group_ones) + _dot(lo, group_ones)
                xx = xx * lax.rsqrt(ssq * (1.0 / ATTN_HEAD_DIM) + NORM_EPS) * w_ref[...] * scale
            stage[sl, :] = xx
            return carry
        lax.fori_loop(0, seq // rows, norm_body, 0, unroll=4)
        to_major(dst)

    for p, (_, dil) in enumerate(DILATED_PATTERNS):
        groups = major // dil
        plen = blk // groups
        nb_log2 = ((seq // dil) // blk).bit_length() - 1

        def starts(r_d, n, dil=dil, groups=groups, plen=plen):
            return [pl.multiple_of((j * dil + r_d) * l16 + n * plen, SUBLANES) for j in range(groups)]

        def load_blk(ref, st, plen=plen):
            return jnp.concatenate([ref[pl.ds(s, plen), :] for s in st], axis=0)

        def store_blk(ref, st, val, plen=plen):
            for j, s in enumerate(st):
                ref[pl.ds(s, plen), :] = val[j * plen:(j + 1) * plen]

        def body(it, carry, p=p, nb_log2=nb_log2, starts=starts, load_blk=load_blk, store_blk=store_blk):
            loaded = []
            for u in range(unroll):
                idx = it * unroll + u
                r_d = lax.shift_right_logical(idx, nb_log2)
                n = idx - lax.shift_left(r_d, nb_log2)
                cur = starts(r_d, n)
                prv = starts(r_d, jnp.maximum(n - 1, 0))
                first = (n == 0).astype(jnp.int32)
                qb = load_blk(qs, cur).astype(BF16)
                kw = jnp.concatenate([load_blk(ks, prv), load_blk(ks, cur)], axis=0).astype(BF16)
                vw = jnp.concatenate([load_blk(vs, prv), load_blk(vs, cur)], axis=0).astype(BF16)
                old = (load_blk(m_s, cur), load_blk(l_s, cur), load_blk(acc_s, cur)) if p > 0 else None
                loaded.append((cur, first, qb, kw, vw, old))
            results = []
            for cur, first, qb, kw, vw, old in loaded:
                vw_aug = jnp.concatenate([vw, ones_cols], axis=1)
                ms, res = [], []
                for h, hmask in enumerate((head0, jnp.logical_not(head0))):
                    qh = jnp.where(hmask, qb, jnp.zeros_like(qb))
                    s = _dot_nt(qh, kw) + bias_ref[p, first, h]
                    mh = jnp.max(s, axis=-1, keepdims=True)
                    ms.append(mh)
                    res.append(_dot(jnp.exp2(s - mh).astype(BF16), vw_aug))
                m_b = jnp.where(head0, ms[0], ms[1])
                a_b = jnp.where(head0, res[0][:, :LANES], res[1][:, :LANES])
                l_b = jnp.where(head0, res[0][:, LANES:], res[1][:, LANES:])
                if old is None:
                    results.append((cur, m_b, l_b, a_b))
                else:
                    m_o, l_o, a_o = old
                    m_new = jnp.maximum(m_o, m_b)
                    w_o = jnp.exp2(m_o - m_new)
                    w_b = jnp.exp2(m_b - m_new)
                    results.append((cur, m_new, w_o * l_o + w_b * l_b, w_o * a_o + w_b * a_b))
            for cur, m_new, l_new, a_new in results:
                store_blk(m_s, cur, m_new)
                store_blk(l_s, cur, l_new)
                store_blk(acc_s, cur, a_new)
            return carry

        lax.fori_loop(0, (seq // blk) // unroll, body, 0)

    def from_major(r, carry):
        sl = pl.ds(pl.multiple_of(r * l16, l16), l16)
        dst0 = jnp.bitwise_and(r, 3) * quarter + lax.shift_right_logical(r, 2)
        stage2[pl.ds(dst0, l16, stride=4), :] = acc_s[sl, :] / l_s[sl, :]
        return carry

    lax.fori_loop(0, major, from_major, 0)

    def to_natural(cidx, carry):
        for r_lo in range(4):
            stage[pl.ds(r_lo + 4 * rows * cidx, rows, stride=4), :] = (
                stage2[pl.ds(pl.multiple_of(r_lo * quarter + cidx * rows, rows), rows), :])
        return carry

    lax.fori_loop(0, quarter // rows, to_natural, 0)

    def out_body(c, carry):
        sl = pl.ds(pl.multiple_of(c * rows, rows), rows)
        o_ref[sl, :] = stage[sl, :].astype(o_ref.dtype)
        return carry

    lax.fori_loop(0, seq // rows, out_body, 0)


def _attention(proj, qw2, kw2, bias_tab, batch, seq, attn_width):
    t, n = proj.shape
    proj3 = proj.reshape(batch, seq, n)
    pairs = attn_width // LANES
    npat = len(DILATED_PATTERNS)
    kern = functools.partial(_attn_kernel, seq=seq, unroll=8)
    return pl.pallas_call(
        kern,
        out_shape=jax.ShapeDtypeStruct((batch, seq, attn_width), BF16),
        grid=(batch, pairs),
        in_specs=[
            pl.BlockSpec((None, seq, LANES), lambda b, h: (b, 0, h)),
            pl.BlockSpec((None, seq, LANES), lambda b, h: (b, 0, pairs + h)),
            pl.BlockSpec((None, seq, LANES), lambda b, h: (b, 0, 2 * pairs + h)),
            pl.BlockSpec((1, LANES), lambda b, h: (0, 0)),
            pl.BlockSpec((1, LANES), lambda b, h: (0, 0)),
            pl.BlockSpec((npat, 2, 2, ATTN_BLOCK, 2 * ATTN_BLOCK), lambda b, h: (0, 0, h, 0, 0)),
        ],
        out_specs=pl.BlockSpec((None, seq, LANES), lambda b, h: (b, 0, h)),
        scratch_shapes=[pltpu.VMEM((seq, LANES), F32) for _ in range(8)],
        compiler_params=_cparams(("arbitrary", "arbitrary"), 40),
        name="dilated_attention",
    )(proj3, proj3, proj3, qw2, kw2, bias_tab)


def _mlstm_kernel(q_ref, k_ref, v_ref, og_ref, g_ref, cw_ref, gb_ref, hw_ref, o_ref,
                  x_scr, c_scr, m_scr, *, rows, dk, dv):
    ch = MLSTM_CHUNK
    nheads = MLSTM_HEADS
    qk_w = nheads * dk
    aug = dv + LANES

    @pl.when(pl.program_id(1) == 0)
    def _():
        x_scr[0:SUBLANES, :] = jnp.zeros((SUBLANES, 2 * qk_w), F32)
        c_scr[...] = jnp.zeros_like(c_scr)
        m_scr[...] = jnp.zeros_like(m_scr)

    x_scr[SUBLANES:, 0:qk_w] = q_ref[...].astype(F32)
    x_scr[SUBLANES:, qk_w:] = k_ref[...].astype(F32)

    ri = lax.broadcasted_iota(jnp.int32, (ch, ch), 0)
    ci = lax.broadcasted_iota(jnp.int32, (ch, ch), 1)
    causal = ci <= ri
    tril = causal.astype(BF16)
    row8 = lax.broadcasted_iota(jnp.int32, (SUBLANES, 1), 0)
    ones_col = (lax.broadcasted_iota(jnp.int32, (ch, LANES), 1) == 0).astype(F32)
    k_scale = dk ** -0.5

    def chunk_body(c, carry):
        r0 = pl.multiple_of(c * ch, ch)
        prev8 = x_scr[pl.ds(r0, SUBLANES), :]
        cur = x_scr[pl.ds(r0 + SUBLANES, ch), :]
        conv = cur * cw_ref[CONV_WIDTH - 1:CONV_WIDTH, :]
        for sh in range(1, CONV_WIDTH):
            rolled = pltpu.roll(cur, sh, 0)
            head_rows = jnp.where(row8 < sh, pltpu.roll(prev8, sh, 0), rolled[0:SUBLANES])
            shifted = jnp.concatenate([head_rows, rolled[SUBLANES:]], axis=0)
            conv = conv + shifted * cw_ref[CONV_WIDTH - 1 - sh:CONV_WIDTH - sh, :]
        qk = conv * jax.nn.sigmoid(conv)

        gates = g_ref[pl.ds(r0, ch), :] + gb_ref[...]
        logf = jax.nn.log_sigmoid(gates)
        lf_hi, lf_lo = _split_bf16(logf)
        bcum = _dot(tril, lf_hi) + _dot(tril, lf_lo)
        gates_t = gates.T
        bcum_t = bcum.T

        for h in range(nheads):
            q = qk[:, h * dk:(h + 1) * dk]
            k = qk[:, qk_w + h * dk:qk_w + (h + 1) * dk] * k_scale
            v = v_ref[pl.ds(r0, ch), h * dv:(h + 1) * dv].astype(F32)
            v_aug = jnp.concatenate([v, ones_col], axis=1)
            i_col = gates[:, h:h + 1]
            b_col = bcum[:, nheads + h:nheads + h + 1]
            i_row = gates_t[h:h + 1, :]
            b_row = bcum_t[nheads + h:nheads + h + 1, :]
            m_prev = m_scr[h, 0:1, 0:1]
            c_prev = c_scr[h]

            dlog = jnp.where(causal, b_col - b_row + i_row, -jnp.inf)
            inter = b_col + m_prev
            m_t = jnp.maximum(inter, jnp.max(dlog, axis=-1, keepdims=True))
            qb = q.astype(BF16)
            kb = k.astype(BF16)
            sm = _dot_nt(qb, kb) * jnp.exp(dlog - m_t)
            e_inter = jnp.exp(inter - m_t)
            tot = e_inter * _dot(qb, c_prev.astype(BF16)) + _dot(sm.astype(BF16), v_aug.astype(BF16))
            num = tot[:, 0:dv]
            den = tot[:, dv:dv + 1]
            hcell = num / jnp.maximum(jnp.abs(den), jnp.exp(-m_t))

            g_last = b_col[ch - 1:ch, :]
            a_col = g_last - b_col + i_col
            m_new = jnp.maximum(g_last + m_prev, jnp.max(a_col, axis=0, keepdims=True))
            decay = jnp.exp(g_last + m_prev - m_new)
            w_col = jnp.exp(a_col - m_new)
            c_scr[h] = decay * c_prev + _dot_tn(kb, (w_col * v_aug).astype(BF16))
            m_scr[h] = jnp.broadcast_to(m_new, (SUBLANES, LANES))

            ms = jnp.mean(hcell * hcell, axis=-1, keepdims=True)
            hn = hcell * lax.rsqrt(ms + NORM_EPS) * hw_ref[:, h * dv:(h + 1) * dv]
            og = og_ref[pl.ds(r0, ch), h * dv:(h + 1) * dv].astype(F32)
            o_ref[pl.ds(r0, ch), h * dv:(h + 1) * dv] = (jax.nn.sigmoid(og) * hn).astype(o_ref.dtype)
        return carry

    lax.fori_loop(0, rows // ch, chunk_body, 0, unroll=2)
    x_scr[0:SUBLANES, :] = x_scr[rows:rows + SUBLANES, :]


def _mlstm(proj, gates, conv_w, gate_b_row, h_norm_w, batch, seq, attn_width, dk, dv):
    t, n = proj.shape
    proj3 = proj.reshape(batch, seq, n)
    gates3 = gates.reshape(batch, seq, LANES)
    nheads = MLSTM_HEADS
    qk_w = nheads * dk
    v_w = nheads * dv
    rows = 512
    q_blk = (3 * attn_width) // qk_w
    v_blk = (3 * attn_width + 2 * qk_w) // v_w
    kern = functools.partial(_mlstm_kernel, rows=rows, dk=dk, dv=dv)
    return pl.pallas_call(
        kern,
        out_shape=jax.ShapeDtypeStruct((batch, seq, v_w), BF16),
        grid=(batch, seq // rows),
        in_specs=[
            pl.BlockSpec((None, rows, qk_w), lambda b, j: (b, j, q_blk)),
            pl.BlockSpec((None, rows, qk_w), lambda b, j: (b, j, q_blk + 1)),
            pl.BlockSpec((None, rows, v_w), lambda b, j: (b, j, v_blk)),
            pl.BlockSpec((None, rows, v_w), lambda b, j: (b, j, v_blk + 1)),
            pl.BlockSpec((None, rows, LANES), lambda b, j: (b, j, 0)),
            pl.BlockSpec((CONV_WIDTH, 2 * qk_w), lambda b, j: (0, 0)),
            pl.BlockSpec((1, LANES), lambda b, j: (0, 0)),
            pl.BlockSpec((1, v_w), lambda b, j: (0, 0)),
        ],
        out_specs=pl.BlockSpec((None, rows, v_w), lambda b, j: (b, j, 0)),
        scratch_shapes=[
            pltpu.VMEM((rows + SUBLANES, 2 * qk_w), F32),
            pltpu.VMEM((nheads, dk, dv + LANES), F32),
            pltpu.VMEM((nheads, SUBLANES, LANES), F32),
        ],
        compiler_params=_cparams(("arbitrary", "arbitrary"), 40),
        name="mlstm",
    )(proj3, proj3, proj3, proj3, gates3, conv_w, gate_b_row, h_norm_w)


def _store_token_tiles(ref, base, h):
    m, w2 = h.shape
    w = w2 // 2
    tile_rows = w // LANES
    u = pltpu.bitcast(h.astype(BF16).astype(F32), jnp.uint32)
    packed = (u[:, :w] >> 16) | (u[:, w:] & jnp.uint32(0xFFFF0000))
    for k in range(tile_rows):
        ref[pl.ds(base + k, m, stride=tile_rows), :] = packed[:, k * LANES:(k + 1) * LANES]


def _load_token_tiles(ref, base, m, tile_rows):
    lo, hi = [], []
    for k in range(tile_rows):
        p = ref[pl.ds(base + k, m, stride=tile_rows), :]
        lo.append(pltpu.bitcast(p << 16, F32))
        hi.append(pltpu.bitcast(p & jnp.uint32(0xFFFF0000), F32))
    return jnp.concatenate(lo, axis=1), jnp.concatenate(hi, axis=1)


def _out_proj_kernel(a_ref, m_ref, w_ref, x_ref, mod_ref, nw_ref, wrh_ref, wrl_ref, br_ref,
                     xo_ref, hp_ref, lg_ref, *, attn_width, sub_rows):
    tm, d = x_ref.shape
    tile_rows = d // 2 // LANES
    half = d // 2
    col = 4 * LANES
    for s in range(tm // sub_rows):
        rs = slice(s * sub_rows, (s + 1) * sub_rows)
        a_blk, m_blk = a_ref[rs, :], m_ref[rs, :]
        ssq = jnp.zeros((sub_rows, 1), F32)
        for c in range(d // col):
            cs = slice(c * col, (c + 1) * col)
            mix = _dot(a_blk, w_ref[0:attn_width, cs]) + _dot(m_blk, w_ref[attn_width:, cs])
            xn = x_ref[rs, cs] + mod_ref[2:3, cs] * mix
            xo_ref[rs, cs] = xn
            ssq = ssq + jnp.sum(xn * xn, axis=-1, keepdims=True)
        rstd = lax.rsqrt(ssq * (1.0 / d) + NORM_EPS)
        logits = br_ref[...]
        for k in range(tile_rows):
            parts = []
            for lo in (k * LANES, half + k * LANES):
                ks = slice(lo, lo + LANES)
                hk = xo_ref[rs, ks] * rstd * nw_ref[:, ks] * (1.0 + mod_ref[4:5, ks]) + mod_ref[3:4, ks]
                h_hi, h_lo = _split_bf16(hk)
                logits = logits + (_dot_nt(wrh_ref[:, ks], h_hi) + _dot_nt(wrh_ref[:, ks], h_lo)
                                   + _dot_nt(wrl_ref[:, ks], h_hi))
                parts.append(pltpu.bitcast(h_hi.astype(F32), jnp.uint32))
            packed = (parts[0] >> 16) | (parts[1] & jnp.uint32(0xFFFF0000))
            hp_ref[pl.ds(s * sub_rows * tile_rows + k, sub_rows, stride=tile_rows), :] = packed
        lg_ref[:, rs] = logits


def _out_proj(attn2d, hm2d, w_out, x2d, mod, norm_w, wr_hi, wr_lo, br, seq):
    t, d = x2d.shape
    aw = attn2d.shape[1]
    mw = hm2d.shape[1]
    tm = 512
    blocks_per_seq = seq // tm
    tile_rows = d // 2 // LANES
    kern = functools.partial(_out_proj_kernel, attn_width=aw, sub_rows=512)
    return pl.pallas_call(
        kern,
        out_shape=(jax.ShapeDtypeStruct((t, d), F32),
                   jax.ShapeDtypeStruct((t * tile_rows, LANES), jnp.uint32),
                   jax.ShapeDtypeStruct((ROUTER_ROWS, t), F32)),
        grid=(t // tm,),
        in_specs=[
            pl.BlockSpec((tm, aw), lambda i: (i, 0)),
            pl.BlockSpec((tm, mw), lambda i: (i, 0)),
            pl.BlockSpec((aw + mw, d), lambda i: (0, 0)),
            pl.BlockSpec((tm, d), lambda i: (i, 0)),
            pl.BlockSpec((None, 6, d), lambda i: (i // blocks_per_seq, 0, 0)),
            pl.BlockSpec((1, d), lambda i: (0, 0)),
            pl.BlockSpec((ROUTER_ROWS, d), lambda i: (0, 0)),
            pl.BlockSpec((ROUTER_ROWS, d), lambda i: (0, 0)),
            pl.BlockSpec((ROUTER_ROWS, 1), lambda i: (0, 0)),
        ],
        out_specs=(pl.BlockSpec((tm, d), lambda i: (i, 0)),
                   pl.BlockSpec((tm * tile_rows, LANES), lambda i: (i, 0)),
                   pl.BlockSpec((ROUTER_ROWS, tm), lambda i: (0, i))),
        compiler_params=_cparams(("arbitrary",), 48),
        name="out_proj",
    )(attn2d, hm2d, w_out, x2d, mod, norm_w, wr_hi, wr_lo, br)


def _route_kernel(lg_ref, idx_ref, w_ref):
    ng, ne = MOE_GROUPS, MOE_EXPERTS_PER_GROUP
    lg = [lg_ref[g:g + 1, :] for g in range(ng)]
    best, gsel = lg[0], jnp.zeros_like(lg[0], dtype=jnp.int32)
    for g in range(1, ng):
        better = lg[g] > best
        best = jnp.where(better, lg[g], best)
        gsel = jnp.where(better, g, gsel)
    denom = jnp.exp(lg[0] - best)
    for g in range(1, ng):
        denom = denom + jnp.exp(lg[g] - best)
    p_group = 1.0 / denom

    le = []
    for e in range(ne):
        v = lg_ref[ng + e:ng + e + 1, :]
        for g in range(1, ng):
            v = jnp.where(gsel == g, lg_ref[ng + g * ne + e:ng + g * ne + e + 1, :], v)
        le.append(v)
    v1, i1 = le[0], jnp.zeros_like(gsel)
    for e in range(1, ne):
        better = le[e] > v1
        v1 = jnp.where(better, le[e], v1)
        i1 = jnp.where(better, e, i1)
    v2 = jnp.full_like(v1, -jnp.inf)
    i2 = jnp.zeros_like(gsel)
    for e in range(ne):
        better = jnp.logical_and(le[e] > v2, i1 != e)
        v2 = jnp.where(better, le[e], v2)
        i2 = jnp.where(better, e, i2)
    e2 = jnp.exp(v2 - v1)
    inv = 1.0 / (1.0 + e2)
    tn = gsel.shape[1]
    zero_i = jnp.zeros((SUBLANES - MOE_TOP_K, tn), jnp.int32)
    idx_ref[...] = jnp.concatenate([gsel * ne + i1, gsel * ne + i2, zero_i], axis=0)
    w_rows = jnp.concatenate([p_group * inv, p_group * (e2 * inv), jnp.zeros((LANES - MOE_TOP_K, tn), F32)], axis=0)
    for c in range(tn // LANES):
        w_ref[c * LANES:(c + 1) * LANES, :] = w_rows[:, c * LANES:(c + 1) * LANES].T


def _route(logits_t):
    rows, t = logits_t.shape
    tn = 2048
    return pl.pallas_call(
        _route_kernel,
        out_shape=(jax.ShapeDtypeStruct((SUBLANES, t), jnp.int32),
                   jax.ShapeDtypeStruct((t, LANES), F32)),
        grid=(t // tn,),
        in_specs=[pl.BlockSpec((rows, tn), lambda i: (0, i))],
        out_specs=(pl.BlockSpec((SUBLANES, tn), lambda i: (0, i)),
                   pl.BlockSpec((tn, LANES), lambda i: (i, 0))),
        compiler_params=_cparams(("arbitrary",), 32),
        name="route",
    )(logits_t)


def _dispatch_tables(eidx, t):
    m = MOE_ROWS
    a_total = MOE_TOP_K * t
    n_blocks = a_total // m + N_EXPERTS
    n_rows = n_blocks * m
    src_bits = (a_total - 1).bit_length()
    experts = jnp.arange(N_EXPERTS, dtype=jnp.int32)
    e_flat = eidx[:MOE_TOP_K].reshape(a_total)
    counts = jnp.sum((e_flat[:, None] == experts[None, :]).astype(jnp.int32), axis=0)
    padded = (counts + m - 1) // m * m
    pad_end = jnp.cumsum(padded)
    pad_start = pad_end - padded
    start = jnp.cumsum(counts) - counts
    n_used = (pad_end[-1] // m).astype(jnp.int32)
    packed = lax.sort(e_flat * (1 << src_bits) + jnp.arange(a_total, dtype=jnp.int32))
    order = packed & ((1 << src_bits) - 1)

    def lookup(idx, table):
        return jnp.sum(jnp.where(idx[:, None] == experts[None, :], table[None, :], 0), axis=1)

    blk = jnp.arange(n_blocks, dtype=jnp.int32)
    block_e = jnp.minimum(jnp.sum((blk[:, None] * m >= pad_end[None, :]).astype(jnp.int32), axis=1), N_EXPERTS - 1)
    block_off = blk * m - lookup(block_e, pad_start)
    block_valid = jnp.clip(lookup(block_e, counts) - block_off, 0, m)
    block_valid = jnp.where(blk < n_used, block_valid, 0)
    sorted_pos = (lookup(block_e, start) + block_off)[:, None] + jnp.arange(m, dtype=jnp.int32)[None, :]
    valid = jnp.arange(m, dtype=jnp.int32)[None, :] < block_valid[:, None]
    all_blk = jnp.arange(-1, n_blocks + MOE_TABLE_TAIL + 1, dtype=jnp.int32)
    all_blk = all_blk.at[-1].set(0)
    spare = a_total + jnp.bitwise_and(all_blk, 1)[:, None] * m + jnp.arange(m, dtype=jnp.int32)[None, :]
    row_dst = jnp.where(valid, order[jnp.clip(sorted_pos, 0, a_total - 1)], spare[1:n_blocks + 1])
    row_dst = jnp.concatenate([spare[:1], row_dst, spare[n_blocks + 1:]], axis=0)
    block_e = jnp.where(blk < n_used, block_e, jnp.max(jnp.where(blk < n_used, block_e, 0)))
    block_first = jnp.concatenate([jnp.ones((1,), jnp.int32), (block_e[1:] != block_e[:-1]).astype(jnp.int32)])
    used_e = jnp.where(blk < n_used, block_e, N_EXPERTS)
    later = jnp.where(used_e[None, :] > block_e[:, None], used_e[None, :], N_EXPERTS)
    block_next = jnp.min(later, axis=1)
    block_next = jnp.where(block_next < N_EXPERTS, block_next, -1).astype(jnp.int32)
    return block_e, block_first, block_next, n_used.reshape(1), row_dst


def _moe_kernel(be_ref, first_ref, next_ref, nu_ref, idx_hbm, h_hbm, wg_hbm, wu_hbm, wd_hbm, out_hbm,
                idx_s, xbuf, ybuf, wg_f, wu_f, wd_f, wgb, wub, wdb, sem_i, sem_g, sem_s, sem_w,
                *, tokens, layer, ff_chunk):
    m = MOE_ROWS
    i = pl.program_id(0)
    n_used = nu_ref[0]
    par = lax.rem(i, 2)
    other = 1 - par
    ff = wgb.shape[1]
    half = wgb.shape[0] // 2
    tr = half // LANES
    slot_rows = m * tr
    n_chunks = ff // ff_chunk
    active = i < n_used

    def weight_copies(expert):
        return [pltpu.make_async_copy(src.at[layer, expert], dst, sem_w.at[k])
                for k, (src, dst) in enumerate(((wg_hbm, wg_f), (wu_hbm, wu_f), (wd_hbm, wd_f)))]

    def tile_rows_at(row):
        return pl.ds(pl.multiple_of(row * tr, tr), tr)

    def idx_copy(table_row):
        s = jnp.bitwise_and(table_row, MOE_IDX_SLOTS - 1)
        return pltpu.make_async_copy(idx_hbm.at[table_row], idx_s.at[s], sem_i.at[s])

    def gather_row(buf_half, table_row, j):
        tok = jnp.bitwise_and(idx_s[jnp.bitwise_and(table_row, MOE_IDX_SLOTS - 1), j], tokens - 1)
        pltpu.make_async_copy(h_hbm.at[tile_rows_at(tok)], xbuf.at[tile_rows_at(buf_half * m + j)],
                              sem_g.at[buf_half]).start()

    def scatter_row(buf_half, table_row, j):
        dst = idx_s[jnp.bitwise_and(table_row, MOE_IDX_SLOTS - 1), j]
        pltpu.make_async_copy(ybuf.at[tile_rows_at(buf_half * m + j)], out_hbm.at[tile_rows_at(dst)],
                              sem_s.at[buf_half]).start()

    def half_rows(buf_half):
        return pl.ds(pl.multiple_of(buf_half * slot_rows, slot_rows), slot_rows)

    def wait_gather(buf_half):
        pltpu.make_async_copy(h_hbm.at[pl.ds(0, slot_rows)], xbuf.at[half_rows(buf_half)], sem_g.at[buf_half]).wait()

    def wait_scatter(buf_half):
        pltpu.make_async_copy(ybuf.at[half_rows(buf_half)], out_hbm.at[pl.ds(0, slot_rows)], sem_s.at[buf_half]).wait()

    base = pl.multiple_of(par * slot_rows, slot_rows)

    @pl.when(i == 0)
    def _():
        ybuf[...] = jnp.zeros_like(ybuf)
        for cp in weight_copies(be_ref[0]):
            cp.start(priority=1)
        last_row = idx_hbm.shape[0] - 1
        idx_copy(last_row).start()
        idx_copy(last_row).wait()

        def spare_body(j, carry):
            scatter_row(0, last_row, j)
            return carry
        lax.fori_loop(0, m, spare_body, 0, unroll=8)
        for row in range(3):
            idx_copy(row).start()
        for row in range(3):
            idx_copy(row).wait()
        idx_copy(3).start()

        def body(j, carry):
            gather_row(0, 1, j)
            return carry
        lax.fori_loop(0, m, body, 0, unroll=8)

    @pl.when(jnp.logical_and(active, first_ref[i] == 1))
    def _():
        for cp in weight_copies(be_ref[i]):
            cp.wait()
        rows = 256
        for src, dst in ((wg_f, wgb), (wu_f, wub), (wd_f, wdb)):
            def cast_body(r, carry, src=src, dst=dst):
                sl = pl.ds(pl.multiple_of(r * rows, rows), rows)
                dst[sl, :] = src[sl, :].astype(BF16)
                return carry
            lax.fori_loop(0, src.shape[0] // rows, cast_body, 0)

        @pl.when(next_ref[i] >= 0)
        def _():
            for cp in weight_copies(next_ref[i]):
                cp.start(priority=1)

    @pl.when(jnp.logical_and(active, i >= 1))
    def _():
        idx_copy(i + 2).wait()
        idx_copy(i + 3).start()

    @pl.when(active)
    def _():
        wait_gather(par)
        for j in range(m):
            scatter_row(other, i, j)
        x_lo, x_hi = _load_token_tiles(xbuf, base, m, tr)
        x_lo, x_hi = x_lo.astype(BF16), x_hi.astype(BF16)
        gather_dots = (5 * n_chunks) // 2
        issued = 0

        def dot_with_rows(dot_index, a, b):
            nonlocal issued
            upto = min(m, (m * (dot_index + 1)) // gather_dots)
            for j in range(issued, upto):
                gather_row(other, i + 2, j)
            issued = upto
            return _dot(a, b)

        for c in range(n_chunks):
            cs = slice(c * ff_chunk, (c + 1) * ff_chunk)
            k = 5 * c
            g = dot_with_rows(k, x_lo, wgb[0:half, cs]) + dot_with_rows(k + 1, x_hi, wgb[half:, cs])
            u = dot_with_rows(k + 2, x_lo, wub[0:half, cs]) + dot_with_rows(k + 3, x_hi, wub[half:, cs])
            act = (g * jax.nn.sigmoid(g) * u).astype(BF16)
            part = dot_with_rows(k + 4, act, wdb[cs, :])
            acc = part if c == 0 else acc + part
        wait_scatter(par)
        _store_token_tiles(ybuf, base, acc)

        @pl.when(i == n_used - 1)
        def _():
            wait_scatter(other)

            def body(j, carry):
                scatter_row(par, i + 1, j)
                return carry
            lax.fori_loop(0, m, body, 0, unroll=8)
            wait_scatter(par)
            wait_gather(other)
            idx_copy(i + 3).wait()


def _moe(block_e, block_first, block_next, n_used, row_dst, h_packed, wg, wu, wd, layer):
    d, ff = wg.shape[2], wg.shape[3]
    tr = d // 2 // LANES
    t = h_packed.shape[0] // tr
    m = MOE_ROWS
    n_blocks = block_e.shape[0]
    kern = functools.partial(_moe_kernel, tokens=t, layer=layer, ff_chunk=256)
    grid_spec = pltpu.PrefetchScalarGridSpec(
        num_scalar_prefetch=4,
        grid=(n_blocks,),
        in_specs=[pl.BlockSpec(memory_space=pl.ANY) for _ in range(5)],
        out_specs=pl.BlockSpec(memory_space=pl.ANY),
        scratch_shapes=[
            pltpu.SMEM((MOE_IDX_SLOTS, m), jnp.int32),
            pltpu.VMEM((2 * m * tr, LANES), jnp.uint32),
            pltpu.VMEM((2 * m * tr, LANES), jnp.uint32),
            pltpu.VMEM((d, ff), F32),
            pltpu.VMEM((d, ff), F32),
            pltpu.VMEM((ff, d), F32),
            pltpu.VMEM((d, ff), BF16),
            pltpu.VMEM((d, ff), BF16),
            pltpu.VMEM((ff, d), BF16),
            pltpu.SemaphoreType.DMA((MOE_IDX_SLOTS,)),
            pltpu.SemaphoreType.DMA((2,)),
            pltpu.SemaphoreType.DMA((2,)),
            pltpu.SemaphoreType.DMA((3,)),
        ],
    )
    return pl.pallas_call(
        kern,
        out_shape=jax.ShapeDtypeStruct(((MOE_TOP_K * t + 2 * m) * tr, LANES), jnp.uint32),
        grid_spec=grid_spec,
        compiler_params=_cparams(("arbitrary",), 58),
        name="moe_experts",
    )(block_e, block_first, block_next, n_used, row_dst, h_packed, wg, wu, wd)


def _combine_kernel(x_ref, y0_ref, y1_ref, w_ref, mod_ref, o_ref):
    tm, d = x_ref.shape
    half = d // 2
    tr = half // LANES
    w0 = w_ref[:, 0:1]
    w1 = w_ref[:, 1:2]
    lo0, hi0 = _load_token_tiles(y0_ref, 0, tm, tr)
    lo1, hi1 = _load_token_tiles(y1_ref, 0, tm, tr)
    o_ref[:, 0:half] = x_ref[:, 0:half] + mod_ref[5:6, 0:half] * (w0 * lo0 + w1 * lo1)
    o_ref[:, half:] = x_ref[:, half:] + mod_ref[5:6, half:] * (w0 * hi0 + w1 * hi1)


def _combine(x2d, y, w_tok, mod, seq):
    t, d = x2d.shape
    tm = 512
    tr = d // 2 // LANES
    blocks_per_seq = seq // tm
    nblk = t // tm
    return pl.pallas_call(
        _combine_kernel,
        out_shape=jax.ShapeDtypeStruct((t, d), F32),
        grid=(nblk,),
        in_specs=[
            pl.BlockSpec((tm, d), lambda i: (i, 0)),
            pl.BlockSpec((tm * tr, LANES), lambda i: (i, 0)),
            pl.BlockSpec((tm * tr, LANES), lambda i: (nblk + i, 0)),
            pl.BlockSpec((tm, LANES), lambda i: (i, 0)),
            pl.BlockSpec((None, 6, d), lambda i: (i // blocks_per_seq, 0, 0)),
        ],
        out_specs=pl.BlockSpec((tm, d), lambda i: (i, 0)),
        compiler_params=_cparams(("arbitrary",), 48),
        name="combine",
    )(x2d, y, y, w_tok, mod)


def kernel(x, c, rel_bias, w_mod, b_mod, norm1_w, norm2_w, w_in, q_norm_w, k_norm_w, conv_w, gate_b, h_norm_w, w_out, router_group_w, router_group_b, router_expert_w, router_expert_b, w_gate, w_up, w_down):
    batch, seq, d = x.shape
    depth = w_mod.shape[0]
    t = batch * seq
    attn_width = d // 2
    mlstm_width = d - attn_width
    dv = mlstm_width // MLSTM_HEADS
    dk = dv // 2
    n_main = 3 * attn_width + 2 * MLSTM_HEADS * dk + 2 * mlstm_width
    assert dk == LANES and attn_width % LANES == 0 and seq % 1024 == 0 and t % MOE_ROWS == 0
    assert t & (t - 1) == 0, "the MoE row tables recover the token of assignment k * t + token with a bit mask"
    assert all((seq // dil) % ATTN_BLOCK == 0 and window // dil == ATTN_BLOCK for window, dil in DILATED_PATTERNS)

    mod_all = _modulation(c, w_mod, b_mod)
    bias_tab = _attn_bias_tables(rel_bias)
    x2d = x.reshape(t, d)

    for l in range(depth):
        mod = mod_all[l]
        w_main = w_in[l, :, :n_main].astype(BF16)
        w_gates = jnp.zeros((d, LANES), F32).at[:, :2 * MLSTM_HEADS].set(w_in[l, :, n_main:])
        proj, gates = _in_proj(x2d, mod, norm1_w[l].reshape(1, d), w_main, w_gates.astype(BF16), seq)

        qw2 = jnp.tile(q_norm_w[l], LANES // ATTN_HEAD_DIM).reshape(1, LANES)
        kw2 = jnp.tile(k_norm_w[l], LANES // ATTN_HEAD_DIM).reshape(1, LANES)
        attn = _attention(proj, qw2, kw2, bias_tab, batch, seq, attn_width)

        gate_b_row = jnp.zeros((1, LANES), F32).at[0, :2 * MLSTM_HEADS].set(gate_b[l])
        hm = _mlstm(proj, gates, conv_w[l], gate_b_row, h_norm_w[l].reshape(1, mlstm_width),
                    batch, seq, attn_width, dk, dv)

        w_r = jnp.concatenate([router_group_w[l], router_expert_w[l]], axis=1).T
        w_r = jnp.zeros((ROUTER_ROWS, d), F32).at[:w_r.shape[0]].set(w_r)
        wr_hi, wr_lo = _split_bf16(w_r)
        b_r = jnp.concatenate([router_group_b[l], router_expert_b[l]])
        b_r = jnp.zeros((ROUTER_ROWS, 1), F32).at[:b_r.shape[0], 0].set(b_r)
        x2d, h_packed, logits_t = _out_proj(
            attn.reshape(t, attn_width), hm.reshape(t, mlstm_width), w_out[l].astype(BF16),
            x2d, mod, norm2_w[l].reshape(1, d), wr_hi, wr_lo, b_r, seq)

        eidx, w_tok = _route(logits_t)
        block_e, block_first, block_next, n_used, row_dst = _dispatch_tables(eidx, t)
        y = _moe(block_e, block_first, block_next, n_used, row_dst, h_packed, w_gate, w_up, w_down, l)
        x2d = _combine(x2d, y, w_tok, mod, seq)

    return x2d.reshape(batch, seq, d)
```

```python
import functools

import jax
import jax.numpy as jnp
import numpy as np
from jax import lax
from jax.experimental import pallas as pl
from jax.experimental.pallas import tpu as pltpu

F32 = jnp.float32
BF16 = jnp.bfloat16

NORM_EPS = 1e-6
ATTN_HEAD_DIM = 64
ATTN_BLOCK = 128
ATTN_MAJOR = 16
LOG2E = 1.4426950408889634
DILATED_PATTERNS = ((128, 1), (512, 4), (2048, 16))
NUM_BUCKETS = 32
REL_MAX_DIST = 2048
MLSTM_HEADS = 4
MLSTM_CHUNK = 128
CONV_WIDTH = 4
MOE_GROUPS = 4
MOE_EXPERTS_PER_GROUP = 8
N_EXPERTS = MOE_GROUPS * MOE_EXPERTS_PER_GROUP
MOE_TOP_K = 2
LANES = 128
SUBLANES = 8
ROUTER_ROWS = 40
MOE_ROWS = 256
MOE_IDX_SLOTS = 4
MOE_TABLE_TAIL = 2
MASK_VALUE = -1e30
MiB = 1024 * 1024


def _cparams(sem, vmem_mib):
    return pltpu.CompilerParams(dimension_semantics=sem, vmem_limit_bytes=vmem_mib * MiB)


def _split_bf16(a):
    hi = a.astype(BF16)
    lo = (a - hi.astype(F32)).astype(BF16)
    return hi, lo


def _dot(a, b):
    return jnp.dot(a, b, preferred_element_type=F32)


def _dot_nt(a, b):
    return lax.dot_general(a, b, (((1,), (1,)), ((), ())), preferred_element_type=F32)


def _dot_tn(a, b):
    return lax.dot_general(a, b, (((0,), (0,)), ((), ())), preferred_element_type=F32)


def _dot3(a_f32, w_hi, w_lo):
    a_hi, a_lo = _split_bf16(a_f32)
    return _dot(a_hi, w_hi) + _dot(a_lo, w_hi) + _dot(a_hi, w_lo)


def _rms_modulate(x, norm_w, scale, shift):
    ms = jnp.mean(x * x, axis=-1, keepdims=True)
    return x * lax.rsqrt(ms + NORM_EPS) * norm_w * (1.0 + scale) + shift


def _mod_kernel(c_ref, w_ref, b_ref, o_ref):
    c = c_ref[...]
    a = c * jax.nn.sigmoid(c)
    w_hi, w_lo = _split_bf16(w_ref[...])
    o_ref[...] = _dot3(a, w_hi, w_lo) + b_ref[...]


def _modulation(c, w_mod, b_mod):
    depth, d, n = w_mod.shape
    b = c.shape[0]
    bp = -(-b // SUBLANES) * SUBLANES
    cp = jnp.zeros((bp, d), F32).at[:b].set(c)
    tn = 1024
    out = pl.pallas_call(
        _mod_kernel,
        out_shape=jax.ShapeDtypeStruct((depth, bp, n), F32),
        grid=(depth, n // tn),
        in_specs=[
            pl.BlockSpec((bp, d), lambda l, j: (0, 0)),
            pl.BlockSpec((None, d, tn), lambda l, j: (l, 0, j)),
            pl.BlockSpec((None, 1, tn), lambda l, j: (l, 0, j)),
        ],
        out_specs=pl.BlockSpec((None, bp, tn), lambda l, j: (l, 0, j)),
        compiler_params=_cparams(("arbitrary", "arbitrary"), 40),
        name="modulation",
    )(cp, w_mod, b_mod.reshape(depth, 1, n))
    return out[:, :b].reshape(depth, b, 6, d)


def _in_proj_kernel(x_ref, mod_ref, nw_ref, w_ref, wg_ref, o_ref, g_ref, h_scr):
    @pl.when(pl.program_id(1) == 0)
    def _():
        h = _rms_modulate(x_ref[...], nw_ref[...], mod_ref[1:2, :], mod_ref[0:1, :])
        h_scr[...] = h.astype(BF16)
        g_ref[...] = _dot(h_scr[...], wg_ref[...])

    o_ref[...] = _dot(h_scr[...], w_ref[...]).astype(BF16)


def _in_proj(x2d, mod, norm_w, w_main, w_gates, seq):
    t, d = x2d.shape
    n = w_main.shape[1]
    tm, tn = 1024, 2048
    blocks_per_seq = seq // tm
    return pl.pallas_call(
        _in_proj_kernel,
        out_shape=(jax.ShapeDtypeStruct((t, n), BF16), jax.ShapeDtypeStruct((t, LANES), F32)),
        grid=(t // tm, n // tn),
        in_specs=[
            pl.BlockSpec((tm, d), lambda i, j: (i, 0)),
            pl.BlockSpec((None, 6, d), lambda i, j: (i // blocks_per_seq, 0, 0)),
            pl.BlockSpec((1, d), lambda i, j: (0, 0)),
            pl.BlockSpec((d, tn), lambda i, j: (0, j)),
            pl.BlockSpec((d, LANES), lambda i, j: (0, 0)),
        ],
        out_specs=(pl.BlockSpec((tm, tn), lambda i, j: (i, j)),
                   pl.BlockSpec((tm, LANES), lambda i, j: (i, 0))),
        scratch_shapes=[pltpu.VMEM((tm, d), BF16)],
        compiler_params=_cparams(("arbitrary", "arbitrary"), 56),
        name="in_proj",
    )(x2d, mod, norm_w, w_main, w_gates)


def _t5_causal_bucket(dist):
    max_exact = NUM_BUCKETS // 2
    d = np.maximum(dist, 1).astype(np.float32)
    large = max_exact + (np.log(d / max_exact) / np.log(REL_MAX_DIST / max_exact)
                         * (NUM_BUCKETS - max_exact)).astype(np.int32)
    large = np.minimum(large, NUM_BUCKETS - 1)
    return np.where(dist < max_exact, dist, large).astype(np.int32)


def _block_positions(dilation):
    g = ATTN_MAJOR // dilation
    plen = ATTN_BLOCK // g
    i = np.arange(ATTN_BLOCK)
    return g * (i % plen) + i // plen


def _attn_bias_tables(rel_bias):
    blk = ATTN_BLOCK
    tables = []
    for window, dilation in DILATED_PATTERNS:
        win = window // dilation
        loc = _block_positions(dilation)
        kpos = np.concatenate([loc, loc + blk])
        rel = loc[:, None] - kpos[None, :] + blk
        valid = (rel >= 0) & (rel <= win)
        bucket = _t5_causal_bucket(np.clip(rel, 0, win) * dilation)
        onehot = (jnp.asarray(bucket)[:, :, None] == jnp.arange(NUM_BUCKETS, dtype=jnp.int32)).astype(F32)
        bias = jnp.einsum('icb,bh->hic', onehot, rel_bias.astype(F32), precision=lax.Precision.HIGHEST) * LOG2E
        normal = jnp.where(valid[None], bias, MASK_VALUE)
        first = jnp.where((valid & (kpos >= blk)[None, :])[None], bias, MASK_VALUE)
        tables.append(jnp.stack([normal, first], axis=0))
    return jnp.stack(tables, axis=0)


def _attn_kernel(q_ref, k_ref, v_ref, qw_ref, kw_ref, bias_ref, o_ref,
                 stage, stage2, qs, ks, vs, acc_s, m_s, l_s, *, seq, unroll):
    blk = ATTN_BLOCK
    major = ATTN_MAJOR
    l16 = seq // major
    lane = lax.broadcasted_iota(jnp.int32, (1, LANES), 1)
    head0 = lane < ATTN_HEAD_DIM
    gi = lax.broadcasted_iota(jnp.int32, (LANES, LANES), 0) // ATTN_HEAD_DIM
    gj = lax.broadcasted_iota(jnp.int32, (LANES, LANES), 1) // ATTN_HEAD_DIM
    group_ones = (gi == gj).astype(BF16)
    ones_cols = jnp.ones((2 * blk, LANES), BF16)
    rows = 256

    quarter = seq // 4

    def to_major(dst):
        def pass1(cidx, carry):
            for r_lo in range(4):
                stage2[pl.ds(pl.multiple_of(r_lo * quarter + cidx * rows, rows), rows), :] = (
                    stage[pl.ds(r_lo + 4 * rows * cidx, rows, stride=4), :])
            return carry
        lax.fori_loop(0, quarter // rows, pass1, 0)

        def pass2(r, carry):
            src0 = jnp.bitwise_and(r, 3) * quarter + lax.shift_right_logical(r, 2)
            dst[pl.ds(pl.multiple_of(r * l16, l16), l16), :] = stage2[pl.ds(src0, l16, stride=4), :]
            return carry
        lax.fori_loop(0, major, pass2, 0)

    for src, dst, w_ref, scale in ((q_ref, qs, qw_ref, ATTN_HEAD_DIM ** -0.5 * LOG2E),
                                   (k_ref, ks, kw_ref, 1.0), (v_ref, vs, None, None)):
        def norm_body(c, carry, src=src, w_ref=w_ref, scale=scale):
            sl = pl.ds(pl.multiple_of(c * rows, rows), rows)
            xx = src[sl, :].astype(F32)
            if w_ref is not None:
                hi, lo = _split_bf16(xx * xx)
                ssq = _dot(hi, group_ones) + _dot(lo, group_ones)
                xx = xx * lax.rsqrt(ssq * (1.0 / ATTN_HEAD_DIM) + NORM_EPS) * w_ref[...] * scale
            stage[sl, :] = xx
            return carry
        lax.fori_loop(0, seq // rows, norm_body, 0, unroll=4)
        to_major(dst)

    for p, (_, dil) in enumerate(DILATED_PATTERNS):
        groups = major // dil
        plen = blk // groups
        nb_log2 = ((seq // dil) // blk).bit_length() - 1

        def starts(r_d, n, dil=dil, groups=groups, plen=plen):
            return [pl.multiple_of((j * dil + r_d) * l16 + n * plen, SUBLANES) for j in range(groups)]

        def load_blk(ref, st, plen=plen):
            return jnp.concatenate([ref[pl.ds(s, plen), :] for s in st], axis=0)

        def store_blk(ref, st, val, plen=plen):
            for j, s in enumerate(st):
                ref[pl.ds(s, plen), :] = val[j * plen:(j + 1) * plen]

        def body(it, carry, p=p, nb_log2=nb_log2, starts=starts, load_blk=load_blk, store_blk=store_blk):
            loaded = []
            for u in range(unroll):
                idx = it * unroll + u
                r_d = lax.shift_right_logical(idx, nb_log2)
                n = idx - lax.shift_left(r_d, nb_log2)
                cur = starts(r_d, n)
                prv = starts(r_d, jnp.maximum(n - 1, 0))
                first = (n == 0).astype(jnp.int32)
                qb = load_blk(qs, cur).astype(BF16)
                kw = jnp.concatenate([load_blk(ks, prv), load_blk(ks, cur)], axis=0).astype(BF16)
                vw = jnp.concatenate([load_blk(vs, prv), load_blk(vs, cur)], axis=0).astype(BF16)
                old = (load_blk(m_s, cur), load_blk(l_s, cur), load_blk(acc_s, cur)) if p > 0 else None
                loaded.append((cur, first, qb, kw, vw, old))
            results = []
            for cur, first, qb, kw, vw, old in loaded:
                vw_aug = jnp.concatenate([vw, ones_cols], axis=1)
                ms, res = [], []
                for h, hmask in enumerate((head0, jnp.logical_not(head0))):
                    qh = jnp.where(hmask, qb, jnp.zeros_like(qb))
                    s = _dot_nt(qh, kw) + bias_ref[p, first, h]
                    mh = jnp.max(s, axis=-1, keepdims=True)
                    ms.append(mh)
                    res.append(_dot(jnp.exp2(s - mh).astype(BF16), vw_aug))
                m_b = jnp.where(head0, ms[0], ms[1])
                a_b = jnp.where(head0, res[0][:, :LANES], res[1][:, :LANES])
                l_b = jnp.where(head0, res[0][:, LANES:], res[1][:, LANES:])
                if old is None:
                    results.append((cur, m_b, l_b, a_b))
                else:
                    m_o, l_o, a_o = old
                    m_new = jnp.maximum(m_o, m_b)
                    w_o = jnp.exp2(m_o - m_new)
                    w_b = jnp.exp2(m_b - m_new)
                    results.append((cur, m_new, w_o * l_o + w_b * l_b, w_o * a_o + w_b * a_b))
            for cur, m_new, l_new, a_new in results:
                store_blk(m_s, cur, m_new)
                store_blk(l_s, cur, l_new)
                store_blk(acc_s, cur, a_new)
            return carry

        lax.fori_loop(0, (seq // blk) // unroll, body, 0)

    def from_major(r, carry):
        sl = pl.ds(pl.multiple_of(r * l16, l16), l16)
        dst0 = jnp.bitwise_and(r, 3) * quarter + lax.shift_right_logical(r, 2)
        stage2[pl.ds(dst0, l16, stride=4), :] = acc_s[sl, :] / l_s[sl, :]
        return carry

    lax.fori_loop(0, major, from_major, 0)

    def to_natural(cidx, carry):
        for r_lo in range(4):
            stage[pl.ds(r_lo + 4 * rows * cidx, rows, stride=4), :] = (
                stage2[pl.ds(pl.multiple_of(r_lo * quarter + cidx * rows, rows), rows), :])
        return carry

    lax.fori_loop(0, quarter // rows, to_natural, 0)

    def out_body(c, carry):
        sl = pl.ds(pl.multiple_of(c * rows, rows), rows)
        o_ref[sl, :] = stage[sl, :].astype(o_ref.dtype)
        return carry

    lax.fori_loop(0, seq // rows, out_body, 0)


def _attention(proj, qw2, kw2, bias_tab, batch, seq, attn_width):
    t, n = proj.shape
    proj3 = proj.reshape(batch, seq, n)
    pairs = attn_width // LANES
    npat = len(DILATED_PATTERNS)
    kern = functools.partial(_attn_kernel, seq=seq, unroll=8)
    return pl.pallas_call(
        kern,
        out_shape=jax.ShapeDtypeStruct((batch, seq, attn_width), BF16),
        grid=(batch, pairs),
        in_specs=[
            pl.BlockSpec((None, seq, LANES), lambda b, h: (b, 0, h)),
            pl.BlockSpec((None, seq, LANES), lambda b, h: (b, 0, pairs + h)),
            pl.BlockSpec((None, seq, LANES), lambda b, h: (b, 0, 2 * pairs + h)),
            pl.BlockSpec((1, LANES), lambda b, h: (0, 0)),
            pl.BlockSpec((1, LANES), lambda b, h: (0, 0)),
            pl.BlockSpec((npat, 2, 2, ATTN_BLOCK, 2 * ATTN_BLOCK), lambda b, h: (0, 0, h, 0, 0)),
        ],
        out_specs=pl.BlockSpec((None, seq, LANES), lambda b, h: (b, 0, h)),
        scratch_shapes=[pltpu.VMEM((seq, LANES), F32) for _ in range(8)],
        compiler_params=_cparams(("arbitrary", "arbitrary"), 40),
        name="dilated_attention",
    )(proj3, proj3, proj3, qw2, kw2, bias_tab)


def _mlstm_kernel(q_ref, k_ref, v_ref, og_ref, g_ref, cw_ref, gb_ref, hw_ref, o_ref,
                  x_scr, c_scr, m_scr, *, rows, dk, dv):
    ch = MLSTM_CHUNK
    nheads = MLSTM_HEADS
    qk_w = nheads * dk
    aug = dv + LANES

    @pl.when(pl.program_id(1) == 0)
    def _():
        x_scr[0:SUBLANES, :] = jnp.zeros((SUBLANES, 2 * qk_w), F32)
        c_scr[...] = jnp.zeros_like(c_scr)
        m_scr[...] = jnp.zeros_like(m_scr)

    x_scr[SUBLANES:, 0:qk_w] = q_ref[...].astype(F32)
    x_scr[SUBLANES:, qk_w:] = k_ref[...].astype(F32)

    ri = lax.broadcasted_iota(jnp.int32, (ch, ch), 0)
    ci = lax.broadcasted_iota(jnp.int32, (ch, ch), 1)
    causal = ci <= ri
    tril = causal.astype(BF16)
    row8 = lax.broadcasted_iota(jnp.int32, (SUBLANES, 1), 0)
    ones_col = (lax.broadcasted_iota(jnp.int32, (ch, LANES), 1) == 0).astype(F32)
    k_scale = dk ** -0.5

    def chunk_body(c, carry):
        r0 = pl.multiple_of(c * ch, ch)
        prev8 = x_scr[pl.ds(r0, SUBLANES), :]
        cur = x_scr[pl.ds(r0 + SUBLANES, ch), :]
        conv = cur * cw_ref[CONV_WIDTH - 1:CONV_WIDTH, :]
        for sh in range(1, CONV_WIDTH):
            rolled = pltpu.roll(cur, sh, 0)
            head_rows = jnp.where(row8 < sh, pltpu.roll(prev8, sh, 0), rolled[0:SUBLANES])
            shifted = jnp.concatenate([head_rows, rolled[SUBLANES:]], axis=0)
            conv = conv + shifted * cw_ref[CONV_WIDTH - 1 - sh:CONV_WIDTH - sh, :]
        qk = conv * jax.nn.sigmoid(conv)

        gates = g_ref[pl.ds(r0, ch), :] + gb_ref[...]
        logf = jax.nn.log_sigmoid(gates)
        lf_hi, lf_lo = _split_bf16(logf)
        bcum = _dot(tril, lf_hi) + _dot(tril, lf_lo)
        gates_t = gates.T
        bcum_t = bcum.T

        for h in range(nheads):
            q = qk[:, h * dk:(h + 1) * dk]
            k = qk[:, qk_w + h * dk:qk_w + (h + 1) * dk] * k_scale
            v = v_ref[pl.ds(r0, ch), h * dv:(h + 1) * dv].astype(F32)
            v_aug = jnp.concatenate([v, ones_col], axis=1)
            i_col = gates[:, h:h + 1]
            b_col = bcum[:, nheads + h:nheads + h + 1]
            i_row = gates_t[h:h + 1, :]
            b_row = bcum_t[nheads + h:nheads + h + 1, :]
            m_prev = m_scr[h, 0:1, 0:1]
            c_prev = c_scr[h]

            dlog = jnp.where(causal, b_col - b_row + i_row, -jnp.inf)
            inter = b_col + m_prev
            m_t = jnp.maximum(inter, jnp.max(dlog, axis=-1, keepdims=True))
            qb = q.astype(BF16)
            kb = k.astype(BF16)
            sm = _dot_nt(qb, kb) * jnp.exp(dlog - m_t)
            e_inter = jnp.exp(inter - m_t)
            tot = e_inter * _dot(qb, c_prev.astype(BF16)) + _dot(sm.astype(BF16), v_aug.astype(BF16))
            num = tot[:, 0:dv]
            den = tot[:, dv:dv + 1]
            hcell = num / jnp.maximum(jnp.abs(den), jnp.exp(-m_t))

            g_last = b_col[ch - 1:ch, :]
            a_col = g_last - b_col + i_col
            m_new = jnp.maximum(g_last + m_prev, jnp.max(a_col, axis=0, keepdims=True))
            decay = jnp.exp(g_last + m_prev - m_new)
            w_col = jnp.exp(a_col - m_new)
            c_scr[h] = decay * c_prev + _dot_tn(kb, (w_col * v_aug).astype(BF16))
            m_scr[h] = jnp.broadcast_to(m_new, (SUBLANES, LANES))

            ms = jnp.mean(hcell * hcell, axis=-1, keepdims=True)
            hn = hcell * lax.rsqrt(ms + NORM_EPS) * hw_ref[:, h * dv:(h + 1) * dv]
            og = og_ref[pl.ds(r0, ch), h * dv:(h + 1) * dv].astype(F32)
            o_ref[pl.ds(r0, ch), h * dv:(h + 1) * dv] = (jax.nn.sigmoid(og) * hn).astype(o_ref.dtype)
        return carry

    lax.fori_loop(0, rows // ch, chunk_body, 0, unroll=4)
    x_scr[0:SUBLANES, :] = x_scr[rows:rows + SUBLANES, :]


def _mlstm(proj, gates, conv_w, gate_b_row, h_norm_w, batch, seq, attn_width, dk, dv):
    t, n = proj.shape
    proj3 = proj.reshape(batch, seq, n)
    gates3 = gates.reshape(batch, seq, LANES)
    nheads = MLSTM_HEADS
    qk_w = nheads * dk
    v_w = nheads * dv
    rows = 512
    q_blk = (3 * attn_width) // qk_w
    v_blk = (3 * attn_width + 2 * qk_w) // v_w
    kern = functools.partial(_mlstm_kernel, rows=rows, dk=dk, dv=dv)
    return pl.pallas_call(
        kern,
        out_shape=jax.ShapeDtypeStruct((batch, seq, v_w), BF16),
        grid=(batch, seq // rows),
        in_specs=[
            pl.BlockSpec((None, rows, qk_w), lambda b, j: (b, j, q_blk)),
            pl.BlockSpec((None, rows, qk_w), lambda b, j: (b, j, q_blk + 1)),
            pl.BlockSpec((None, rows, v_w), lambda b, j: (b, j, v_blk)),
            pl.BlockSpec((None, rows, v_w), lambda b, j: (b, j, v_blk + 1)),
            pl.BlockSpec((None, rows, LANES), lambda b, j: (b, j, 0)),
            pl.BlockSpec((CONV_WIDTH, 2 * qk_w), lambda b, j: (0, 0)),
            pl.BlockSpec((1, LANES), lambda b, j: (0, 0)),
            pl.BlockSpec((1, v_w), lambda b, j: (0, 0)),
        ],
        out_specs=pl.BlockSpec((None, rows, v_w), lambda b, j: (b, j, 0)),
        scratch_shapes=[
            pltpu.VMEM((rows + SUBLANES, 2 * qk_w), F32),
            pltpu.VMEM((nheads, dk, dv + LANES), F32),
            pltpu.VMEM((nheads, SUBLANES, LANES), F32),
        ],
        compiler_params=_cparams(("arbitrary", "arbitrary"), 40),
        name="mlstm",
    )(proj3, proj3, proj3, proj3, gates3, conv_w, gate_b_row, h_norm_w)


def _store_token_tiles(ref, base, h):
    m, w2 = h.shape
    w = w2 // 2
    tile_rows = w // LANES
    u = pltpu.bitcast(h.astype(BF16).astype(F32), jnp.uint32)
    packed = (u[:, :w] >> 16) | (u[:, w:] & jnp.uint32(0xFFFF0000))
    for k in range(tile_rows):
        ref[pl.ds(base + k, m, stride=tile_rows), :] = packed[:, k * LANES:(k + 1) * LANES]


def _load_token_tiles(ref, base, m, tile_rows):
    lo, hi = [], []
    for k in range(tile_rows):
        p = ref[pl.ds(base + k, m, stride=tile_rows), :]
        lo.append(pltpu.bitcast(p << 16, F32))
        hi.append(pltpu.bitcast(p & jnp.uint32(0xFFFF0000), F32))
    return jnp.concatenate(lo, axis=1), jnp.concatenate(hi, axis=1)


def _out_proj_kernel(a_ref, m_ref, w_ref, x_ref, mod_ref, nw_ref, wrh_ref, wrl_ref, br_ref,
                     xo_ref, hp_ref, lg_ref, *, attn_width, sub_rows):
    tm, d = x_ref.shape
    tile_rows = d // 2 // LANES
    half = d // 2
    col = 4 * LANES
    for s in range(tm // sub_rows):
        rs = slice(s * sub_rows, (s + 1) * sub_rows)
        a_blk, m_blk = a_ref[rs, :], m_ref[rs, :]
        ssq = jnp.zeros((sub_rows, 1), F32)
        for c in range(d // col):
            cs = slice(c * col, (c + 1) * col)
            mix = _dot(a_blk, w_ref[0:attn_width, cs]) + _dot(m_blk, w_ref[attn_width:, cs])
            xn = x_ref[rs, cs] + mod_ref[2:3, cs] * mix
            xo_ref[rs, cs] = xn
            ssq = ssq + jnp.sum(xn * xn, axis=-1, keepdims=True)
        rstd = lax.rsqrt(ssq * (1.0 / d) + NORM_EPS)
        logits = br_ref[...]
        for k in range(tile_rows):
            parts = []
            for lo in (k * LANES, half + k * LANES):
                ks = slice(lo, lo + LANES)
                hk = xo_ref[rs, ks] * rstd * nw_ref[:, ks] * (1.0 + mod_ref[4:5, ks]) + mod_ref[3:4, ks]
                h_hi, h_lo = _split_bf16(hk)
                logits = logits + (_dot_nt(wrh_ref[:, ks], h_hi) + _dot_nt(wrh_ref[:, ks], h_lo)
                                   + _dot_nt(wrl_ref[:, ks], h_hi))
                parts.append(pltpu.bitcast(h_hi.astype(F32), jnp.uint32))
            packed = (parts[0] >> 16) | (parts[1] & jnp.uint32(0xFFFF0000))
            hp_ref[pl.ds(s * sub_rows * tile_rows + k, sub_rows, stride=tile_rows), :] = packed
        lg_ref[:, rs] = logits


def _out_proj(attn2d, hm2d, w_out, x2d, mod, norm_w, wr_hi, wr_lo, br, seq):
    t, d = x2d.shape
    aw = attn2d.shape[1]
    mw = hm2d.shape[1]
    tm = 512
    blocks_per_seq = seq // tm
    tile_rows = d // 2 // LANES
    kern = functools.partial(_out_proj_kernel, attn_width=aw, sub_rows=512)
    return pl.pallas_call(
        kern,
        out_shape=(jax.ShapeDtypeStruct((t, d), F32),
                   jax.ShapeDtypeStruct((t * tile_rows, LANES), jnp.uint32),
                   jax.ShapeDtypeStruct((ROUTER_ROWS, t), F32)),
        grid=(t // tm,),
        in_specs=[
            pl.BlockSpec((tm, aw), lambda i: (i, 0)),
            pl.BlockSpec((tm, mw), lambda i: (i, 0)),
            pl.BlockSpec((aw + mw, d), lambda i: (0, 0)),
            pl.BlockSpec((tm, d), lambda i: (i, 0)),
            pl.BlockSpec((None, 6, d), lambda i: (i // blocks_per_seq, 0, 0)),
            pl.BlockSpec((1, d), lambda i: (0, 0)),
            pl.BlockSpec((ROUTER_ROWS, d), lambda i: (0, 0)),
            pl.BlockSpec((ROUTER_ROWS, d), lambda i: (0, 0)),
            pl.BlockSpec((ROUTER_ROWS, 1), lambda i: (0, 0)),
        ],
        out_specs=(pl.BlockSpec((tm, d), lambda i: (i, 0)),
                   pl.BlockSpec((tm * tile_rows, LANES), lambda i: (i, 0)),
                   pl.BlockSpec((ROUTER_ROWS, tm), lambda i: (0, i))),
        compiler_params=_cparams(("arbitrary",), 48),
        name="out_proj",
    )(attn2d, hm2d, w_out, x2d, mod, norm_w, wr_hi, wr_lo, br)


def _route_kernel(lg_ref, idx_ref, w_ref):
    ng, ne = MOE_GROUPS, MOE_EXPERTS_PER_GROUP
    lg = [lg_ref[g:g + 1, :] for g in range(ng)]
    best, gsel = lg[0], jnp.zeros_like(lg[0], dtype=jnp.int32)
    for g in range(1, ng):
        better = lg[g] > best
        best = jnp.where(better, lg[g], best)
        gsel = jnp.where(better, g, gsel)
    denom = jnp.exp(lg[0] - best)
    for g in range(1, ng):
        denom = denom + jnp.exp(lg[g] - best)
    p_group = 1.0 / denom

    le = []
    for e in range(ne):
        v = lg_ref[ng + e:ng + e + 1, :]
        for g in range(1, ng):
            v = jnp.where(gsel == g, lg_ref[ng + g * ne + e:ng + g * ne + e + 1, :], v)
        le.append(v)
    v1, i1 = le[0], jnp.zeros_like(gsel)
    for e in range(1, ne):
        better = le[e] > v1
        v1 = jnp.where(better, le[e], v1)
        i1 = jnp.where(better, e, i1)
    v2 = jnp.full_like(v1, -jnp.inf)
    i2 = jnp.zeros_like(gsel)
    for e in range(ne):
        better = jnp.logical_and(le[e] > v2, i1 != e)
        v2 = jnp.where(better, le[e], v2)
        i2 = jnp.where(better, e, i2)
    e2 = jnp.exp(v2 - v1)
    inv = 1.0 / (1.0 + e2)
    tn = gsel.shape[1]
    zero_i = jnp.zeros((SUBLANES - MOE_TOP_K, tn), jnp.int32)
    idx_ref[...] = jnp.concatenate([gsel * ne + i1, gsel * ne + i2, zero_i], axis=0)
    w_rows = jnp.concatenate([p_group * inv, p_group * (e2 * inv), jnp.zeros((LANES - MOE_TOP_K, tn), F32)], axis=0)
    for c in range(tn // LANES):
        w_ref[c * LANES:(c + 1) * LANES, :] = w_rows[:, c * LANES:(c + 1) * LANES].T


def _route(logits_t):
    rows, t = logits_t.shape
    tn = 2048
    return pl.pallas_call(
        _route_kernel,
        out_shape=(jax.ShapeDtypeStruct((SUBLANES, t), jnp.int32),
                   jax.ShapeDtypeStruct((t, LANES), F32)),
        grid=(t // tn,),
        in_specs=[pl.BlockSpec((rows, tn), lambda i: (0, i))],
        out_specs=(pl.BlockSpec((SUBLANES, tn), lambda i: (0, i)),
                   pl.BlockSpec((tn, LANES), lambda i: (i, 0))),
        compiler_params=_cparams(("arbitrary",), 32),
        name="route",
    )(logits_t)


def _dispatch_tables(eidx, t):
    m = MOE_ROWS
    a_total = MOE_TOP_K * t
    n_blocks = a_total // m + N_EXPERTS
    n_rows = n_blocks * m
    src_bits = (a_total - 1).bit_length()
    experts = jnp.arange(N_EXPERTS, dtype=jnp.int32)
    e_flat = eidx[:MOE_TOP_K].reshape(a_total)
    counts = jnp.sum((e_flat[:, None] == experts[None, :]).astype(jnp.int32), axis=0)
    padded = (counts + m - 1) // m * m
    pad_end = jnp.cumsum(padded)
    pad_start = pad_end - padded
    start = jnp.cumsum(counts) - counts
    n_used = (pad_end[-1] // m).astype(jnp.int32)
    packed = lax.sort(e_flat * (1 << src_bits) + jnp.arange(a_total, dtype=jnp.int32))
    order = packed & ((1 << src_bits) - 1)

    def lookup(idx, table):
        return jnp.sum(jnp.where(idx[:, None] == experts[None, :], table[None, :], 0), axis=1)

    blk = jnp.arange(n_blocks, dtype=jnp.int32)
    block_e = jnp.minimum(jnp.sum((blk[:, None] * m >= pad_end[None, :]).astype(jnp.int32), axis=1), N_EXPERTS - 1)
    block_off = blk * m - lookup(block_e, pad_start)
    block_valid = jnp.clip(lookup(block_e, counts) - block_off, 0, m)
    block_valid = jnp.where(blk < n_used, block_valid, 0)
    sorted_pos = (lookup(block_e, start) + block_off)[:, None] + jnp.arange(m, dtype=jnp.int32)[None, :]
    valid = jnp.arange(m, dtype=jnp.int32)[None, :] < block_valid[:, None]
    all_blk = jnp.arange(-1, n_blocks + MOE_TABLE_TAIL + 1, dtype=jnp.int32)
    all_blk = all_blk.at[-1].set(0)
    spare = a_total + jnp.bitwise_and(all_blk, 1)[:, None] * m + jnp.arange(m, dtype=jnp.int32)[None, :]
    row_dst = jnp.where(valid, order[jnp.clip(sorted_pos, 0, a_total - 1)], spare[1:n_blocks + 1])
    row_dst = jnp.concatenate([spare[:1], row_dst, spare[n_blocks + 1:]], axis=0)
    block_e = jnp.where(blk < n_used, block_e, jnp.max(jnp.where(blk < n_used, block_e, 0)))
    block_first = jnp.concatenate([jnp.ones((1,), jnp.int32), (block_e[1:] != block_e[:-1]).astype(jnp.int32)])
    used_e = jnp.where(blk < n_used, block_e, N_EXPERTS)
    later = jnp.where(used_e[None, :] > block_e[:, None], used_e[None, :], N_EXPERTS)
    block_next = jnp.min(later, axis=1)
    block_next = jnp.where(block_next < N_EXPERTS, block_next, -1).astype(jnp.int32)
    return block_e, block_first, block_next, n_used.reshape(1), row_dst


def _moe_kernel(be_ref, first_ref, next_ref, nu_ref, idx_hbm, h_hbm, wg_hbm, wu_hbm, wd_hbm, out_hbm,
                idx_s, xbuf, ybuf, wg_f, wu_f, wd_f, wgb, wub, wdb, sem_i, sem_g, sem_s, sem_w,
                *, tokens, layer, ff_chunk):
    m = MOE_ROWS
    i = pl.program_id(0)
    n_used = nu_ref[0]
    par = lax.rem(i, 2)
    other = 1 - par
    ff = wgb.shape[1]
    half = wgb.shape[0] // 2
    tr = half // LANES
    slot_rows = m * tr
    n_chunks = ff // ff_chunk
    active = i < n_used

    def weight_copies(expert):
        return [pltpu.make_async_copy(src.at[layer, expert], dst, sem_w.at[k])
                for k, (src, dst) in enumerate(((wg_hbm, wg_f), (wu_hbm, wu_f), (wd_hbm, wd_f)))]

    def tile_rows_at(row):
        return pl.ds(pl.multiple_of(row * tr, tr), tr)

    def idx_copy(table_row):
        s = jnp.bitwise_and(table_row, MOE_IDX_SLOTS - 1)
        return pltpu.make_async_copy(idx_hbm.at[table_row], idx_s.at[s], sem_i.at[s])

    def gather_row(buf_half, table_row, j):
        tok = jnp.bitwise_and(idx_s[jnp.bitwise_and(table_row, MOE_IDX_SLOTS - 1), j], tokens - 1)
        pltpu.make_async_copy(h_hbm.at[tile_rows_at(tok)], xbuf.at[tile_rows_at(buf_half * m + j)],
                              sem_g.at[buf_half]).start()

    def scatter_row(buf_half, table_row, j):
        dst = idx_s[jnp.bitwise_and(table_row, MOE_IDX_SLOTS - 1), j]
        pltpu.make_async_copy(ybuf.at[tile_rows_at(buf_half * m + j)], out_hbm.at[tile_rows_at(dst)],
                              sem_s.at[buf_half]).start()

    def half_rows(buf_half):
        return pl.ds(pl.multiple_of(buf_half * slot_rows, slot_rows), slot_rows)

    def wait_gather(buf_half):
        pltpu.make_async_copy(h_hbm.at[pl.ds(0, slot_rows)], xbuf.at[half_rows(buf_half)], sem_g.at[buf_half]).wait()

    def wait_scatter(buf_half):
        pltpu.make_async_copy(ybuf.at[half_rows(buf_half)], out_hbm.at[pl.ds(0, slot_rows)], sem_s.at[buf_half]).wait()

    base = pl.multiple_of(par * slot_rows, slot_rows)

    @pl.when(i == 0)
    def _():
        ybuf[...] = jnp.zeros_like(ybuf)
        for cp in weight_copies(be_ref[0]):
            cp.start(priority=1)
        last_row = idx_hbm.shape[0] - 1
        idx_copy(last_row).start()
        idx_copy(last_row).wait()

        def spare_body(j, carry):
            scatter_row(0, last_row, j)
            return carry
        lax.fori_loop(0, m, spare_body, 0, unroll=8)
        for row in range(3):
            idx_copy(row).start()
        for row in range(3):
            idx_copy(row).wait()
        idx_copy(3).start()

        def body(j, carry):
            gather_row(0, 1, j)
            return carry
        lax.fori_loop(0, m, body, 0, unroll=8)

    @pl.when(jnp.logical_and(active, first_ref[i] == 1))
    def _():
        for cp in weight_copies(be_ref[i]):
            cp.wait()
        rows = 256
        for src, dst in ((wg_f, wgb), (wu_f, wub), (wd_f, wdb)):
            def cast_body(r, carry, src=src, dst=dst):
                sl = pl.ds(pl.multiple_of(r * rows, rows), rows)
                dst[sl, :] = src[sl, :].astype(BF16)
                return carry
            lax.fori_loop(0, src.shape[0] // rows, cast_body, 0)

        @pl.when(next_ref[i] >= 0)
        def _():
            for cp in weight_copies(next_ref[i]):
                cp.start(priority=1)

    @pl.when(jnp.logical_and(active, i >= 1))
    def _():
        idx_copy(i + 2).wait()
        idx_copy(i + 3).start()

    @pl.when(active)
    def _():
        wait_gather(par)
        for j in range(m):
            scatter_row(other, i, j)
        x_lo, x_hi = _load_token_tiles(xbuf, base, m, tr)
        x_lo, x_hi = x_lo.astype(BF16), x_hi.astype(BF16)
        gather_dots = (5 * n_chunks) // 2
        issued = 0

        def dot_with_rows(dot_index, a, b):
            nonlocal issued
            upto = min(m, (m * (dot_index + 1)) // gather_dots)
            for j in range(issued, upto):
                gather_row(other, i + 2, j)
            issued = upto
            return _dot(a, b)

        for c in range(n_chunks):
            cs = slice(c * ff_chunk, (c + 1) * ff_chunk)
            k = 5 * c
            g = dot_with_rows(k, x_lo, wgb[0:half, cs]) + dot_with_rows(k + 1, x_hi, wgb[half:, cs])
            u = dot_with_rows(k + 2, x_lo, wub[0:half, cs]) + dot_with_rows(k + 3, x_hi, wub[half:, cs])
            act = (g * jax.nn.sigmoid(g) * u).astype(BF16)
            part = dot_with_rows(k + 4, act, wdb[cs, :])
            acc = part if c == 0 else acc + part
        wait_scatter(par)
        _store_token_tiles(ybuf, base, acc)

        @pl.when(i == n_used - 1)
        def _():
            wait_scatter(other)

            def body(j, carry):
                scatter_row(par, i + 1, j)
                return carry
            lax.fori_loop(0, m, body, 0, unroll=8)
            wait_scatter(par)
            wait_gather(other)
            idx_copy(i + 3).wait()


def _moe(block_e, block_first, block_next, n_used, row_dst, h_packed, wg, wu, wd, layer):
    d, ff = wg.shape[2], wg.shape[3]
    tr = d // 2 // LANES
    t = h_packed.shape[0] // tr
    m = MOE_ROWS
    n_blocks = block_e.shape[0]
    kern = functools.partial(_moe_kernel, tokens=t, layer=layer, ff_chunk=256)
    grid_spec = pltpu.PrefetchScalarGridSpec(
        num_scalar_prefetch=4,
        grid=(n_blocks,),
        in_specs=[pl.BlockSpec(memory_space=pl.ANY) for _ in range(5)],
        out_specs=pl.BlockSpec(memory_space=pl.ANY),
        scratch_shapes=[
            pltpu.SMEM((MOE_IDX_SLOTS, m), jnp.int32),
            pltpu.VMEM((2 * m * tr, LANES), jnp.uint32),
            pltpu.VMEM((2 * m * tr, LANES), jnp.uint32),
            pltpu.VMEM((d, ff), F32),
            pltpu.VMEM((d, ff), F32),
            pltpu.VMEM((ff, d), F32),
            pltpu.VMEM((d, ff), BF16),
            pltpu.VMEM((d, ff), BF16),
            pltpu.VMEM((ff, d), BF16),
            pltpu.SemaphoreType.DMA((MOE_IDX_SLOTS,)),
            pltpu.SemaphoreType.DMA((2,)),
            pltpu.SemaphoreType.DMA((2,)),
            pltpu.SemaphoreType.DMA((3,)),
        ],
    )
    return pl.pallas_call(
        kern,
        out_shape=jax.ShapeDtypeStruct(((MOE_TOP_K * t + 2 * m) * tr, LANES), jnp.uint32),
        grid_spec=grid_spec,
        compiler_params=_cparams(("arbitrary",), 58),
        name="moe_experts",
    )(block_e, block_first, block_next, n_used, row_dst, h_packed, wg, wu, wd)


def _combine_kernel(x_ref, y0_ref, y1_ref, w_ref, mod_ref, o_ref):
    tm, d = x_ref.shape
    half = d // 2
    tr = half // LANES
    w0 = w_ref[:, 0:1]
    w1 = w_ref[:, 1:2]
    lo0, hi0 = _load_token_tiles(y0_ref, 0, tm, tr)
    lo1, hi1 = _load_token_tiles(y1_ref, 0, tm, tr)
    o_ref[:, 0:half] = x_ref[:, 0:half] + mod_ref[5:6, 0:half] * (w0 * lo0 + w1 * lo1)
    o_ref[:, half:] = x_ref[:, half:] + mod_ref[5:6, half:] * (w0 * hi0 + w1 * hi1)


def _combine(x2d, y, w_tok, mod, seq):
    t, d = x2d.shape
    tm = 512
    tr = d // 2 // LANES
    blocks_per_seq = seq // tm
    nblk = t // tm
    return pl.pallas_call(
        _combine_kernel,
        out_shape=jax.ShapeDtypeStruct((t, d), F32),
        grid=(nblk,),
        in_specs=[
            pl.BlockSpec((tm, d), lambda i: (i, 0)),
            pl.BlockSpec((tm * tr, LANES), lambda i: (i, 0)),
            pl.BlockSpec((tm * tr, LANES), lambda i: (nblk + i, 0)),
            pl.BlockSpec((tm, LANES), lambda i: (i, 0)),
            pl.BlockSpec((None, 6, d), lambda i: (i // blocks_per_seq, 0, 0)),
        ],
        out_specs=pl.BlockSpec((tm, d), lambda i: (i, 0)),
        compiler_params=_cparams(("arbitrary",), 48),
        name="combine",
    )(x2d, y, y, w_tok, mod)


def kernel(x, c, rel_bias, w_mod, b_mod, norm1_w, norm2_w, w_in, q_norm_w, k_norm_w, conv_w, gate_b, h_norm_w, w_out, router_group_w, router_group_b, router_expert_w, router_expert_b, w_gate, w_up, w_down):
    batch, seq, d = x.shape
    depth = w_mod.shape[0]
    t = batch * seq
    attn_width = d // 2
    mlstm_width = d - attn_width
    dv = mlstm_width // MLSTM_HEADS
    dk = dv // 2
    n_main = 3 * attn_width + 2 * MLSTM_HEADS * dk + 2 * mlstm_width
    assert dk == LANES and attn_width % LANES == 0 and seq % 1024 == 0 and t % MOE_ROWS == 0
    assert t & (t - 1) == 0, "the MoE row tables recover the token of assignment k * t + token with a bit mask"
    assert all((seq // dil) % ATTN_BLOCK == 0 and window // dil == ATTN_BLOCK for window, dil in DILATED_PATTERNS)

    mod_all = _modulation(c, w_mod, b_mod)
    bias_tab = _attn_bias_tables(rel_bias)
    x2d = x.reshape(t, d)

    for l in range(depth):
        mod = mod_all[l]
        w_main = w_in[l, :, :n_main].astype(BF16)
        w_gates = jnp.zeros((d, LANES), F32).at[:, :2 * MLSTM_HEADS].set(w_in[l, :, n_main:])
        proj, gates = _in_proj(x2d, mod, norm1_w[l].reshape(1, d), w_main, w_gates.astype(BF16), seq)

        qw2 = jnp.tile(q_norm_w[l], LANES // ATTN_HEAD_DIM).reshape(1, LANES)
        kw2 = jnp.tile(k_norm_w[l], LANES // ATTN_HEAD_DIM).reshape(1, LANES)
        attn = _attention(proj, qw2, kw2, bias_tab, batch, seq, attn_width)

        gate_b_row = jnp.zeros((1, LANES), F32).at[0, :2 * MLSTM_HEADS].set(gate_b[l])
        hm = _mlstm(proj, gates, conv_w[l], gate_b_row, h_norm_w[l].reshape(1, mlstm_width),
                    batch, seq, attn_width, dk, dv)

        w_r = jnp.concatenate([router_group_w[l], router_expert_w[l]], axis=1).T
        w_r = jnp.zeros((ROUTER_ROWS, d), F32).at[:w_r.shape[0]].set(w_r)
        wr_hi, wr_lo = _split_bf16(w_r)
        b_r = jnp.concatenate([router_group_b[l], router_expert_b[l]])
        b_r = jnp.zeros((ROUTER_ROWS, 1), F32).at[:b_r.shape[0], 0].set(b_r)
        x2d, h_packed, logits_t = _out_proj(
            attn.reshape(t, attn_width), hm.reshape(t, mlstm_width), w_out[l].astype(BF16),
            x2d, mod, norm2_w[l].reshape(1, d), wr_hi, wr_lo, b_r, seq)

        eidx, w_tok = _route(logits_t)
        block_e, block_first, block_next, n_used, row_dst = _dispatch_tables(eidx, t)
        y = _moe(block_e, block_first, block_next, n_used, row_dst, h_packed, w_gate, w_up, w_down, l)
        x2d = _combine(x2d, y, w_tok, mod, seq)

    return x2d.reshape(batch, seq, d)
```

```python
import functools

import jax
import jax.numpy as jnp
import numpy as np
from jax import lax
from jax.experimental import pallas as pl
from jax.experimental.pallas import tpu as pltpu

F32 = jnp.float32
BF16 = jnp.bfloat16

NORM_EPS = 1e-6
ATTN_HEAD_DIM = 64
ATTN_BLOCK = 128
ATTN_MAJOR = 16
LOG2E = 1.4426950408889634
DILATED_PATTERNS = ((128, 1), (512, 4), (2048, 16))
NUM_BUCKETS = 32
REL_MAX_DIST = 2048
MLSTM_HEADS = 4
MLSTM_CHUNK = 128
CONV_WIDTH = 4
MOE_GROUPS = 4
MOE_EXPERTS_PER_GROUP = 8
N_EXPERTS = MOE_GROUPS * MOE_EXPERTS_PER_GROUP
MOE_TOP_K = 2
LANES = 128
SUBLANES = 8
ROUTER_ROWS = 40
MOE_ROWS = 256
MOE_IDX_SLOTS = 4
MOE_TABLE_TAIL = 2
MASK_VALUE = -1e30
MiB = 1024 * 1024


def _cparams(sem, vmem_mib):
    return pltpu.CompilerParams(dimension_semantics=sem, vmem_limit_bytes=vmem_mib * MiB)


def _split_bf16(a):
    hi = a.astype(BF16)
    lo = (a - hi.astype(F32)).astype(BF16)
    return hi, lo


def _dot(a, b):
    return jnp.dot(a, b, preferred_element_type=F32)


def _dot_nt(a, b):
    return lax.dot_general(a, b, (((1,), (1,)), ((), ())), preferred_element_type=F32)


def _dot_tn(a, b):
    return lax.dot_general(a, b, (((0,), (0,)), ((), ())), preferred_element_type=F32)


def _dot3(a_f32, w_hi, w_lo):
    a_hi, a_lo = _split_bf16(a_f32)
    return _dot(a_hi, w_hi) + _dot(a_lo, w_hi) + _dot(a_hi, w_lo)


def _rms_modulate(x, norm_w, scale, shift):
    ms = jnp.mean(x * x, axis=-1, keepdims=True)
    return x * lax.rsqrt(ms + NORM_EPS) * norm_w * (1.0 + scale) + shift


def _mod_kernel(c_ref, w_ref, b_ref, o_ref):
    c = c_ref[...]
    a = c * jax.nn.sigmoid(c)
    w_hi, w_lo = _split_bf16(w_ref[...])
    o_ref[...] = _dot3(a, w_hi, w_lo) + b_ref[...]


def _modulation(c, w_mod, b_mod):
    depth, d, n = w_mod.shape
    b = c.shape[0]
    bp = -(-b // SUBLANES) * SUBLANES
    cp = jnp.zeros((bp, d), F32).at[:b].set(c)
    tn = 1024
    out = pl.pallas_call(
        _mod_kernel,
        out_shape=jax.ShapeDtypeStruct((depth, bp, n), F32),
        grid=(depth, n // tn),
        in_specs=[
            pl.BlockSpec((bp, d), lambda l, j: (0, 0)),
            pl.BlockSpec((None, d, tn), lambda l, j: (l, 0, j)),
            pl.BlockSpec((None, 1, tn), lambda l, j: (l, 0, j)),
        ],
        out_specs=pl.BlockSpec((None, bp, tn), lambda l, j: (l, 0, j)),
        compiler_params=_cparams(("arbitrary", "arbitrary"), 40),
        name="modulation",
    )(cp, w_mod, b_mod.reshape(depth, 1, n))
    return out[:, :b].reshape(depth, b, 6, d)


def _in_proj_kernel(x_ref, mod_ref, nw_ref, w_ref, wg_ref, o_ref, g_ref, h_scr):
    @pl.when(pl.program_id(1) == 0)
    def _():
        h = _rms_modulate(x_ref[...], nw_ref[...], mod_ref[1:2, :], mod_ref[0:1, :])
        h_scr[...] = h.astype(BF16)
        g_ref[...] = _dot(h_scr[...], wg_ref[...])

    o_ref[...] = _dot(h_scr[...], w_ref[...]).astype(BF16)


def _in_proj(x2d, mod, norm_w, w_main, w_gates, seq):
    t, d = x2d.shape
    n = w_main.shape[1]
    tm, tn = 1024, 2048
    blocks_per_seq = seq // tm
    return pl.pallas_call(
        _in_proj_kernel,
        out_shape=(jax.ShapeDtypeStruct((t, n), BF16), jax.ShapeDtypeStruct((t, LANES), F32)),
        grid=(t // tm, n // tn),
        in_specs=[
            pl.BlockSpec((tm, d), lambda i, j: (i, 0)),
            pl.BlockSpec((None, 6, d), lambda i, j: (i // blocks_per_seq, 0, 0)),
            pl.BlockSpec((1, d), lambda i, j: (0, 0)),
            pl.BlockSpec((d, tn), lambda i, j: (0, j)),
            pl.BlockSpec((d, LANES), lambda i, j: (0, 0)),
        ],
        out_specs=(pl.BlockSpec((tm, tn), lambda i, j: (i, j)),
                   pl.BlockSpec((tm, LANES), lambda i, j: (i, 0))),
        scratch_shapes=[pltpu.VMEM((tm, d), BF16)],
        compiler_params=_cparams(("arbitrary", "arbitrary"), 56),
        name="in_proj",
    )(x2d, mod, norm_w, w_main, w_gates)


def _t5_causal_bucket(dist):
    max_exact = NUM_BUCKETS // 2
    d = np.maximum(dist, 1).astype(np.float32)
    large = max_exact + (np.log(d / max_exact) / np.log(REL_MAX_DIST / max_exact)
                         * (NUM_BUCKETS - max_exact)).astype(np.int32)
    large = np.minimum(large, NUM_BUCKETS - 1)
    return np.where(dist < max_exact, dist, large).astype(np.int32)


def _block_positions(dilation):
    g = ATTN_MAJOR // dilation
    plen = ATTN_BLOCK // g
    i = np.arange(ATTN_BLOCK)
    return g * (i % plen) + i // plen


def _attn_bias_tables(rel_bias):
    blk = ATTN_BLOCK
    tables = []
    for window, dilation in DILATED_PATTERNS:
        win = window // dilation
        loc = _block_positions(dilation)
        kpos = np.concatenate([loc, loc + blk])
        rel = loc[:, None] - kpos[None, :] + blk
        valid = (rel >= 0) & (rel <= win)
        bucket = _t5_causal_bucket(np.clip(rel, 0, win) * dilation)
        onehot = (jnp.asarray(bucket)[:, :, None] == jnp.arange(NUM_BUCKETS, dtype=jnp.int32)).astype(F32)
        bias = jnp.einsum('icb,bh->hic', onehot, rel_bias.astype(F32), precision=lax.Precision.HIGHEST) * LOG2E
        normal = jnp.where(valid[None], bias, MASK_VALUE)
        first = jnp.where((valid & (kpos >= blk)[None, :])[None], bias, MASK_VALUE)
        tables.append(jnp.stack([normal, first], axis=0))
    return jnp.stack(tables, axis=0)


def _attn_kernel(q_ref, k_ref, v_ref, qw_ref, kw_ref, bias_ref, o_ref,
                 stage, stage2, qs, ks, vs, acc_s, m_s, l_s, *, seq, unroll):
    blk = ATTN_BLOCK
    major = ATTN_MAJOR
    l16 = seq // major
    lane = lax.broadcasted_iota(jnp.int32, (1, LANES), 1)
    head0 = lane < ATTN_HEAD_DIM
    gi = lax.broadcasted_iota(jnp.int32, (LANES, LANES), 0) // ATTN_HEAD_DIM
    gj = lax.broadcasted_iota(jnp.int32, (LANES, LANES), 1) // ATTN_HEAD_DIM
    group_ones = (gi == gj).astype(BF16)
    ones_cols = jnp.ones((2 * blk, LANES), BF16)
    rows = 256

    quarter = seq // 4

    def to_major(dst):
        def pass1(cidx, carry):
            for r_lo in range(4):
                stage2[pl.ds(pl.multiple_of(r_lo * quarter + cidx * rows, rows), rows), :] = (
                    stage[pl.ds(r_lo + 4 * rows * cidx, rows, stride=4), :])
            return carry
        lax.fori_loop(0, quarter // rows, pass1, 0)

        def pass2(r, carry):
            src0 = jnp.bitwise_and(r, 3) * quarter + lax.shift_right_logical(r, 2)
            dst[pl.ds(pl.multiple_of(r * l16, l16), l16), :] = stage2[pl.ds(src0, l16, stride=4), :]
            return carry
        lax.fori_loop(0, major, pass2, 0)

    for src, dst, w_ref, scale in ((q_ref, qs, qw_ref, ATTN_HEAD_DIM ** -0.5 * LOG2E),
                                   (k_ref, ks, kw_ref, 1.0), (v_ref, vs, None, None)):
        def norm_body(c, carry, src=src, w_ref=w_ref, scale=scale):
            sl = pl.ds(pl.multiple_of(c * rows, rows), rows)
            xx = src[sl, :].astype(F32)
            if w_ref is not None:
                hi, lo = _split_bf16(xx * xx)
                ssq = _dot(hi, group_ones) + _dot(lo, group_ones)
                xx = xx * lax.rsqrt(ssq * (1.0 / ATTN_HEAD_DIM) + NORM_EPS) * w_ref[...] * scale
            stage[sl, :] = xx
            return carry
        lax.fori_loop(0, seq // rows, norm_body, 0, unroll=4)
        to_major(dst)

    for p, (_, dil) in enumerate(DILATED_PATTERNS):
        groups = major // dil
        plen = blk // groups
        nb_log2 = ((seq // dil) // blk).bit_length() - 1

        def starts(r_d, n, dil=dil, groups=groups, plen=plen):
            return [pl.multiple_of((j * dil + r_d) * l16 + n * plen, SUBLANES) for j in range(groups)]

        def load_blk(ref, st, plen=plen):
            return jnp.concatenate([ref[pl.ds(s, plen), :] for s in st], axis=0)

        def store_blk(ref, st, val, plen=plen):
            for j, s in enumerate(st):
                ref[pl.ds(s, plen), :] = val[j * plen:(j + 1) * plen]

        def body(it, carry, p=p, nb_log2=nb_log2, starts=starts, load_blk=load_blk, store_blk=store_blk):
            loaded = []
            for u in range(unroll):
                idx = it * unroll + u
                r_d = lax.shift_right_logical(idx, nb_log2)
                n = idx - lax.shift_left(r_d, nb_log2)
                cur = starts(r_d, n)
                prv = starts(r_d, jnp.maximum(n - 1, 0))
                first = (n == 0).astype(jnp.int32)
                qb = load_blk(qs, cur).astype(BF16)
                kw = jnp.concatenate([load_blk(ks, prv), load_blk(ks, cur)], axis=0).astype(BF16)
                vw = jnp.concatenate([load_blk(vs, prv), load_blk(vs, cur)], axis=0).astype(BF16)
                old = (load_blk(m_s, cur), load_blk(l_s, cur), load_blk(acc_s, cur)) if p > 0 else None
                loaded.append((cur, first, qb, kw, vw, old))
            results = []
            for cur, first, qb, kw, vw, old in loaded:
                vw_aug = jnp.concatenate([vw, ones_cols], axis=1)
                ms, res = [], []
                for h, hmask in enumerate((head0, jnp.logical_not(head0))):
                    qh = jnp.where(hmask, qb, jnp.zeros_like(qb))
                    s = _dot_nt(qh, kw) + bias_ref[p, first, h]
                    mh = jnp.max(s, axis=-1, keepdims=True)
                    ms.append(mh)
                    res.append(_dot(jnp.exp2(s - mh).astype(BF16), vw_aug))
                m_b = jnp.where(head0, ms[0], ms[1])
                a_b = jnp.where(head0, res[0][:, :LANES], res[1][:, :LANES])
                l_b = jnp.where(head0, res[0][:, LANES:], res[1][:, LANES:])
                if old is None:
                    results.append((cur, m_b, l_b, a_b))
                else:
                    m_o, l_o, a_o = old
                    m_new = jnp.maximum(m_o, m_b)
                    w_o = jnp.exp2(m_o - m_new)
                    w_b = jnp.exp2(m_b - m_new)
                    results.append((cur, m_new, w_o * l_o + w_b * l_b, w_o * a_o + w_b * a_b))
            for cur, m_new, l_new, a_new in results:
                store_blk(m_s, cur, m_new)
                store_blk(l_s, cur, l_new)
                store_blk(acc_s, cur, a_new)
            return carry

        lax.fori_loop(0, (seq // blk) // unroll, body, 0)

    def from_major(r, carry):
        sl = pl.ds(pl.multiple_of(r * l16, l16), l16)
        dst0 = jnp.bitwise_and(r, 3) * quarter + lax.shift_right_logical(r, 2)
        stage2[pl.ds(dst0, l16, stride=4), :] = acc_s[sl, :] / l_s[sl, :]
        return carry

    lax.fori_loop(0, major, from_major, 0)

    def to_natural(cidx, carry):
        for r_lo in range(4):
            stage[pl.ds(r_lo + 4 * rows * cidx, rows, stride=4), :] = (
                stage2[pl.ds(pl.multiple_of(r_lo * quarter + cidx * rows, rows), rows), :])
        return carry

    lax.fori_loop(0, quarter // rows, to_natural, 0)

    def out_body(c, carry):
        sl = pl.ds(pl.multiple_of(c * rows, rows), rows)
        o_ref[sl, :] = stage[sl, :].astype(o_ref.dtype)
        return carry

    lax.fori_loop(0, seq // rows, out_body, 0)


def _attention(proj, qw2, kw2, bias_tab, batch, seq, attn_width):
    t, n = proj.shape
    proj3 = proj.reshape(batch, seq, n)
    pairs = attn_width // LANES
    npat = len(DILATED_PATTERNS)
    kern = functools.partial(_attn_kernel, seq=seq, unroll=8)
    return pl.pallas_call(
        kern,
        out_shape=jax.ShapeDtypeStruct((batch, seq, attn_width), BF16),
        grid=(batch, pairs),
        in_specs=[
            pl.BlockSpec((None, seq, LANES), lambda b, h: (b, 0, h)),
            pl.BlockSpec((None, seq, LANES), lambda b, h: (b, 0, pairs + h)),
            pl.BlockSpec((None, seq, LANES), lambda b, h: (b, 0, 2 * pairs + h)),
            pl.BlockSpec((1, LANES), lambda b, h: (0, 0)),
            pl.BlockSpec((1, LANES), lambda b, h: (0, 0)),
            pl.BlockSpec((npat, 2, 2, ATTN_BLOCK, 2 * ATTN_BLOCK), lambda b, h: (0, 0, h, 0, 0)),
        ],
        out_specs=pl.BlockSpec((None, seq, LANES), lambda b, h: (b, 0, h)),
        scratch_shapes=[pltpu.VMEM((seq, LANES), F32) for _ in range(8)],
        compiler_params=_cparams(("arbitrary", "arbitrary"), 40),
        name="dilated_attention",
    )(proj3, proj3, proj3, qw2, kw2, bias_tab)


def _mlstm_kernel(q_ref, k_ref, v_ref, og_ref, g_ref, cw_ref, gb_ref, hw_ref, o_ref,
                  x_scr, c_scr, m_scr, *, rows, dk, dv):
    ch = MLSTM_CHUNK
    nheads = MLSTM_HEADS
    qk_w = nheads * dk
    aug = dv + LANES

    @pl.when(pl.program_id(1) == 0)
    def _():
        x_scr[0:SUBLANES, :] = jnp.zeros((SUBLANES, 2 * qk_w), F32)
        c_scr[...] = jnp.zeros_like(c_scr)
        m_scr[...] = jnp.zeros_like(m_scr)

    x_scr[SUBLANES:, 0:qk_w] = q_ref[...].astype(F32)
    x_scr[SUBLANES:, qk_w:] = k_ref[...].astype(F32)

    ri = lax.broadcasted_iota(jnp.int32, (ch, ch), 0)
    ci = lax.broadcasted_iota(jnp.int32, (ch, ch), 1)
    causal = ci <= ri
    tril = causal.astype(BF16)
    row8 = lax.broadcasted_iota(jnp.int32, (SUBLANES, 1), 0)
    ones_col = (lax.broadcasted_iota(jnp.int32, (ch, LANES), 1) == 0).astype(F32)
    k_scale = dk ** -0.5

    def chunk_body(c, carry):
        r0 = pl.multiple_of(c * ch, ch)
        prev8 = x_scr[pl.ds(r0, SUBLANES), :]
        cur = x_scr[pl.ds(r0 + SUBLANES, ch), :]
        conv = cur * cw_ref[CONV_WIDTH - 1:CONV_WIDTH, :]
        for sh in range(1, CONV_WIDTH):
            rolled = pltpu.roll(cur, sh, 0)
            head_rows = jnp.where(row8 < sh, pltpu.roll(prev8, sh, 0), rolled[0:SUBLANES])
            shifted = jnp.concatenate([head_rows, rolled[SUBLANES:]], axis=0)
            conv = conv + shifted * cw_ref[CONV_WIDTH - 1 - sh:CONV_WIDTH - sh, :]
        qk = conv * jax.nn.sigmoid(conv)

        gates = g_ref[pl.ds(r0, ch), :] + gb_ref[...]
        logf = jax.nn.log_sigmoid(gates)
        lf_hi, lf_lo = _split_bf16(logf)
        bcum = _dot(tril, lf_hi) + _dot(tril, lf_lo)
        gates_t = gates.T
        bcum_t = bcum.T

        for h in range(nheads):
            q = qk[:, h * dk:(h + 1) * dk]
            k = qk[:, qk_w + h * dk:qk_w + (h + 1) * dk] * k_scale
            v = v_ref[pl.ds(r0, ch), h * dv:(h + 1) * dv].astype(F32)
            v_aug = jnp.concatenate([v, ones_col], axis=1)
            i_col = gates[:, h:h + 1]
            b_col = bcum[:, nheads + h:nheads + h + 1]
            i_row = gates_t[h:h + 1, :]
            b_row = bcum_t[nheads + h:nheads + h + 1, :]
            m_prev = m_scr[h, 0:1, 0:1]
            c_prev = c_scr[h]

            dlog = jnp.where(causal, b_col - b_row + i_row, -jnp.inf)
            inter = b_col + m_prev
            m_t = jnp.maximum(inter, jnp.max(dlog, axis=-1, keepdims=True))
            qb = q.astype(BF16)
            kb = k.astype(BF16)
            sm = _dot_nt(qb, kb) * jnp.exp(dlog - m_t)
            e_inter = jnp.exp(inter - m_t)
            tot = e_inter * _dot(qb, c_prev.astype(BF16)) + _dot(sm.astype(BF16), v_aug.astype(BF16))
            num = tot[:, 0:dv]
            den = tot[:, dv:dv + 1]
            hcell = num / jnp.maximum(jnp.abs(den), jnp.exp(-m_t))

            g_last = b_col[ch - 1:ch, :]
            a_col = g_last - b_col + i_col
            m_new = jnp.maximum(g_last + m_prev, jnp.max(a_col, axis=0, keepdims=True))
            decay = jnp.exp(g_last + m_prev - m_new)
            w_col = jnp.exp(a_col - m_new)
            c_scr[h] = decay * c_prev + _dot_tn(kb, (w_col * v_aug).astype(BF16))
            m_scr[h] = jnp.broadcast_to(m_new, (SUBLANES, LANES))

            ms = jnp.mean(hcell * hcell, axis=-1, keepdims=True)
            hn = hcell * lax.rsqrt(ms + NORM_EPS) * hw_ref[:, h * dv:(h + 1) * dv]
            og = og_ref[pl.ds(r0, ch), h * dv:(h + 1) * dv].astype(F32)
            o_ref[pl.ds(r0, ch), h * dv:(h + 1) * dv] = (jax.nn.sigmoid(og) * hn).astype(o_ref.dtype)
        return carry

    lax.fori_loop(0, rows // ch, chunk_body, 0, unroll=4)
    x_scr[0:SUBLANES, :] = x_scr[rows:rows + SUBLANES, :]


def _mlstm(proj, gates, conv_w, gate_b_row, h_norm_w, batch, seq, attn_width, dk, dv):
    t, n = proj.shape
    proj3 = proj.reshape(batch, seq, n)
    gates3 = gates.reshape(batch, seq, LANES)
    nheads = MLSTM_HEADS
    qk_w = nheads * dk
    v_w = nheads * dv
    rows = 512
    q_blk = (3 * attn_width) // qk_w
    v_blk = (3 * attn_width + 2 * qk_w) // v_w
    kern = functools.partial(_mlstm_kernel, rows=rows, dk=dk, dv=dv)
    return pl.pallas_call(
        kern,
        out_shape=jax.ShapeDtypeStruct((batch, seq, v_w), BF16),
        grid=(batch, seq // rows),
        in_specs=[
            pl.BlockSpec((None, rows, qk_w), lambda b, j: (b, j, q_blk)),
            pl.BlockSpec((None, rows, qk_w), lambda b, j: (b, j, q_blk + 1)),
            pl.BlockSpec((None, rows, v_w), lambda b, j: (b, j, v_blk)),
            pl.BlockSpec((None, rows, v_w), lambda b, j: (b, j, v_blk + 1)),
            pl.BlockSpec((None, rows, LANES), lambda b, j: (b, j, 0)),
            pl.BlockSpec((CONV_WIDTH, 2 * qk_w), lambda b, j: (0, 0)),
            pl.BlockSpec((1, LANES), lambda b, j: (0, 0)),
            pl.BlockSpec((1, v_w), lambda b, j: (0, 0)),
        ],
        out_specs=pl.BlockSpec((None, rows, v_w), lambda b, j: (b, j, 0)),
        scratch_shapes=[
            pltpu.VMEM((rows + SUBLANES, 2 * qk_w), F32),
            pltpu.VMEM((nheads, dk, dv + LANES), F32),
            pltpu.VMEM((nheads, SUBLANES, LANES), F32),
        ],
        compiler_params=_cparams(("arbitrary", "arbitrary"), 40),
        name="mlstm",
    )(proj3, proj3, proj3, proj3, gates3, conv_w, gate_b_row, h_norm_w)


def _store_token_tiles(ref, base, h):
    m, w2 = h.shape
    w = w2 // 2
    tile_rows = w // LANES
    u = pltpu.bitcast(h.astype(BF16).astype(F32), jnp.uint32)
    packed = (u[:, :w] >> 16) | (u[:, w:] & jnp.uint32(0xFFFF0000))
    for k in range(tile_rows):
        ref[pl.ds(base + k, m, stride=tile_rows), :] = packed[:, k * LANES:(k + 1) * LANES]


def _load_token_tiles(ref, base, m, tile_rows):
    lo, hi = [], []
    for k in range(tile_rows):
        p = ref[pl.ds(base + k, m, stride=tile_rows), :]
        lo.append(pltpu.bitcast(p << 16, F32))
        hi.append(pltpu.bitcast(p & jnp.uint32(0xFFFF0000), F32))
    return jnp.concatenate(lo, axis=1), jnp.concatenate(hi, axis=1)


def _out_proj_kernel(a_ref, m_ref, w_ref, x_ref, mod_ref, nw_ref, wrh_ref, wrl_ref, br_ref,
                     xo_ref, hp_ref, lg_ref, *, attn_width, sub_rows):
    tm, d = x_ref.shape
    tile_rows = d // 2 // LANES
    half = d // 2
    col = 4 * LANES
    for s in range(tm // sub_rows):
        rs = slice(s * sub_rows, (s + 1) * sub_rows)
        a_blk, m_blk = a_ref[rs, :], m_ref[rs, :]
        ssq = jnp.zeros((sub_rows, 1), F32)
        for c in range(d // col):
            cs = slice(c * col, (c + 1) * col)
            mix = _dot(a_blk, w_ref[0:attn_width, cs]) + _dot(m_blk, w_ref[attn_width:, cs])
            xn = x_ref[rs, cs] + mod_ref[2:3, cs] * mix
            xo_ref[rs, cs] = xn
            ssq = ssq + jnp.sum(xn * xn, axis=-1, keepdims=True)
        rstd = lax.rsqrt(ssq * (1.0 / d) + NORM_EPS)
        logits = br_ref[...]
        for k in range(tile_rows):
            parts = []
            for lo in (k * LANES, half + k * LANES):
                ks = slice(lo, lo + LANES)
                hk = xo_ref[rs, ks] * rstd * nw_ref[:, ks] * (1.0 + mod_ref[4:5, ks]) + mod_ref[3:4, ks]
                h_hi, h_lo = _split_bf16(hk)
                logits = logits + (_dot_nt(wrh_ref[:, ks], h_hi) + _dot_nt(wrh_ref[:, ks], h_lo)
                                   + _dot_nt(wrl_ref[:, ks], h_hi))
                parts.append(pltpu.bitcast(h_hi.astype(F32), jnp.uint32))
            packed = (parts[0] >> 16) | (parts[1] & jnp.uint32(0xFFFF0000))
            hp_ref[pl.ds(s * sub_rows * tile_rows + k, sub_rows, stride=tile_rows), :] = packed
        lg_ref[:, rs] = logits


def _out_proj(attn2d, hm2d, w_out, x2d, mod, norm_w, wr_hi, wr_lo, br, seq):
    t, d = x2d.shape
    aw = attn2d.shape[1]
    mw = hm2d.shape[1]
    tm = 512
    blocks_per_seq = seq // tm
    tile_rows = d // 2 // LANES
    kern = functools.partial(_out_proj_kernel, attn_width=aw, sub_rows=512)
    return pl.pallas_call(
        kern,
        out_shape=(jax.ShapeDtypeStruct((t, d), F32),
                   jax.ShapeDtypeStruct((t * tile_rows, LANES), jnp.uint32),
                   jax.ShapeDtypeStruct((ROUTER_ROWS, t), F32)),
        grid=(t // tm,),
        in_specs=[
            pl.BlockSpec((tm, aw), lambda i: (i, 0)),
            pl.BlockSpec((tm, mw), lambda i: (i, 0)),
            pl.BlockSpec((aw + mw, d), lambda i: (0, 0)),
            pl.BlockSpec((tm, d), lambda i: (i, 0)),
            pl.BlockSpec((None, 6, d), lambda i: (i // blocks_per_seq, 0, 0)),
            pl.BlockSpec((1, d), lambda i: (0, 0)),
            pl.BlockSpec((ROUTER_ROWS, d), lambda i: (0, 0)),
            pl.BlockSpec((ROUTER_ROWS, d), lambda i: (0, 0)),
            pl.BlockSpec((ROUTER_ROWS, 1), lambda i: (0, 0)),
        ],
        out_specs=(pl.BlockSpec((tm, d), lambda i: (i, 0)),
                   pl.BlockSpec((tm * tile_rows, LANES), lambda i: (i, 0)),
                   pl.BlockSpec((ROUTER_ROWS, tm), lambda i: (0, i))),
        compiler_params=_cparams(("arbitrary",), 48),
        name="out_proj",
    )(attn2d, hm2d, w_out, x2d, mod, norm_w, wr_hi, wr_lo, br)


def _route_kernel(lg_ref, idx_ref, w_ref):
    ng, ne = MOE_GROUPS, MOE_EXPERTS_PER_GROUP
    lg = [lg_ref[g:g + 1, :] for g in range(ng)]
    best, gsel = lg[0], jnp.zeros_like(lg[0], dtype=jnp.int32)
    for g in range(1, ng):
        better = lg[g] > best
        best = jnp.where(better, lg[g], best)
        gsel = jnp.where(better, g, gsel)
    denom = jnp.exp(lg[0] - best)
    for g in range(1, ng):
        denom = denom + jnp.exp(lg[g] - best)
    p_group = 1.0 / denom

    le = []
    for e in range(ne):
        v = lg_ref[ng + e:ng + e + 1, :]
        for g in range(1, ng):
            v = jnp.where(gsel == g, lg_ref[ng + g * ne + e:ng + g * ne + e + 1, :], v)
        le.append(v)
    v1, i1 = le[0], jnp.zeros_like(gsel)
    for e in range(1, ne):
        better = le[e] > v1
        v1 = jnp.where(better, le[e], v1)
        i1 = jnp.where(better, e, i1)
    v2 = jnp.full_like(v1, -jnp.inf)
    i2 = jnp.zeros_like(gsel)
    for e in range(ne):
        better = jnp.logical_and(le[e] > v2, i1 != e)
        v2 = jnp.where(better, le[e], v2)
        i2 = jnp.where(better, e, i2)
    e2 = jnp.exp(v2 - v1)
    inv = 1.0 / (1.0 + e2)
    tn = gsel.shape[1]
    zero_i = jnp.zeros((SUBLANES - MOE_TOP_K, tn), jnp.int32)
    idx_ref[...] = jnp.concatenate([gsel * ne + i1, gsel * ne + i2, zero_i], axis=0)
    w_rows = jnp.concatenate([p_group * inv, p_group * (e2 * inv), jnp.zeros((LANES - MOE_TOP_K, tn), F32)], axis=0)
    for c in range(tn // LANES):
        w_ref[c * LANES:(c + 1) * LANES, :] = w_rows[:, c * LANES:(c + 1) * LANES].T


def _route(logits_t):
    rows, t = logits_t.shape
    tn = 2048
    return pl.pallas_call(
        _route_kernel,
        out_shape=(jax.ShapeDtypeStruct((SUBLANES, t), jnp.int32),
                   jax.ShapeDtypeStruct((t, LANES), F32)),
        grid=(t // tn,),
        in_specs=[pl.BlockSpec((rows, tn), lambda i: (0, i))],
        out_specs=(pl.BlockSpec((SUBLANES, tn), lambda i: (0, i)),
                   pl.BlockSpec((tn, LANES), lambda i: (i, 0))),
        compiler_params=_cparams(("arbitrary",), 32),
        name="route",
    )(logits_t)


def _dispatch_tables(eidx, t):
    m = MOE_ROWS
    a_total = MOE_TOP_K * t
    n_blocks = a_total // m + N_EXPERTS
    n_rows = n_blocks * m
    src_bits = (a_total - 1).bit_length()
    experts = jnp.arange(N_EXPERTS, dtype=jnp.int32)
    e_flat = eidx[:MOE_TOP_K].reshape(a_total)
    counts = jnp.sum((e_flat[:, None] == experts[None, :]).astype(jnp.int32), axis=0)
    padded = (counts + m - 1) // m * m
    pad_end = jnp.cumsum(padded)
    pad_start = pad_end - padded
    start = jnp.cumsum(counts) - counts
    n_used = (pad_end[-1] // m).astype(jnp.int32)
    packed = lax.sort(e_flat * (1 << src_bits) + jnp.arange(a_total, dtype=jnp.int32))
    order = packed & ((1 << src_bits) - 1)

    def lookup(idx, table):
        return jnp.sum(jnp.where(idx[:, None] == experts[None, :], table[None, :], 0), axis=1)

    blk = jnp.arange(n_blocks, dtype=jnp.int32)
    block_e = jnp.minimum(jnp.sum((blk[:, None] * m >= pad_end[None, :]).astype(jnp.int32), axis=1), N_EXPERTS - 1)
    block_off = blk * m - lookup(block_e, pad_start)
    block_valid = jnp.clip(lookup(block_e, counts) - block_off, 0, m)
    block_valid = jnp.where(blk < n_used, block_valid, 0)
    sorted_pos = (lookup(block_e, start) + block_off)[:, None] + jnp.arange(m, dtype=jnp.int32)[None, :]
    valid = jnp.arange(m, dtype=jnp.int32)[None, :] < block_valid[:, None]
    all_blk = jnp.arange(-1, n_blocks + MOE_TABLE_TAIL + 1, dtype=jnp.int32)
    all_blk = all_blk.at[-1].set(0)
    spare = a_total + jnp.bitwise_and(all_blk, 1)[:, None] * m + jnp.arange(m, dtype=jnp.int32)[None, :]
    row_dst = jnp.where(valid, order[jnp.clip(sorted_pos, 0, a_total - 1)], spare[1:n_blocks + 1])
    row_dst = jnp.concatenate([spare[:1], row_dst, spare[n_blocks + 1:]], axis=0)
    block_e = jnp.where(blk < n_used, block_e, jnp.max(jnp.where(blk < n_used, block_e, 0)))
    block_first = jnp.concatenate([jnp.ones((1,), jnp.int32), (block_e[1:] != block_e[:-1]).astype(jnp.int32)])
    used_e = jnp.where(blk < n_used, block_e, N_EXPERTS)
    later = jnp.where(used_e[None, :] > block_e[:, None], used_e[None, :], N_EXPERTS)
    block_next = jnp.min(later, axis=1)
    block_next = jnp.where(block_next < N_EXPERTS, block_next, -1).astype(jnp.int32)
    return block_e, block_first, block_next, n_used.reshape(1), row_dst


def _moe_kernel(be_ref, first_ref, next_ref, nu_ref, idx_hbm, h_hbm, wg_hbm, wu_hbm, wd_hbm, out_hbm,
                idx_s, xbuf, ybuf, wg_f, wu_f, wd_f, wgb, wub, wdb, sem_i, sem_g, sem_s, sem_w,
                *, tokens, layer, ff_chunk):
    m = MOE_ROWS
    i = pl.program_id(0)
    n_used = nu_ref[0]
    par = lax.rem(i, 2)
    other = 1 - par
    ff = wgb.shape[1]
    half = wgb.shape[0] // 2
    tr = half // LANES
    slot_rows = m * tr
    n_chunks = ff // ff_chunk
    active = i < n_used

    def weight_copies(expert):
        return [pltpu.make_async_copy(src.at[layer, expert], dst, sem_w.at[k])
                for k, (src, dst) in enumerate(((wg_hbm, wg_f), (wu_hbm, wu_f), (wd_hbm, wd_f)))]

    def tile_rows_at(row):
        return pl.ds(pl.multiple_of(row * tr, tr), tr)

    def idx_copy(table_row):
        s = jnp.bitwise_and(table_row, MOE_IDX_SLOTS - 1)
        return pltpu.make_async_copy(idx_hbm.at[table_row], idx_s.at[s], sem_i.at[s])

    def gather_row(buf_half, table_row, j):
        tok = jnp.bitwise_and(idx_s[jnp.bitwise_and(table_row, MOE_IDX_SLOTS - 1), j], tokens - 1)
        pltpu.make_async_copy(h_hbm.at[tile_rows_at(tok)], xbuf.at[tile_rows_at(buf_half * m + j)],
                              sem_g.at[buf_half]).start()

    def scatter_row(buf_half, table_row, j):
        dst = idx_s[jnp.bitwise_and(table_row, MOE_IDX_SLOTS - 1), j]
        pltpu.make_async_copy(ybuf.at[tile_rows_at(buf_half * m + j)], out_hbm.at[tile_rows_at(dst)],
                              sem_s.at[buf_half]).start()

    def half_rows(buf_half):
        return pl.ds(pl.multiple_of(buf_half * slot_rows, slot_rows), slot_rows)

    def wait_gather(buf_half):
        pltpu.make_async_copy(h_hbm.at[pl.ds(0, slot_rows)], xbuf.at[half_rows(buf_half)], sem_g.at[buf_half]).wait()

    def wait_scatter(buf_half):
        pltpu.make_async_copy(ybuf.at[half_rows(buf_half)], out_hbm.at[pl.ds(0, slot_rows)], sem_s.at[buf_half]).wait()

    base = pl.multiple_of(par * slot_rows, slot_rows)

    @pl.when(i == 0)
    def _():
        ybuf[...] = jnp.zeros_like(ybuf)
        for cp in weight_copies(be_ref[0]):
            cp.start(priority=1)
        last_row = idx_hbm.shape[0] - 1
        idx_copy(last_row).start()
        idx_copy(last_row).wait()

        def spare_body(j, carry):
            scatter_row(0, last_row, j)
            return carry
        lax.fori_loop(0, m, spare_body, 0, unroll=8)
        for row in range(3):
            idx_copy(row).start()
        for row in range(3):
            idx_copy(row).wait()
        idx_copy(3).start()

        def body(j, carry):
            gather_row(0, 1, j)
            return carry
        lax.fori_loop(0, m, body, 0, unroll=8)

    @pl.when(jnp.logical_and(active, first_ref[i] == 1))
    def _():
        for cp in weight_copies(be_ref[i]):
            cp.wait()
        rows = 256
        for src, dst in ((wg_f, wgb), (wu_f, wub), (wd_f, wdb)):
            def cast_body(r, carry, src=src, dst=dst):
                sl = pl.ds(pl.multiple_of(r * rows, rows), rows)
                dst[sl, :] = src[sl, :].astype(BF16)
                return carry
            lax.fori_loop(0, src.shape[0] // rows, cast_body, 0)

        @pl.when(next_ref[i] >= 0)
        def _():
            for cp in weight_copies(next_ref[i]):
                cp.start(priority=1)

    @pl.when(jnp.logical_and(active, i >= 1))
    def _():
        idx_copy(i + 2).wait()
        idx_copy(i + 3).start()

    @pl.when(active)
    def _():
        wait_gather(par)
        for j in range(m):
            scatter_row(other, i, j)
        x_lo, x_hi = _load_token_tiles(xbuf, base, m, tr)
        x_lo, x_hi = x_lo.astype(BF16), x_hi.astype(BF16)
        gather_dots = (5 * n_chunks) // 2
        issued = 0

        def dot_with_rows(dot_index, a, b):
            nonlocal issued
            upto = min(m, (m * (dot_index + 1)) // gather_dots)
            for j in range(issued, upto):
                gather_row(other, i + 2, j)
            issued = upto
            return _dot(a, b)

        for c in range(n_chunks):
            cs = slice(c * ff_chunk, (c + 1) * ff_chunk)
            k = 5 * c
            g = dot_with_rows(k, x_lo, wgb[0:half, cs]) + dot_with_rows(k + 1, x_hi, wgb[half:, cs])
            u = dot_with_rows(k + 2, x_lo, wub[0:half, cs]) + dot_with_rows(k + 3, x_hi, wub[half:, cs])
            act = (g * jax.nn.sigmoid(g) * u).astype(BF16)
            part = dot_with_rows(k + 4, act, wdb[cs, :])
            acc = part if c == 0 else acc + part
        wait_scatter(par)
        _store_token_tiles(ybuf, base, acc)

        @pl.when(i == n_used - 1)
        def _():
            wait_scatter(other)

            def body(j, carry):
                scatter_row(par, i + 1, j)
                return carry
            lax.fori_loop(0, m, body, 0, unroll=8)
            wait_scatter(par)
            wait_gather(other)
            idx_copy(i + 3).wait()


def _moe(block_e, block_first, block_next, n_used, row_dst, h_packed, wg, wu, wd, layer):
    d, ff = wg.shape[2], wg.shape[3]
    tr = d // 2 // LANES
    t = h_packed.shape[0] // tr
    m = MOE_ROWS
    n_blocks = block_e.shape[0]
    kern = functools.partial(_moe_kernel, tokens=t, layer=layer, ff_chunk=512)
    grid_spec = pltpu.PrefetchScalarGridSpec(
        num_scalar_prefetch=4,
        grid=(n_blocks,),
        in_specs=[pl.BlockSpec(memory_space=pl.ANY) for _ in range(5)],
        out_specs=pl.BlockSpec(memory_space=pl.ANY),
        scratch_shapes=[
            pltpu.SMEM((MOE_IDX_SLOTS, m), jnp.int32),
            pltpu.VMEM((2 * m * tr, LANES), jnp.uint32),
            pltpu.VMEM((2 * m * tr, LANES), jnp.uint32),
            pltpu.VMEM((d, ff), F32),
            pltpu.VMEM((d, ff), F32),
            pltpu.VMEM((ff, d), F32),
            pltpu.VMEM((d, ff), BF16),
            pltpu.VMEM((d, ff), BF16),
            pltpu.VMEM((ff, d), BF16),
            pltpu.SemaphoreType.DMA((MOE_IDX_SLOTS,)),
            pltpu.SemaphoreType.DMA((2,)),
            pltpu.SemaphoreType.DMA((2,)),
            pltpu.SemaphoreType.DMA((3,)),
        ],
    )
    return pl.pallas_call(
        kern,
        out_shape=jax.ShapeDtypeStruct(((MOE_TOP_K * t + 2 * m) * tr, LANES), jnp.uint32),
        grid_spec=grid_spec,
        compiler_params=_cparams(("arbitrary",), 58),
        name="moe_experts",
    )(block_e, block_first, block_next, n_used, row_dst, h_packed, wg, wu, wd)


def _combine_kernel(x_ref, y0_ref, y1_ref, w_ref, mod_ref, o_ref):
    tm, d = x_ref.shape
    half = d // 2
    tr = half // LANES
    w0 = w_ref[:, 0:1]
    w1 = w_ref[:, 1:2]
    lo0, hi0 = _load_token_tiles(y0_ref, 0, tm, tr)
    lo1, hi1 = _load_token_tiles(y1_ref, 0, tm, tr)
    o_ref[:, 0:half] = x_ref[:, 0:half] + mod_ref[5:6, 0:half] * (w0 * lo0 + w1 * lo1)
    o_ref[:, half:] = x_ref[:, half:] + mod_ref[5:6, half:] * (w0 * hi0 + w1 * hi1)


def _combine(x2d, y, w_tok, mod, seq):
    t, d = x2d.shape
    tm = 512
    tr = d // 2 // LANES
    blocks_per_seq = seq // tm
    nblk = t // tm
    return pl.pallas_call(
        _combine_kernel,
        out_shape=jax.ShapeDtypeStruct((t, d), F32),
        grid=(nblk,),
        in_specs=[
            pl.BlockSpec((tm, d), lambda i: (i, 0)),
            pl.BlockSpec((tm * tr, LANES), lambda i: (i, 0)),
            pl.BlockSpec((tm * tr, LANES), lambda i: (nblk + i, 0)),
            pl.BlockSpec((tm, LANES), lambda i: (i, 0)),
            pl.BlockSpec((None, 6, d), lambda i: (i // blocks_per_seq, 0, 0)),
        ],
        out_specs=pl.BlockSpec((tm, d), lambda i: (i, 0)),
        compiler_params=_cparams(("arbitrary",), 48),
        name="combine",
    )(x2d, y, y, w_tok, mod)


def kernel(x, c, rel_bias, w_mod, b_mod, norm1_w, norm2_w, w_in, q_norm_w, k_norm_w, conv_w, gate_b, h_norm_w, w_out, router_group_w, router_group_b, router_expert_w, router_expert_b, w_gate, w_up, w_down):
    batch, seq, d = x.shape
    depth = w_mod.shape[0]
    t = batch * seq
    attn_width = d // 2
    mlstm_width = d - attn_width
    dv = mlstm_width // MLSTM_HEADS
    dk = dv // 2
    n_main = 3 * attn_width + 2 * MLSTM_HEADS * dk + 2 * mlstm_width
    assert dk == LANES and attn_width % LANES == 0 and seq % 1024 == 0 and t % MOE_ROWS == 0
    assert t & (t - 1) == 0, "the MoE row tables recover the token of assignment k * t + token with a bit mask"
    assert all((seq // dil) % ATTN_BLOCK == 0 and window // dil == ATTN_BLOCK for window, dil in DILATED_PATTERNS)

    mod_all = _modulation(c, w_mod, b_mod)
    bias_tab = _attn_bias_tables(rel_bias)
    x2d = x.reshape(t, d)

    for l in range(depth):
        mod = mod_all[l]
        w_main = w_in[l, :, :n_main].astype(BF16)
        w_gates = jnp.zeros((d, LANES), F32).at[:, :2 * MLSTM_HEADS].set(w_in[l, :, n_main:])
        proj, gates = _in_proj(x2d, mod, norm1_w[l].reshape(1, d), w_main, w_gates.astype(BF16), seq)

        qw2 = jnp.tile(q_norm_w[l], LANES // ATTN_HEAD_DIM).reshape(1, LANES)
        kw2 = jnp.tile(k_norm_w[l], LANES // ATTN_HEAD_DIM).reshape(1, LANES)
        attn = _attention(proj, qw2, kw2, bias_tab, batch, seq, attn_width)

        gate_b_row = jnp.zeros((1, LANES), F32).at[0, :2 * MLSTM_HEADS].set(gate_b[l])
        hm = _mlstm(proj, gates, conv_w[l], gate_b_row, h_norm_w[l].reshape(1, mlstm_width),
                    batch, seq, attn_width, dk, dv)

        w_r = jnp.concatenate([router_group_w[l], router_expert_w[l]], axis=1).T
        w_r = jnp.zeros((ROUTER_ROWS, d), F32).at[:w_r.shape[0]].set(w_r)
        wr_hi, wr_lo = _split_bf16(w_r)
        b_r = jnp.concatenate([router_group_b[l], router_expert_b[l]])
        b_r = jnp.zeros((ROUTER_ROWS, 1), F32).at[:b_r.shape[0], 0].set(b_r)
        x2d, h_packed, logits_t = _out_proj(
            attn.reshape(t, attn_width), hm.reshape(t, mlstm_width), w_out[l].astype(BF16),
            x2d, mod, norm2_w[l].reshape(1, d), wr_hi, wr_lo, b_r, seq)

        eidx, w_tok = _route(logits_t)
        block_e, block_first, block_next, n_used, row_dst = _dispatch_tables(eidx, t)
        y = _moe(block_e, block_first, block_next, n_used, row_dst, h_packed, w_gate, w_up, w_down, l)
        x2d = _combine(x2d, y, w_tok, mod, seq)

    return x2d.reshape(batch, seq, d)
```
